```python
import jax
import jax.numpy as jnp
from jax import lax
import numpy as np

D_MODEL = 2048
BATCH = 4
SEQ = 4096
DEPTH = 4

GRID_W = 64
CTX_LEN = 256
N_MIXERS = 2
N_RWKV = (DEPTH + N_MIXERS - 1) // N_MIXERS
N_NAT = DEPTH // N_MIXERS
D_FF = ((8 * D_MODEL // 3 + 255) // 256) * 256
N_MOD = 9
MACARON_WEIGHT = 0.5
RMS_EPS = 1e-6
RWKV_HEAD = 64
RWKV_HEADS = D_MODEL // RWKV_HEAD
RWKV_DECAY_LORA = max(32, int(round(1.8 * D_MODEL ** 0.5 / 32)) * 32)
RWKV_AAA_LORA = max(32, int(round(1.8 * D_MODEL ** 0.5 / 32)) * 32)
RWKV_MV_LORA = max(32, int(round(1.3 * D_MODEL ** 0.5 / 32)) * 32)
RWKV_GATE_LORA = max(32, int(round(0.6 * D_MODEL ** 0.8 / 32)) * 32)
RWKV_GN_EPS = 64e-5
L2_EPS_SQ = 1e-24
NAT_HEAD_DIM = 64
NAT_HEADS = D_MODEL // NAT_HEAD_DIM
WIN_H = 8
WIN_W = 16
MASK_VALUE = -1e30

kernel_name = "hybrid_rwkv7_natten_macaron_dit"


def _rms_norm(x, g):
    xf = x.astype(jnp.float32)
    y = xf * lax.rsqrt(jnp.mean(xf * xf, axis=-1, keepdims=True) + RMS_EPS)
    return y.astype(x.dtype) * g


def _modulated_input(x, g_pre, shift, scale):
    return _rms_norm(x, g_pre) * (1 + scale) + shift


def _gated_residual(x, y, g_post, gate, weight):
    return x + weight * gate * _rms_norm(y, g_post)


def _swiglu(h, w_gate, w_up, w_down):
    return (jax.nn.silu(h @ w_gate) * (h @ w_up)) @ w_down


def _ffn_sublayer(x, mod, slot, g_pre, g_post, w_gate, w_up, w_down):
    h = _modulated_input(x, g_pre, mod[..., 3 * slot, :], mod[..., 3 * slot + 1, :])
    return _gated_residual(x, _swiglu(h, w_gate, w_up, w_down), g_post, mod[..., 3 * slot + 2, :], MACARON_WEIGHT)


def _centred_shift(h):
    prev = jnp.pad(h[:, :-1], ((0, 0), (1, 0), (0, 0)))
    nxt = jnp.pad(h[:, 1:], ((0, 0), (0, 1), (0, 0)))
    return 0.5 * (prev + nxt) - h


def _heads(t):
    return t.reshape(t.shape[:-1] + (RWKV_HEADS, RWKV_HEAD))


def _rwkv_prep(h, v_first, v_res, mix, w_r, w_k, w_v, w0, w1, w2, a0, a1, a2, k_k, k_a, g1, g2):
    xx = _centred_shift(h)
    xr, xw, xk, xv, xa, xg = (h + xx * mix[m] for m in range(6))
    r = xr @ w_r
    k = xk @ w_k
    v = xv @ w_v
    if v_res is not None:
        v0, v1, v2 = v_res
        v = v + (v_first - v) * jax.nn.sigmoid(v0 + (xv @ v1) @ v2)
    lora_w = jnp.einsum('zbtl,zld->zbtd', jnp.tanh(jnp.einsum('btd,zdl->zbtl', xw, w1)), w2)
    w_log = -jax.nn.softplus(-(w0[:, None, None, :] + lora_w)) - 0.5
    decay = jnp.exp(-jnp.exp(w_log.astype(jnp.float32)))
    a = jax.nn.sigmoid(a0[:, None, None, :] + jnp.einsum('zbtl,zld->zbtd', jnp.einsum('btd,zdl->zbtl', xa, a1), a2))
    kk = _heads(k * k_k).astype(jnp.float32)
    kk = kk * lax.rsqrt(jnp.maximum(jnp.sum(kk * kk, axis=-1, keepdims=True), L2_EPS_SQ))
    k_dir = k[None] * (1 + (a - 1) * k_a)
    g = jax.nn.sigmoid(xg @ g1) @ g2
    return r, decay, k_dir, v, kk, a, g


def _wkv_step(state, inp):
    r, w, k, v, a, b = inp
    sa = jnp.einsum('bhij,bhj->bhi', state, a)
    state = state * w[:, :, None, :] + sa[..., None] * b[:, :, None, :] + v[..., None] * k[:, :, None, :]
    return state, jnp.einsum('bhij,bhj->bhi', state, r)


def _wkv_scan(state, r, w, k, v, a, b, reverse):
    xs = tuple(jnp.swapaxes(t.astype(jnp.float32), 0, 1) for t in (r, w, k, v, a, b))
    state, y = lax.scan(_wkv_step, state, xs, reverse=reverse)
    return state, jnp.swapaxes(y, 0, 1)


def _rwkv_out(wkv, r, k_dir, v, g, r_k, gn_w, gn_b, w_o):
    B, T = wkv.shape[:2]
    mu = jnp.mean(wkv, axis=-1, keepdims=True)
    var = jnp.mean(jnp.square(wkv - mu), axis=-1, keepdims=True)
    o = ((wkv - mu) * lax.rsqrt(var + RWKV_GN_EPS)).reshape(B, T, D_MODEL).astype(r.dtype) * gn_w + gn_b
    coef = jnp.einsum('bthn,zbthn,hn->bth', _heads(r), _heads(k_dir), r_k)
    bonus = (coef[..., None] * _heads(v)).reshape(B, T, D_MODEL)
    return ((o + bonus) * g) @ w_o


def _rwkv7_bidir(h_lat, h_ctx, vf_lat, vf_ctx, v_res, need_ctx_out, mix, w_r, w_k, w_v, w_o,
                 w0, w1, w2, a0, a1, a2, k_k, k_a, r_k, g1, g2, gn_w, gn_b):
    proj = (mix, w_r, w_k, w_v, w0, w1, w2, a0, a1, a2, k_k, k_a, g1, g2)
    seqs = (_rwkv_prep(h_ctx, vf_ctx, v_res, *proj), _rwkv_prep(h_lat, vf_lat, v_res, *proj))
    state0 = jnp.zeros((h_lat.shape[0], RWKV_HEADS, RWKV_HEAD, RWKV_HEAD), jnp.float32)
    wkv = [None, None]
    for z, reverse in enumerate((False, True)):
        state = state0
        for s, (r, decay, k_dir, v, kk, a, _) in enumerate(seqs):
            state, y = _wkv_scan(state, _heads(r), _heads(decay[z]), _heads(k_dir[z]), _heads(v),
                                 -kk, kk * _heads(a[z]), reverse)
            wkv[s] = y if wkv[s] is None else wkv[s] + y
    r_l, _, kd_l, v_l, _, _, g_l = seqs[1]
    y_lat = _rwkv_out(wkv[1], r_l, kd_l, v_l, g_l, r_k, gn_w, gn_b, w_o)
    y_ctx = None
    if need_ctx_out:
        r_c, _, kd_c, v_c, _, _, g_c = seqs[0]
        y_ctx = _rwkv_out(wkv[0], r_c, kd_c, v_c, g_c, r_k, gn_w, gn_b, w_o)
    return y_lat, y_ctx, v_l, seqs[0][3]


def _neighbourhood_attention(h_lat, h_ctx, need_ctx_out, w_qkv, w_o, rpb):
    B, T, D = h_lat.shape
    n_ctx = h_ctx.shape[1]
    rows = T // GRID_W
    kh = min(WIN_H, rows)
    n_lat = kh * GRID_W
    scale = NAT_HEAD_DIM ** -0.5
    qkv = (h_lat @ w_qkv).reshape(B, rows, GRID_W, 3, NAT_HEADS, NAT_HEAD_DIM)
    q, k, v = qkv[:, :, :, 0], qkv[:, :, :, 1], qkv[:, :, :, 2]
    qkv_c = (h_ctx @ w_qkv).reshape(B, n_ctx, 3, NAT_HEADS, NAT_HEAD_DIM)
    q_c, k_c, v_c = qkv_c[:, :, 0], qkv_c[:, :, 1], qkv_c[:, :, 2]
    cols = jnp.arange(GRID_W)
    col_start = jnp.clip(cols - WIN_W // 2, 0, GRID_W - WIN_W)
    col_ok = (cols[None, :] >= col_start[:, None]) & (cols[None, :] < col_start[:, None] + WIN_W)
    dc_idx = jnp.clip(cols[None, :] - cols[:, None] + WIN_W - 1, 0, 2 * WIN_W - 2)
    rpb_cols = rpb[:, :, dc_idx].astype(jnp.float32)

    def row_block(r):
        r0 = jnp.clip(r - kh // 2, 0, rows - kh)
        q_r = lax.dynamic_index_in_dim(q, r, axis=1, keepdims=False)
        k_r = lax.dynamic_slice_in_dim(k, r0, kh, axis=1)
        v_r = lax.dynamic_slice_in_dim(v, r0, kh, axis=1)
        bias = jnp.take(rpb_cols, r0 + jnp.arange(kh) - r + WIN_H - 1, axis=1)
        s = jnp.einsum('bqhd,bkwhd->bhqkw', q_r, k_r).astype(jnp.float32) * scale + jnp.transpose(bias, (0, 2, 1, 3))
        s = jnp.where(col_ok[:, None, :], s, MASK_VALUE)
        s_c = jnp.einsum('bqhd,bchd->bhqc', q_r, k_c).astype(jnp.float32) * scale
        p = jax.nn.softmax(jnp.concatenate([s.reshape(B, NAT_HEADS, GRID_W, n_lat), s_c], axis=-1), axis=-1).astype(v.dtype)
        p_lat = p[..., :n_lat].reshape(B, NAT_HEADS, GRID_W, kh, GRID_W)
        return (jnp.einsum('bhqkw,bkwhd->bqhd', p_lat, v_r)
                + jnp.einsum('bhqc,bchd->bqhd', p[..., n_lat:], v_c))

    o = lax.map(row_block, jnp.arange(rows))
    y_lat = jnp.moveaxis(o, 0, 1).reshape(B, T, D) @ w_o
    y_ctx = None
    if need_ctx_out:
        s_cc = jnp.einsum('bqhd,bkhd->bhqk', q_c, k_c).astype(jnp.float32) * scale
        p_cc = jax.nn.softmax(s_cc, axis=-1).astype(v_c.dtype)
        y_ctx = jnp.einsum('bhqk,bkhd->bqhd', p_cc, v_c).reshape(B, n_ctx, D) @ w_o
    return y_lat, y_ctx


def setup_inputs(seed: int = 0) -> dict:
    key = jax.random.key(seed)
    ks = iter(jax.random.split(key, 40))
    D, F = D_MODEL, D_FF

    def nrm(shape, scale):
        return jax.random.normal(next(ks), shape, jnp.float32) * scale

    def unif(shape, lo, hi):
        return jax.random.uniform(next(ks), shape, jnp.float32, minval=lo, maxval=hi)

    return {
        "x": nrm((BATCH, SEQ, D), 1.0),
        "c": nrm((BATCH, D), 1.0),
        "ctx": nrm((BATCH, CTX_LEN, D), 1.0),
        "c_ctx": nrm((D,), 1.0),
        "ada_w": nrm((DEPTH, D, N_MOD * D), 0.5 * D ** -0.5),
        "ada_b": nrm((DEPTH, N_MOD * D), 0.01),
        "norm_pre": 1.0 + nrm((DEPTH, 3, D), 0.02),
        "norm_post": 1.0 + nrm((DEPTH, 3, D), 0.02),
        "ffn_w_gate": nrm((DEPTH, 2, D, F), D ** -0.5),
        "ffn_w_up": nrm((DEPTH, 2, D, F), D ** -0.5),
        "ffn_w_down": nrm((DEPTH, 2, F, D), F ** -0.5),
        "rwkv_mix": unif((N_RWKV, 6, D), 0.0, 1.0),
        "rwkv_w_r": nrm((N_RWKV, D, D), D ** -0.5),
        "rwkv_w_k": nrm((N_RWKV, D, D), D ** -0.5),
        "rwkv_w_v": nrm((N_RWKV, D, D), D ** -0.5),
        "rwkv_w_o": nrm((N_RWKV, D, D), D ** -0.5),
        "rwkv_w0": unif((N_RWKV, 2, D), -6.0, -1.0),
        "rwkv_w1": nrm((N_RWKV, 2, D, RWKV_DECAY_LORA), D ** -0.5),
        "rwkv_w2": nrm((N_RWKV, 2, RWKV_DECAY_LORA, D), 0.1 * RWKV_DECAY_LORA ** -0.5),
        "rwkv_a0": nrm((N_RWKV, 2, D), 0.1),
        "rwkv_a1": nrm((N_RWKV, 2, D, RWKV_AAA_LORA), D ** -0.5),
        "rwkv_a2": nrm((N_RWKV, 2, RWKV_AAA_LORA, D), RWKV_AAA_LORA ** -0.5),
        "rwkv_v0": nrm((N_RWKV - 1, D), 0.1),
        "rwkv_v1": nrm((N_RWKV - 1, D, RWKV_MV_LORA), D ** -0.5),
        "rwkv_v2": nrm((N_RWKV - 1, RWKV_MV_LORA, D), RWKV_MV_LORA ** -0.5),
        "rwkv_k_k": 0.85 + nrm((N_RWKV, D), 0.02),
        "rwkv_k_a": 1.0 + nrm((N_RWKV, D), 0.02),
        "rwkv_r_k": nrm((N_RWKV, RWKV_HEADS, RWKV_HEAD), 0.1),
        "rwkv_g1": nrm((N_RWKV, D, RWKV_GATE_LORA), D ** -0.5),
        "rwkv_g2": nrm((N_RWKV, RWKV_GATE_LORA, D), RWKV_GATE_LORA ** -0.5),
        "rwkv_gn_w": 1.0 + nrm((N_RWKV, D), 0.02),
        "rwkv_gn_b": nrm((N_RWKV, D), 0.01),
        "nat_w_qkv": nrm((N_NAT, D, 3 * D), D ** -0.5),
        "nat_w_o": nrm((N_NAT, D, D), D ** -0.5),
        "nat_rpb": nrm((N_NAT, NAT_HEADS, 2 * WIN_H - 1, 2 * WIN_W - 1), 0.1),
    }


def reference(x, c, ctx, c_ctx, ada_w, ada_b, norm_pre, norm_post, ffn_w_gate, ffn_w_up, ffn_w_down,
              rwkv_mix, rwkv_w_r, rwkv_w_k, rwkv_w_v, rwkv_w_o, rwkv_w0, rwkv_w1, rwkv_w2,
              rwkv_a0, rwkv_a1, rwkv_a2, rwkv_v0, rwkv_v1, rwkv_v2, rwkv_k_k, rwkv_k_a, rwkv_r_k,
              rwkv_g1, rwkv_g2, rwkv_gn_w, rwkv_gn_b, nat_w_qkv, nat_w_o, nat_rpb):
    B = x.shape[0]
    silu_c = jax.nn.silu(c)
    silu_cc = jax.nn.silu(c_ctx)[None]
    xc = ctx
    vf_lat = vf_ctx = None
    for i in range(DEPTH):
        last = i == DEPTH - 1
        j = i // N_MIXERS
        m_lat = (silu_c @ ada_w[i] + ada_b[i]).reshape(B, 1, N_MOD, D_MODEL)
        m_ctx = (silu_cc @ ada_w[i] + ada_b[i]).reshape(1, 1, N_MOD, D_MODEL)
        x = _ffn_sublayer(x, m_lat, 0, norm_pre[i, 0], norm_post[i, 0], ffn_w_gate[i, 0], ffn_w_up[i, 0], ffn_w_down[i, 0])
        xc = _ffn_sublayer(xc, m_ctx, 0, norm_pre[i, 0], norm_post[i, 0], ffn_w_gate[i, 0], ffn_w_up[i, 0], ffn_w_down[i, 0])
        h = _modulated_input(x, norm_pre[i, 1], m_lat[..., 3, :], m_lat[..., 4, :])
        hc = _modulated_input(xc, norm_pre[i, 1], m_ctx[..., 3, :], m_ctx[..., 4, :])
        if i % N_MIXERS == 0:
            v_res = None if j == 0 else (rwkv_v0[j - 1], rwkv_v1[j - 1], rwkv_v2[j - 1])
            y, yc, v_lat, v_ctx = _rwkv7_bidir(
                h, hc, vf_lat, vf_ctx, v_res, not last, rwkv_mix[j], rwkv_w_r[j], rwkv_w_k[j], rwkv_w_v[j],
                rwkv_w_o[j], rwkv_w0[j], rwkv_w1[j], rwkv_w2[j], rwkv_a0[j], rwkv_a1[j], rwkv_a2[j],
                rwkv_k_k[j], rwkv_k_a[j], rwkv_r_k[j], rwkv_g1[j], rwkv_g2[j], rwkv_gn_w[j], rwkv_gn_b[j])
            if j == 0:
                vf_lat, vf_ctx = v_lat, v_ctx
        else:
            y, yc = _neighbourhood_attention(h, hc, not last, nat_w_qkv[j], nat_w_o[j], nat_rpb[j])
        x = _gated_residual(x, y, norm_post[i, 1], m_lat[..., 5, :], 1.0)
        x = _ffn_sublayer(x, m_lat, 2, norm_pre[i, 2], norm_post[i, 2], ffn_w_gate[i, 1], ffn_w_up[i, 1], ffn_w_down[i, 1])
        if not last:
            xc = _gated_residual(xc, yc, norm_post[i, 1], m_ctx[..., 5, :], 1.0)
            xc = _ffn_sublayer(xc, m_ctx, 2, norm_pre[i, 2], norm_post[i, 2], ffn_w_gate[i, 1], ffn_w_up[i, 1], ffn_w_down[i, 1])
    return x
```

```python
import functools

import jax
import jax.numpy as jnp
from jax import lax
from jax.experimental import pallas as pl
from jax.experimental.pallas import tpu as pltpu

F32 = jnp.float32
BF16 = jnp.bfloat16

LANES = 128
HEAD = 64
N_MOD = 9
MACARON_WEIGHT = 0.5
RMS_EPS = 1e-6
RWKV_GN_EPS = 64e-5
L2_EPS_SQ = 1e-24
GRID_W = 64
WIN_H = 8
WIN_W = 16
MASK_VALUE = -1e30
VMEM_LIMIT = 56 * 1024 * 1024


def _cparams(sem):
    return pltpu.CompilerParams(dimension_semantics=sem, vmem_limit_bytes=VMEM_LIMIT)


def _dot(a, b):
    return jnp.dot(a, b, preferred_element_type=F32)


def _dot_nt(a, b):
    return lax.dot_general(a, b, (((1,), (1,)), ((), ())), preferred_element_type=F32)


def _dot_tn(a, b):
    return lax.dot_general(a, b, (((0,), (0,)), ((), ())), preferred_element_type=F32)


def _split3(x):
    hi = x.astype(BF16)
    r1 = x - hi.astype(F32)
    mid = r1.astype(BF16)
    lo = (r1 - mid.astype(F32)).astype(BF16)
    return hi, mid, lo


def _dot_exact_lhs01(m01, x):
    hi, mid, lo = _split3(x)
    return _dot(m01, hi) + _dot(m01, mid) + _dot(m01, lo)


def _dot_hp(a, b):
    ah = a.astype(BF16)
    al = (a - ah.astype(F32)).astype(BF16)
    bh = b.astype(BF16)
    bl = (b - bh.astype(F32)).astype(BF16)
    return _dot(ah, bh) + (_dot(ah, bl) + _dot(al, bh))


def _wkv_kernel(r_ref, v_ref, kk_ref, lw_ref, kd_ref, a_ref, s0_ref, y_ref, sfin_ref, s_scr,
                *, chunk, pairs):
    z = pl.program_id(0)
    c = pl.program_id(3)
    C = chunk

    @pl.when(c == 0)
    def _():
        s_scr[...] = s0_ref[...]

    row = lax.broadcasted_iota(jnp.int32, (C, C), 0)
    col = lax.broadcasted_iota(jnp.int32, (C, C), 1)
    sgn = 1 - 2 * z
    d = (row - col) * sgn
    strict = d > 0
    incl = d >= 0
    incl_bf = incl.astype(BF16)
    lane = lax.broadcasted_iota(jnp.int32, (1, LANES), 1)
    head_masks = (lane < HEAD, lane >= HEAD)
    srow = lax.broadcasted_iota(jnp.int32, (LANES, LANES), 0)
    scol = lax.broadcasted_iota(jnp.int32, (LANES, LANES), 1)
    blockdiag = (srow < HEAD) == (scol < HEAD)
    eye = (row == col).astype(F32)
    levels = []
    s = 1
    while s < C:
        levels.append((row // (2 * s) == col // (2 * s)) & (row // s != col // s))
        s *= 2

    for p in range(pairs):
        sl = slice(p * LANES, (p + 1) * LANES)
        r = r_ref[:, sl]
        v = v_ref[:, sl]
        kk = kk_ref[:, sl]
        lw = lw_ref[:, sl]
        kd = kd_ref[:, sl]
        al = a_ref[:, sl]
        S = s_scr[p]

        a_vec = -kk
        b_vec = kk * al
        L = _dot_exact_lhs01(incl_bf, lw)
        l_end = jnp.sum(lw, axis=0, keepdims=True)
        p_rem = jnp.exp(l_end - L)
        p_inv = jnp.exp(-L)
        at = (a_vec * jnp.exp(L - lw))
        rt = (r * jnp.exp(L))
        bt = (b_vec * p_inv).astype(BF16)
        kt = (kd * p_inv).astype(BF16)
        bh = (b_vec * p_rem).astype(BF16)
        kh = (kd * p_rem).astype(BF16)
        v_bf = v.astype(BF16)
        s_bf = S.astype(BF16)
        at_s = _dot_nt(at.astype(BF16), s_bf)
        rt_s = _dot_nt(rt.astype(BF16), s_bf)

        us = []
        ys = []
        for h in range(2):
            mh = head_masks[h]
            at_h = jnp.where(mh, at, 0.0).astype(BF16)
            rt_h = jnp.where(mh, rt, 0.0).astype(BF16)
            a_ab = jnp.where(strict, _dot_nt(at_h, bt), 0.0)
            a_ak = jnp.where(strict, _dot_nt(at_h, kt), 0.0)
            r_b = jnp.where(incl, _dot_nt(rt_h, bt), 0.0)
            r_k = jnp.where(incl, _dot_nt(rt_h, kt), 0.0)
            t = eye + jnp.where(levels[0], a_ab, 0.0)
            for lm in levels[1:]:
                x = jnp.where(lm, a_ab, 0.0).astype(BF16)
                t_bf = t.astype(BF16)
                t = t + _dot(t_bf, _dot(x, t_bf).astype(BF16))
            rhs = at_s + _dot(a_ak.astype(BF16), v_bf)
            u = _dot(t.astype(BF16), rhs.astype(BF16))
            y = rt_s + _dot(r_b.astype(BF16), u.astype(BF16)) + _dot(r_k.astype(BF16), v_bf)
            us.append(u)
            ys.append(y)
        u = jnp.where(head_masks[0], us[0], us[1])
        y = jnp.where(head_masks[0], ys[0], ys[1])
        y_ref[:, sl] = y
        upd = _dot_tn(u.astype(BF16), bh) + _dot_tn(v_bf, kh)
        s_new = S * jnp.exp(l_end) + jnp.where(blockdiag, upd, 0.0)
        s_scr[p] = s_new

    @pl.when(c == pl.num_programs(3) - 1)
    def _():
        sfin_ref[...] = s_scr[...]


def _wkv_scan(r, v, kk, lw, kd, a, s0, *, chunk=64, pairs=2):
    B, T, D = r.shape
    lw_lanes = pairs * LANES
    assert T % chunk == 0 and D % lw_lanes == 0
    nc = T // chunk
    ng = D // lw_lanes

    def cidx(z, c):
        return c + z * (nc - 1 - 2 * c)

    tok_spec = pl.BlockSpec((None, chunk, lw_lanes), lambda z, b, g, c: (b, cidx(z, c), g))
    dir_spec = pl.BlockSpec((None, None, chunk, lw_lanes), lambda z, b, g, c: (z, b, cidx(z, c), g))
    st_spec = pl.BlockSpec((None, None, pairs, LANES, LANES), lambda z, b, g, c: (z, b, g, 0, 0))
    y, s_fin = pl.pallas_call(
        functools.partial(_wkv_kernel, chunk=chunk, pairs=pairs),
        grid=(2, B, ng, nc),
        in_specs=[tok_spec, tok_spec, tok_spec, dir_spec, dir_spec, dir_spec, st_spec],
        out_specs=[dir_spec, st_spec],
        out_shape=[jax.ShapeDtypeStruct((2, B, T, D), F32),
                   jax.ShapeDtypeStruct(s0.shape, F32)],
        scratch_shapes=[pltpu.VMEM((pairs, LANES, LANES), F32)],
        compiler_params=_cparams(("arbitrary", "arbitrary", "arbitrary", "arbitrary")),
        name="wkv_scan",
    )(r, v, kk, lw, kd, a, s0)
    return y, s_fin


def _rms(x, g):
    return x * lax.rsqrt(jnp.mean(x * x, axis=-1, keepdims=True) + RMS_EPS) * g


def _tile(n, want, unit):
    if n <= want:
        return n
    t = (want // unit) * unit
    while n % t:
        t -= unit
    return t


def _mod_spec(base, rows_per_group, tm):
    assert rows_per_group % tm == 0
    return lambda i, *_: (base + (i * tm) // rows_per_group, 0, 0)


def _adaln_kernel(c_ref, w_ref, b_ref, o_ref):
    cv = c_ref[...]
    cv = cv * jax.nn.sigmoid(cv)
    o_ref[...] = _dot(cv.astype(BF16), w_ref[...].astype(BF16)) + b_ref[...]


def _adaln(cvec, ada_w, ada_b, *, bn=1024):
    depth, D, N = ada_w.shape
    bn = _tile(N, bn, LANES)
    return pl.pallas_call(
        _adaln_kernel,
        grid=(depth, N // bn),
        in_specs=[pl.BlockSpec((8, D), lambda i, n: (0, 0)),
                  pl.BlockSpec((None, D, bn), lambda i, n: (i, 0, n)),
                  pl.BlockSpec((None, 1, bn), lambda i, n: (i, 0, n))],
        out_specs=pl.BlockSpec((None, 8, bn), lambda i, n: (i, 0, n)),
        out_shape=jax.ShapeDtypeStruct((depth, 8, N), F32),
        compiler_params=_cparams(("arbitrary", "arbitrary")),
        name="adaln",
    )(cvec, ada_w, ada_b.reshape(depth, 1, N))


def _ffn_kernel(x_ref, mod_ref, gpre_ref, gpost_ref, wg_ref, wu_ref, wd_ref, o_ref, h_scr, acc_scr, *, slot):
    f = pl.program_id(1)

    @pl.when(f == 0)
    def _():
        shift = mod_ref[3 * slot:3 * slot + 1, :]
        scale = mod_ref[3 * slot + 1:3 * slot + 2, :]
        h = _rms(x_ref[...], gpre_ref[...]) * (1 + scale) + shift
        h_scr[...] = h.astype(BF16)
        acc_scr[...] = jnp.zeros_like(acc_scr)

    h = h_scr[...]
    g = _dot(h, wg_ref[...])
    u = _dot(h, wu_ref[...])
    a = (g * jax.nn.sigmoid(g)) * u
    acc_scr[...] += _dot(a.astype(BF16), wd_ref[...])

    @pl.when(f == pl.num_programs(1) - 1)
    def _():
        gate = mod_ref[3 * slot + 2:3 * slot + 3, :]
        o_ref[...] = x_ref[...] + MACARON_WEIGHT * gate * _rms(acc_scr[...], gpost_ref[...])


def _ffn(x, mod, grp, slot, g_pre, g_post, wg, wu, wd, layer, which, *, tm=512, tf=512):
    M, D = x.shape
    F = wg.shape[-1]
    tm = _tile(M, tm, 8)
    tf = _tile(F, tf, LANES)
    base, rpg = grp
    return pl.pallas_call(
        functools.partial(_ffn_kernel, slot=slot),
        grid=(M // tm, F // tf),
        in_specs=[pl.BlockSpec((tm, D), lambda i, f: (i, 0)),
                  pl.BlockSpec((None, N_MOD, D), _mod_spec(base, rpg, tm)),
                  pl.BlockSpec((1, D), lambda i, f: (0, 0)),
                  pl.BlockSpec((1, D), lambda i, f: (0, 0)),
                  pl.BlockSpec((None, None, D, tf), lambda i, f: (layer, which, 0, f)),
                  pl.BlockSpec((None, None, D, tf), lambda i, f: (layer, which, 0, f)),
                  pl.BlockSpec((None, None, tf, D), lambda i, f: (layer, which, f, 0))],
        out_specs=pl.BlockSpec((tm, D), lambda i, f: (i, 0)),
        out_shape=jax.ShapeDtypeStruct((M, D), F32),
        scratch_shapes=[pltpu.VMEM((tm, D), BF16), pltpu.VMEM((tm, D), F32)],
        compiler_params=_cparams(("arbitrary", "arbitrary")),
        name="ffn",
    )(x, mod, g_pre.reshape(1, D), g_post.reshape(1, D), wg, wu, wd)


_ACTS = {None: lambda t: t, "tanh": jnp.tanh, "sigmoid": jax.nn.sigmoid}


def _mm_kernel(x_ref, w_ref, o_ref, *, act):
    o_ref[...] = _ACTS[act](_dot(x_ref[...].astype(BF16), w_ref[...]))


def _mm(x, w, act=None, *, bm=512, bn=512):
    M, K = x.shape
    N = w.shape[1]
    bm = _tile(M, bm, 8)
    bn = _tile(N, bn, LANES)
    return pl.pallas_call(
        functools.partial(_mm_kernel, act=act),
        grid=(M // bm, N // bn),
        in_specs=[pl.BlockSpec((bm, K), lambda i, n: (i, 0)),
                  pl.BlockSpec((K, bn), lambda i, n: (0, n))],
        out_specs=pl.BlockSpec((bm, bn), lambda i, n: (i, n)),
        out_shape=jax.ShapeDtypeStruct((M, N), F32),
        compiler_params=_cparams(("arbitrary", "arbitrary")),
        name="mm",
    )(x, w)


def _normmod_kernel(x_ref, mod_ref, gpre_ref, o_ref, *, slot):
    shift = mod_ref[3 * slot:3 * slot + 1, :]
    scale = mod_ref[3 * slot + 1:3 * slot + 2, :]
    o_ref[...] = _rms(x_ref[...], gpre_ref[...]) * (1 + scale) + shift


def _normmod(x, mod, grp, slot, g_pre, *, tm=512):
    M, D = x.shape
    tm = _tile(M, tm, 8)
    base, rpg = grp
    return pl.pallas_call(
        functools.partial(_normmod_kernel, slot=slot),
        grid=(M // tm,),
        in_specs=[pl.BlockSpec((tm, D), lambda i: (i, 0)),
                  pl.BlockSpec((None, N_MOD, D), _mod_spec(base, rpg, tm)),
                  pl.BlockSpec((1, D), lambda i: (0, 0))],
        out_specs=pl.BlockSpec((tm, D), lambda i: (i, 0)),
        out_shape=jax.ShapeDtypeStruct((M, D), F32),
        compiler_params=_cparams(("arbitrary",)),
        name="normmod",
    )(x, mod, g_pre.reshape(1, D))


def _outproj_kernel(a_ref, w_ref, x_ref, mod_ref, gpost_ref, o_ref, *, slot):
    y = _dot(a_ref[...].astype(BF16), w_ref[...])
    gate = mod_ref[3 * slot + 2:3 * slot + 3, :]
    o_ref[...] = x_ref[...] + gate * _rms(y, gpost_ref[...])


def _outproj(a, w, x, mod, grp, slot, g_post, *, tm=256):
    M, D = x.shape
    K = a.shape[1]
    tm = _tile(M, tm, 8)
    base, rpg = grp
    return pl.pallas_call(
        functools.partial(_outproj_kernel, slot=slot),
        grid=(M // tm,),
        in_specs=[pl.BlockSpec((tm, K), lambda i: (i, 0)),
                  pl.BlockSpec((K, D), lambda i: (0, 0)),
                  pl.BlockSpec((tm, D), lambda i: (i, 0)),
                  pl.BlockSpec((None, N_MOD, D), _mod_spec(base, rpg, tm)),
                  pl.BlockSpec((1, D), lambda i: (0, 0))],
        out_specs=pl.BlockSpec((tm, D), lambda i: (i, 0)),
        out_shape=jax.ShapeDtypeStruct((M, D), F32),
        compiler_params=_cparams(("arbitrary",)),
        name="outproj",
    )(a, w, x, mod, g_post.reshape(1, D))


def _softmax_pv(s_parts, v_parts):
    m = functools.reduce(jnp.maximum, [jnp.max(s, axis=-1, keepdims=True) for s in s_parts])
    es = [jnp.exp(s - m) for s in s_parts]
    denom = functools.reduce(jnp.add, [jnp.sum(e, axis=-1, keepdims=True) for e in es])
    return functools.reduce(jnp.add, [_dot((e / denom).astype(BF16), v) for e, v in zip(es, v_parts)])


def _nat_kernel(q_ref, k_ref, v_ref, kc_ref, vc_ref, bias_ref, o_ref, *, rows):
    scale = HEAD ** -0.5
    n_lat = WIN_H * GRID_W
    lane = lax.broadcasted_iota(jnp.int32, (1, LANES), 1)
    head_masks = (lane < HEAD, lane >= HEAD)
    qcol = lax.broadcasted_iota(jnp.int32, (GRID_W, n_lat), 0)
    kcol = lax.broadcasted_iota(jnp.int32, (GRID_W, n_lat), 1) % GRID_W
    cstart = jnp.clip(qcol - WIN_W // 2, 0, GRID_W - WIN_W)
    col_ok = (kcol >= cstart) & (kcol < cstart + WIN_W)
    kc = kc_ref[...].astype(BF16)
    vc = vc_ref[...].astype(BF16)

    def row_body(r, carry):
        r0 = jnp.clip(r - WIN_H // 2, 0, rows - WIN_H)
        q = q_ref[pl.ds(pl.multiple_of(r * GRID_W, GRID_W), GRID_W), :]
        kw = k_ref[pl.ds(pl.multiple_of(r0 * GRID_W, GRID_W), n_lat), :].astype(BF16)
        vw = v_ref[pl.ds(pl.multiple_of(r0 * GRID_W, GRID_W), n_lat), :].astype(BF16)
        outs = []
        for h in range(2):
            qh = jnp.where(head_masks[h], q, 0.0).astype(BF16)
            s = _dot_nt(qh, kw) * scale + bias_ref[h, r - r0]
            s = jnp.where(col_ok, s, MASK_VALUE)
            s_c = _dot_nt(qh, kc) * scale
            outs.append(_softmax_pv([s, s_c], [vw, vc]))
        o_ref[pl.ds(pl.multiple_of(r * GRID_W, GRID_W), GRID_W), :] = jnp.where(head_masks[0], outs[0], outs[1])
        return carry

    lax.fori_loop(0, rows, row_body, 0)


def _nat_bias_table(rpb):
    H = rpb.shape[0]
    off = jnp.arange(WIN_H)
    dr = off[None, :] - off[:, None] + WIN_H - 1
    cols = jnp.arange(GRID_W)
    dc = jnp.clip(cols[None, :] - cols[:, None] + WIN_W - 1, 0, 2 * WIN_W - 2)
    t = rpb[:, dr[:, None, :, None], dc[None, :, None, :]]
    return t.reshape(H, WIN_H, GRID_W, WIN_H * GRID_W).astype(F32)


def _nat_attention(qkv, qkv_c, bias):
    B, T, D3 = qkv.shape
    D = D3 // 3
    C = qkv_c.shape[1]
    nd = D // LANES
    rows = T // GRID_W
    assert rows >= WIN_H
    return pl.pallas_call(
        functools.partial(_nat_kernel, rows=rows),
        grid=(B, nd),
        in_specs=[pl.BlockSpec((None, T, LANES), lambda b, p: (b, 0, p)),
                  pl.BlockSpec((None, T, LANES), lambda b, p: (b, 0, nd + p)),
                  pl.BlockSpec((None, T, LANES), lambda b, p: (b, 0, 2 * nd + p)),
                  pl.BlockSpec((None, C, LANES), lambda b, p: (b, 0, nd + p)),
                  pl.BlockSpec((None, C, LANES), lambda b, p: (b, 0, 2 * nd + p)),
                  pl.BlockSpec((2, WIN_H, GRID_W, WIN_H * GRID_W), lambda b, p: (p, 0, 0, 0))],
        out_specs=pl.BlockSpec((None, T, LANES), lambda b, p: (b, 0, p)),
        out_shape=jax.ShapeDtypeStruct((B, T, D), F32),
        compiler_params=_cparams(("arbitrary", "arbitrary")),
        name="nat_attention",
    )(qkv, qkv, qkv, qkv_c, qkv_c, bias)


def _ctx_attn_kernel(q_ref, k_ref, v_ref, o_ref):
    scale = HEAD ** -0.5
    lane = lax.broadcasted_iota(jnp.int32, (1, LANES), 1)
    head_masks = (lane < HEAD, lane >= HEAD)
    q = q_ref[...]
    k = k_ref[...].astype(BF16)
    v = v_ref[...].astype(BF16)
    outs = []
    for h in range(2):
        qh = jnp.where(head_masks[h], q, 0.0).astype(BF16)
        outs.append(_softmax_pv([_dot_nt(qh, k) * scale], [v]))
    o_ref[...] = jnp.where(head_masks[0], outs[0], outs[1])


def _ctx_attention(qkv_c):
    B, C, D3 = qkv_c.shape
    D = D3 // 3
    nd = D // LANES
    return pl.pallas_call(
        _ctx_attn_kernel,
        grid=(B, nd),
        in_specs=[pl.BlockSpec((None, C, LANES), lambda b, p: (b, 0, p)),
                  pl.BlockSpec((None, C, LANES), lambda b, p: (b, 0, nd + p)),
                  pl.BlockSpec((None, C, LANES), lambda b, p: (b, 0, 2 * nd + p))],
        out_specs=pl.BlockSpec((None, C, LANES), lambda b, p: (b, 0, p)),
        out_shape=jax.ShapeDtypeStruct((B, C, D), F32),
        compiler_params=_cparams(("arbitrary", "arbitrary")),
        name="ctx_attention",
    )(qkv_c, qkv_c, qkv_c)


def _pad_to(w, axis, mult=LANES):
    n = w.shape[axis]
    pad = (-n) % mult
    if pad == 0:
        return w
    widths = [(0, 0)] * w.ndim
    widths[axis] = (0, pad)
    return jnp.pad(w, widths)


def _lora_in(w):
    w = _pad_to(w, 2)
    return jnp.concatenate(list(w), axis=1).astype(BF16)


def _lora_out(w):
    return _pad_to(w, 1).astype(BF16)


def _centred_shift(h):
    prev = jnp.pad(h[:, :-1], ((0, 0), (1, 0), (0, 0)))
    nxt = jnp.pad(h[:, 1:], ((0, 0), (0, 1), (0, 0)))
    return 0.5 * (prev + nxt) - h


def _rwkv_prep(h, v_first, p):
    B, L, D = h.shape
    M = B * L
    xx = _centred_shift(h)
    xr, xw, xk, xv, xa, xg = ((h + xx * p["mix"][m]).reshape(M, D) for m in range(6))
    r = _mm(xr, p["w_r"])
    k = _mm(xk, p["w_k"])
    v = _mm(xv, p["w_v"])
    if p["v_res"] is not None:
        v0, v1, v2 = p["v_res"]
        v = v + (v_first - v) * jax.nn.sigmoid(v0 + _mm(_mm(xv, v1), v2[0]))
    lp = p["w2"].shape[1]
    t_w = _mm(xw, p["w1"], act="tanh")
    lora_w = jnp.stack([_mm(t_w[:, z * lp:(z + 1) * lp], p["w2"][z]) for z in range(2)])
    w_log = -jax.nn.softplus(-(p["w0"][:, None, :] + lora_w)) - 0.5
    lw = -jnp.exp(w_log)
    la = p["a2"].shape[1]
    t_a = _mm(xa, p["a1"])
    a = jax.nn.sigmoid(p["a0"][:, None, :] + jnp.stack([_mm(t_a[:, z * la:(z + 1) * la], p["a2"][z]) for z in range(2)]))
    kk = (k * p["k_k"]).reshape(M, D // HEAD, HEAD)
    kk = (kk * lax.rsqrt(jnp.maximum(jnp.sum(kk * kk, axis=-1, keepdims=True), L2_EPS_SQ))).reshape(M, D)
    kd = k[None] * (1 + (a - 1) * p["k_a"])
    g = _mm(_mm(xg, p["g1"], act="sigmoid"), p["g2"][0])
    return dict(r=r, v=v, kk=kk, lw=lw, kd=kd, a=a, g=g, shape=(B, L, D))


def _rwkv_scan_inputs(q):
    B, L, D = q["shape"]
    three = lambda t: t.reshape(B, L, D)
    four = lambda t: t.reshape(2, B, L, D)
    return three(q["r"]), three(q["v"]), three(q["kk"]), four(q["lw"]), four(q["kd"]), four(q["a"])


def _rwkv_post(y, q, p):
    B, L, D = q["shape"]
    M, H = B * L, D // HEAD
    wkv = (y[0] + y[1]).reshape(M, H, HEAD)
    mu = jnp.mean(wkv, axis=-1, keepdims=True)
    var = jnp.mean(jnp.square(wkv - mu), axis=-1, keepdims=True)
    o = ((wkv - mu) * lax.rsqrt(var + RWKV_GN_EPS)).reshape(M, D) * p["gn_w"] + p["gn_b"]
    rh = q["r"].reshape(M, H, HEAD)
    coef = jnp.sum(rh[None] * q["kd"].reshape(2, M, H, HEAD) * p["r_k"], axis=(0, 3))
    bonus = (coef[..., None] * q["v"].reshape(M, H, HEAD)).reshape(M, D)
    return (o + bonus) * q["g"]


def kernel(x, c, ctx, c_ctx, ada_w, ada_b, norm_pre, norm_post, ffn_w_gate, ffn_w_up, ffn_w_down, rwkv_mix, rwkv_w_r, rwkv_w_k, rwkv_w_v, rwkv_w_o, rwkv_w0, rwkv_w1, rwkv_w2, rwkv_a0, rwkv_a1, rwkv_a2, rwkv_v0, rwkv_v1, rwkv_v2, rwkv_k_k, rwkv_k_a, rwkv_r_k, rwkv_g1, rwkv_g2, rwkv_gn_w, rwkv_gn_b, nat_w_qkv, nat_w_o, nat_rpb):
    B, T, D = x.shape
    C = ctx.shape[1]
    depth = ada_w.shape[0]
    assert B + 1 <= 8 and D % (2 * LANES) == 0
    cvec = jnp.zeros((8, D), F32).at[:B].set(c).at[B].set(c_ctx)
    mods = _adaln(cvec, ada_w, ada_b).reshape(depth, 8, N_MOD, D)
    wg, wu, wd = (w.astype(BF16) for w in (ffn_w_gate, ffn_w_up, ffn_w_down))
    grp_l, grp_c = (0, T), (B, B * C)
    xl = x.reshape(B * T, D)
    xc = ctx.reshape(B * C, D)
    vf_l = vf_c = None
    for i in range(depth):
        last = i == depth - 1
        j = i // 2
        mod = mods[i]
        ffn = functools.partial(_ffn, mod=mod, wg=wg, wu=wu, wd=wd, layer=i)
        xl = ffn(xl, grp=grp_l, slot=0, g_pre=norm_pre[i, 0], g_post=norm_post[i, 0], which=0)
        xc = ffn(xc, grp=grp_c, slot=0, g_pre=norm_pre[i, 0], g_post=norm_post[i, 0], which=0)
        hl = _normmod(xl, mod, grp_l, 1, norm_pre[i, 1])
        hc = _normmod(xc, mod, grp_c, 1, norm_pre[i, 1])
        if i % 2 == 0:
            p = dict(
                mix=rwkv_mix[j], w_r=rwkv_w_r[j].astype(BF16), w_k=rwkv_w_k[j].astype(BF16),
                w_v=rwkv_w_v[j].astype(BF16), w0=rwkv_w0[j], w1=_lora_in(rwkv_w1[j]), w2=_lora_out(rwkv_w2[j]),
                a0=rwkv_a0[j], a1=_lora_in(rwkv_a1[j]), a2=_lora_out(rwkv_a2[j]),
                k_k=rwkv_k_k[j], k_a=rwkv_k_a[j], r_k=rwkv_r_k[j],
                g1=_lora_in(rwkv_g1[j][None]), g2=_lora_out(rwkv_g2[j][None]),
                gn_w=rwkv_gn_w[j], gn_b=rwkv_gn_b[j],
                v_res=None if j == 0 else (rwkv_v0[j - 1], _lora_in(rwkv_v1[j - 1][None]), _lora_out(rwkv_v2[j - 1][None])))
            q_c = _rwkv_prep(hc.reshape(B, C, D), vf_c, p)
            q_l = _rwkv_prep(hl.reshape(B, T, D), vf_l, p)
            if j == 0:
                vf_l, vf_c = q_l["v"], q_c["v"]
            s0 = jnp.zeros((2, B, D // LANES, LANES, LANES), F32)
            y_c, s_c = _wkv_scan(*_rwkv_scan_inputs(q_c), s0)
            y_l, _ = _wkv_scan(*_rwkv_scan_inputs(q_l), s_c)
            w_o = rwkv_w_o[j].astype(BF16)
            a_l = _rwkv_post(y_l, q_l, p)
            a_c = None if last else _rwkv_post(y_c, q_c, p)
        else:
            w_qkv = nat_w_qkv[j].astype(BF16)
            w_o = nat_w_o[j].astype(BF16)
            qkv_l = _mm(hl, w_qkv).reshape(B, T, 3 * D)
            qkv_c = _mm(hc, w_qkv).reshape(B, C, 3 * D)
            a_l = _nat_attention(qkv_l, qkv_c, _nat_bias_table(nat_rpb[j])).reshape(B * T, D)
            a_c = None if last else _ctx_attention(qkv_c).reshape(B * C, D)
        xl = _outproj(a_l, w_o, xl, mod, grp_l, 1, norm_post[i, 1])
        xl = ffn(xl, grp=grp_l, slot=2, g_pre=norm_pre[i, 2], g_post=norm_post[i, 2], which=1)
        if not last:
            xc = _outproj(a_c, w_o, xc, mod, grp_c, 1, norm_post[i, 1])
            xc = ffn(xc, grp=grp_c, slot=2, g_pre=norm_pre[i, 2], g_post=norm_post[i, 2], which=1)
    return xl.reshape(B, T, D)
```

```python
import functools

import jax
import jax.numpy as jnp
from jax import lax
from jax.experimental import pallas as pl
from jax.experimental.pallas import tpu as pltpu

F32 = jnp.float32
BF16 = jnp.bfloat16

LANES = 128
SUBLANES = 8
HEAD = 64
N_MOD = 9
MACARON_WEIGHT = 0.5
RMS_EPS = 1e-6
RWKV_GN_EPS = 64e-5
L2_EPS_SQ = 1e-24
GRID_W = 64
WIN_H = 8
WIN_W = 16
MASK_VALUE = -1e30
VMEM_LIMIT = 56 * 1024 * 1024


def _cparams(sem):
    return pltpu.CompilerParams(dimension_semantics=sem, vmem_limit_bytes=VMEM_LIMIT)


def _dot(a, b):
    return jnp.dot(a, b, preferred_element_type=F32)


def _dot_nt(a, b):
    return lax.dot_general(a, b, (((1,), (1,)), ((), ())), preferred_element_type=F32)


def _dot_tn(a, b):
    return lax.dot_general(a, b, (((0,), (0,)), ((), ())), preferred_element_type=F32)


def _split3(x):
    hi = x.astype(BF16)
    r1 = x - hi.astype(F32)
    mid = r1.astype(BF16)
    lo = (r1 - mid.astype(F32)).astype(BF16)
    return hi, mid, lo


def _dot_exact_lhs01(m01, x):
    hi, mid, lo = _split3(x)
    return _dot(m01, hi) + _dot(m01, mid) + _dot(m01, lo)


def _dot_exact_rhs01(x, m01):
    hi, mid, lo = _split3(x)
    return _dot(hi, m01) + _dot(mid, m01) + _dot(lo, m01)


def _head_ones(n):
    r = lax.broadcasted_iota(jnp.int32, (n, n), 0) // HEAD
    c = lax.broadcasted_iota(jnp.int32, (n, n), 1) // HEAD
    return (r == c).astype(BF16)


def _softplus(x):
    return jnp.maximum(x, 0.0) + jnp.log1p(jnp.exp(-jnp.abs(x)))


def _rms(x, g):
    return x * lax.rsqrt(jnp.mean(x * x, axis=-1, keepdims=True) + RMS_EPS) * g


def _modulated(x, mod_ref, gpre_ref, slot):
    shift = mod_ref[3 * slot:3 * slot + 1, :]
    scale = mod_ref[3 * slot + 1:3 * slot + 2, :]
    return _rms(x, gpre_ref[...]) * (1 + scale) + shift


def _tile(n, want, unit):
    if n <= want:
        return n
    t = (want // unit) * unit
    while n % t:
        t -= unit
    return t


def _mod_spec(base, rows_per_group, tm):
    assert rows_per_group % tm == 0
    return lambda i, *_: (base + (i * tm) // rows_per_group, 0, 0)


def _wkv_kernel(r_ref, v_ref, kk_ref, lw_ref, kd_ref, a_ref, s0_ref, y_ref, sfin_ref, s_scr,
                *, chunk, pairs):
    z = pl.program_id(0)
    c = pl.program_id(3)
    C = chunk

    @pl.when(c == 0)
    def _():
        s_scr[...] = s0_ref[...]

    row = lax.broadcasted_iota(jnp.int32, (C, C), 0)
    col = lax.broadcasted_iota(jnp.int32, (C, C), 1)
    d = (row - col) * (1 - 2 * z)
    strict = d > 0
    incl = d >= 0
    lane = lax.broadcasted_iota(jnp.int32, (1, LANES), 1)
    head_masks = (lane < HEAD, lane >= HEAD)
    srow = lax.broadcasted_iota(jnp.int32, (LANES, LANES), 0)
    scol = lax.broadcasted_iota(jnp.int32, (LANES, LANES), 1)
    blockdiag = (srow < HEAD) == (scol < HEAD)
    eye = (row == col).astype(BF16)
    levels = []
    s = 1
    while s < C:
        levels.append((row // (2 * s) == col // (2 * s)) & (row // s != col // s))
        s *= 2

    lw = lw_ref[...]
    kk = kk_ref[...]
    kd = kd_ref[...]
    L = _dot_exact_lhs01(incl.astype(BF16), lw)
    l_end = jnp.sum(lw, axis=0, keepdims=True)
    lm = 0.5 * l_end
    e_m = jnp.exp(-lm)
    e_p = jnp.exp(lm)
    e_sh = jnp.exp(lm - L)
    at_t = -kk * jnp.exp(L - lw)
    rt_t = r_ref[...] * jnp.exp(L)
    sr_all = jnp.concatenate([at_t, rt_t], axis=0).astype(BF16)
    at_sh = at_t * e_m
    rt_sh = rt_t * e_m
    bt = (kk * a_ref[...]) * e_sh
    kt = kd * e_sh
    bk_all = jnp.concatenate([bt, kt], axis=0).astype(BF16)
    bkh_all = jnp.concatenate([bt * e_p, kt * e_p], axis=0).astype(BF16)
    v_all = v_ref[...].astype(BF16)
    s_decay = jnp.exp(l_end)

    heads = [(p, h) for p in range(pairs) for h in range(2)]
    lanes = [slice(p * LANES, (p + 1) * LANES) for p in range(pairs)]
    sr = [_dot_nt(sr_all[:, lanes[p]], s_scr[p].astype(BF16)) for p in range(pairs)]
    quads = []
    for p, h in heads:
        lhs = jnp.concatenate([jnp.where(head_masks[h], at_sh[:, lanes[p]], 0.0),
                               jnp.where(head_masks[h], rt_sh[:, lanes[p]], 0.0)], axis=0).astype(BF16)
        quads.append(_dot_nt(lhs, bk_all[:, lanes[p]]))
    a_ab = [jnp.where(strict, q[:C, :C], 0.0).astype(BF16) for q in quads]
    a_ak = [jnp.where(strict, q[:C, C:], 0.0).astype(BF16) for q in quads]
    r_b = [jnp.where(incl, q[C:, :C], 0.0).astype(BF16) for q in quads]
    r_k = [jnp.where(incl, q[C:, C:], 0.0).astype(BF16) for q in quads]
    t = [eye + jnp.where(levels[0], a, jnp.zeros_like(a)) for a in a_ab]
    for lv in levels[1:]:
        x = [jnp.where(lv, a, jnp.zeros_like(a)) for a in a_ab]
        m1 = [_dot(xi, ti).astype(BF16) for xi, ti in zip(x, t)]
        m2 = [_dot(ti, mi).astype(BF16) for ti, mi in zip(t, m1)]
        t = [ti + mi for ti, mi in zip(t, m2)]
    rhs = [(sr[p][:C] + _dot(a_ak[i], v_all[:, lanes[p]])).astype(BF16) for i, (p, h) in enumerate(heads)]
    u = [_dot(ti, ri) for ti, ri in zip(t, rhs)]
    y = [sr[p][C:] + _dot(r_b[i], u[i].astype(BF16)) + _dot(r_k[i], v_all[:, lanes[p]])
         for i, (p, h) in enumerate(heads)]
    for p in range(pairs):
        y_ref[:, lanes[p]] = jnp.where(head_masks[0], y[2 * p], y[2 * p + 1])
        u_p = jnp.where(head_masks[0], u[2 * p], u[2 * p + 1]).astype(BF16)
        upd = _dot_tn(jnp.concatenate([u_p, v_all[:, lanes[p]]], axis=0), bkh_all[:, lanes[p]])
        s_scr[p] = s_scr[p] * s_decay[:, lanes[p]] + jnp.where(blockdiag, upd, 0.0)

    @pl.when(c == pl.num_programs(3) - 1)
    def _():
        sfin_ref[...] = s_scr[...]


def _wkv_scan(r, v, kk, lw, kd, a, s0, *, chunk=128, pairs=8):
    B, T, D = r.shape
    pairs = min(pairs, D // LANES)
    lw_lanes = pairs * LANES
    assert T % chunk == 0 and D % lw_lanes == 0
    nc = T // chunk
    ng = D // lw_lanes

    def cidx(z, c):
        return c + z * (nc - 1 - 2 * c)

    tok_spec = pl.BlockSpec((None, chunk, lw_lanes), lambda z, b, g, c: (b, cidx(z, c), g))
    dir_spec = pl.BlockSpec((None, None, chunk, lw_lanes), lambda z, b, g, c: (z, b, cidx(z, c), g))
    st_spec = pl.BlockSpec((None, None, pairs, LANES, LANES), lambda z, b, g, c: (z, b, g, 0, 0))
    y, s_fin = pl.pallas_call(
        functools.partial(_wkv_kernel, chunk=chunk, pairs=pairs),
        grid=(2, B, ng, nc),
        in_specs=[tok_spec, tok_spec, tok_spec, dir_spec, dir_spec, dir_spec, st_spec],
        out_specs=[dir_spec, st_spec],
        out_shape=[jax.ShapeDtypeStruct((2, B, T, D), F32),
                   jax.ShapeDtypeStruct(s0.shape, F32)],
        scratch_shapes=[pltpu.VMEM((pairs, LANES, LANES), F32)],
        compiler_params=_cparams(("arbitrary", "arbitrary", "arbitrary", "arbitrary")),
        name="wkv_scan",
    )(r, v, kk, lw, kd, a, s0)
    return y, s_fin


def _rwkv_prep_kernel(*refs, seq_len, tm, has_vres):
    (x_ref, xp_ref, xn_ref, mod_ref, gpre_ref, mix_ref, wr_ref, wk_ref, wv_ref,
     w1_ref, w2_ref, w0_ref, a1_ref, a2_ref, a0_ref, g1_ref, g2_ref, kk_ref, ka_ref) = refs[:19]
    pos = 19
    if has_vres:
        v1_ref, v2_ref, v0_ref, vf_ref = refs[pos:pos + 4]
        pos += 4
    r_ref, v_ref, kkn_ref, g_ref, lw_ref, kd_ref, a_ref = refs[pos:pos + 7]
    pos += 7
    xmix_scr, tw_scr, ta_scr, tg_scr = refs[pos:pos + 4]
    tv_scr = refs[pos + 4] if has_vres else None
    i = pl.program_id(0)
    n = pl.program_id(1)

    @pl.when(n == 0)
    def _():
        h = _modulated(x_ref[...], mod_ref, gpre_ref, 1)
        t0 = i * tm
        hp = _modulated(xp_ref[...], mod_ref, gpre_ref, 1)[SUBLANES - 1:SUBLANES, :]
        hn = _modulated(xn_ref[...], mod_ref, gpre_ref, 1)[0:1, :]
        hp = jnp.where(t0 % seq_len == 0, 0.0, hp)
        hn = jnp.where((t0 + tm) % seq_len == 0, 0.0, hn)
        rowid = lax.broadcasted_iota(jnp.int32, (tm, 1), 0)
        prev = jnp.where(rowid == 0, hp, pltpu.roll(h, 1, 0))
        nxt = jnp.where(rowid == tm - 1, hn, pltpu.roll(h, tm - 1, 0))
        xx = 0.5 * (prev + nxt) - h
        mixed = lambda m: (h + xx * mix_ref[m:m + 1, :]).astype(BF16)
        xmix_scr[0] = mixed(0)
        xmix_scr[1] = mixed(2)
        xv = mixed(3)
        xmix_scr[2] = xv
        tw_scr[...] = jnp.tanh(_dot(mixed(1), w1_ref[...])).astype(BF16)
        ta_scr[...] = _dot(mixed(4), a1_ref[...]).astype(BF16)
        tg_scr[...] = jax.nn.sigmoid(_dot(mixed(5), g1_ref[...])).astype(BF16)
        if has_vres:
            tv_scr[...] = _dot(xv, v1_ref[...]).astype(BF16)

    r_ref[...] = _dot(xmix_scr[0], wr_ref[...])
    k = _dot(xmix_scr[1], wk_ref[...])
    v = _dot(xmix_scr[2], wv_ref[...])
    if has_vres:
        v = v + (vf_ref[...] - v) * jax.nn.sigmoid(v0_ref[...] + _dot(tv_scr[...], v2_ref[...]))
    v_ref[...] = v
    g_ref[...] = _dot(tg_scr[...], g2_ref[...])
    kk = k * kk_ref[...]
    ssq = _dot_exact_rhs01(kk * kk, _head_ones(kk.shape[1]))
    kkn_ref[...] = kk * lax.rsqrt(jnp.maximum(ssq, L2_EPS_SQ))
    lp = w2_ref.shape[1]
    la = a2_ref.shape[1]
    for zz in range(2):
        lora_w = _dot(tw_scr[:, zz * lp:(zz + 1) * lp], w2_ref[zz])
        w_log = -_softplus(-(w0_ref[zz:zz + 1, :] + lora_w)) - 0.5
        lw_ref[zz] = -jnp.exp(w_log)
        a = jax.nn.sigmoid(a0_ref[zz:zz + 1, :] + _dot(ta_scr[:, zz * la:(zz + 1) * la], a2_ref[zz]))
        a_ref[zz] = a
        kd_ref[zz] = k * (1 + (a - 1) * ka_ref[...])


def _rwkv_prep(x, mod, grp, g_pre, p, v_first, seq_len, *, tm=512, tn=256):
    M, D = x.shape
    tm = _tile(seq_len, tm, SUBLANES)
    tn = _tile(D, tn, LANES)
    base, rpg = grp
    has_vres = p["v_res"] is not None
    nb = M // SUBLANES
    row = lambda i, n: (i, 0)
    col = lambda i, n: (0, n)
    full = lambda i, n: (0, 0)
    col3 = lambda i, n: (0, 0, n)
    tile = lambda i, n: (i, n)
    tile3 = lambda i, n: (0, i, n)
    lw1, la1, lg1 = p["w1"].shape[1], p["a1"].shape[1], p["g1"].shape[1]
    in_specs = [
        pl.BlockSpec((tm, D), row),
        pl.BlockSpec((SUBLANES, D), lambda i, n: (jnp.maximum(i * (tm // SUBLANES) - 1, 0), 0)),
        pl.BlockSpec((SUBLANES, D), lambda i, n: (jnp.minimum((i + 1) * (tm // SUBLANES), nb - 1), 0)),
        pl.BlockSpec((None, N_MOD, D), _mod_spec(base, rpg, tm)),
        pl.BlockSpec((1, D), full),
        pl.BlockSpec((6, D), full),
        pl.BlockSpec((D, tn), col), pl.BlockSpec((D, tn), col), pl.BlockSpec((D, tn), col),
        pl.BlockSpec((D, lw1), full), pl.BlockSpec((2, lw1 // 2, tn), col3), pl.BlockSpec((2, tn), col),
        pl.BlockSpec((D, la1), full), pl.BlockSpec((2, la1 // 2, tn), col3), pl.BlockSpec((2, tn), col),
        pl.BlockSpec((D, lg1), full), pl.BlockSpec((lg1, tn), col),
        pl.BlockSpec((1, tn), col), pl.BlockSpec((1, tn), col),
    ]
    args = [x, x, x, mod, g_pre.reshape(1, D), p["mix"], p["w_r"], p["w_k"], p["w_v"],
            p["w1"], p["w2"], p["w0"], p["a1"], p["a2"], p["a0"], p["g1"], p["g2"][0],
            p["k_k"].reshape(1, D), p["k_a"].reshape(1, D)]
    scratch = [pltpu.VMEM((3, tm, D), BF16), pltpu.VMEM((tm, lw1), BF16),
               pltpu.VMEM((tm, la1), BF16), pltpu.VMEM((tm, lg1), BF16)]
    if has_vres:
        v0, v1, v2 = p["v_res"]
        lv1 = v1.shape[1]
        in_specs += [pl.BlockSpec((D, lv1), full), pl.BlockSpec((lv1, tn), col),
                     pl.BlockSpec((1, tn), col), pl.BlockSpec((tm, tn), tile)]
        args += [v1, v2[0], v0.reshape(1, D), v_first]
        scratch.append(pltpu.VMEM((tm, lv1), BF16))
    one = jax.ShapeDtypeStruct((M, D), F32)
    two = jax.ShapeDtypeStruct((2, M, D), F32)
    r, v, kk, g, lw, kd, a = pl.pallas_call(
        functools.partial(_rwkv_prep_kernel, seq_len=seq_len, tm=tm, has_vres=has_vres),
        grid=(M // tm, D // tn),
        in_specs=in_specs,
        out_specs=[pl.BlockSpec((tm, tn), tile)] * 4 + [pl.BlockSpec((2, tm, tn), tile3)] * 3,
        out_shape=[one] * 4 + [two] * 3,
        scratch_shapes=scratch,
        compiler_params=_cparams(("arbitrary", "arbitrary")),
        name="rwkv_prep",
    )(*args)
    return dict(r=r, v=v, kk=kk, g=g, lw=lw, kd=kd, a=a)


def _rwkv_out_kernel(y_ref, r_ref, kd_ref, v_ref, g_ref, rk_ref, gnw_ref, gnb_ref, w_ref, x_ref, mod_ref,
                     gpost_ref, o_ref, pre_scr, *, slab):
    D = x_ref.shape[1]
    ones = _head_ones(slab)
    for j in range(D // slab):
        sl = slice(j * slab, (j + 1) * slab)
        wkv = y_ref[0, :, sl] + y_ref[1, :, sl]
        mu = _dot_exact_rhs01(wkv, ones) * (1.0 / HEAD)
        cen = wkv - mu
        var = _dot_exact_rhs01(cen * cen, ones) * (1.0 / HEAD)
        o = cen * lax.rsqrt(var + RWKV_GN_EPS) * gnw_ref[:, sl] + gnb_ref[:, sl]
        coef = _dot_exact_rhs01(r_ref[:, sl] * (kd_ref[0, :, sl] + kd_ref[1, :, sl]) * rk_ref[:, sl], ones)
        pre_scr[:, sl] = ((o + coef * v_ref[:, sl]) * g_ref[:, sl]).astype(BF16)
    y = _dot(pre_scr[...], w_ref[...])
    o_ref[...] = x_ref[...] + mod_ref[5:6, :] * _rms(y, gpost_ref[...])


def _rwkv_out(y, q, p, w_o, x, mod, grp, g_post, *, tm=128, slab=512):
    M, D = x.shape
    tm = _tile(M, tm, SUBLANES)
    slab = _tile(D, slab, LANES)
    base, rpg = grp
    row = lambda i: (i, 0)
    row3 = lambda i: (0, i, 0)
    full = lambda i: (0, 0)
    return pl.pallas_call(
        functools.partial(_rwkv_out_kernel, slab=slab),
        grid=(M // tm,),
        in_specs=[pl.BlockSpec((2, tm, D), row3), pl.BlockSpec((tm, D), row), pl.BlockSpec((2, tm, D), row3),
                  pl.BlockSpec((tm, D), row), pl.BlockSpec((tm, D), row),
                  pl.BlockSpec((1, D), full), pl.BlockSpec((1, D), full), pl.BlockSpec((1, D), full),
                  pl.BlockSpec((D, D), full), pl.BlockSpec((tm, D), row),
                  pl.BlockSpec((None, N_MOD, D), _mod_spec(base, rpg, tm)), pl.BlockSpec((1, D), full)],
        out_specs=pl.BlockSpec((tm, D), row),
        out_shape=jax.ShapeDtypeStruct((M, D), F32),
        scratch_shapes=[pltpu.VMEM((tm, D), BF16)],
        compiler_params=_cparams(("arbitrary",)),
        name="rwkv_out",
    )(y, q["r"], q["kd"], q["v"], q["g"], p["r_k"].reshape(1, D), p["gn_w"].reshape(1, D),
      p["gn_b"].reshape(1, D), w_o, x, mod, g_post.reshape(1, D))


def _adaln_kernel(c_ref, w_ref, b_ref, o_ref):
    cv = c_ref[...]
    cv = cv * jax.nn.sigmoid(cv)
    o_ref[...] = _dot(cv.astype(BF16), w_ref[...].astype(BF16)) + b_ref[...]


def _adaln(cvec, ada_w, ada_b, *, bn=1024):
    depth, D, N = ada_w.shape
    bn = _tile(N, bn, LANES)
    return pl.pallas_call(
        _adaln_kernel,
        grid=(depth, N // bn),
        in_specs=[pl.BlockSpec((8, D), lambda i, n: (0, 0)),
                  pl.BlockSpec((None, D, bn), lambda i, n: (i, 0, n)),
                  pl.BlockSpec((None, 1, bn), lambda i, n: (i, 0, n))],
        out_specs=pl.BlockSpec((None, 8, bn), lambda i, n: (i, 0, n)),
        out_shape=jax.ShapeDtypeStruct((depth, 8, N), F32),
        compiler_params=_cparams(("arbitrary", "arbitrary")),
        name="adaln",
    )(cvec, ada_w, ada_b.reshape(depth, 1, N))


def _ffn_kernel(x_ref, mod_ref, gpre_ref, gpost_ref, wg_ref, wu_ref, wd_ref, o_ref, h_scr, acc_scr, *, slot):
    f = pl.program_id(1)

    @pl.when(f == 0)
    def _():
        h_scr[...] = _modulated(x_ref[...], mod_ref, gpre_ref, slot).astype(BF16)
        acc_scr[...] = jnp.zeros_like(acc_scr)

    h = h_scr[...]
    g = _dot(h, wg_ref[...])
    u = _dot(h, wu_ref[...])
    a = (g * jax.nn.sigmoid(g)) * u
    acc_scr[...] += _dot(a.astype(BF16), wd_ref[...])

    @pl.when(f == pl.num_programs(1) - 1)
    def _():
        gate = mod_ref[3 * slot + 2:3 * slot + 3, :]
        o_ref[...] = x_ref[...] + MACARON_WEIGHT * gate * _rms(acc_scr[...], gpost_ref[...])


def _ffn(x, mod, grp, slot, g_pre, g_post, wg, wu, wd, layer, which, *, tm=512, tf=512):
    M, D = x.shape
    F = wg.shape[-1]
    tm = _tile(M, tm, SUBLANES)
    tf = _tile(F, tf, LANES)
    base, rpg = grp
    return pl.pallas_call(
        functools.partial(_ffn_kernel, slot=slot),
        grid=(M // tm, F // tf),
        in_specs=[pl.BlockSpec((tm, D), lambda i, f: (i, 0)),
                  pl.BlockSpec((None, N_MOD, D), _mod_spec(base, rpg, tm)),
                  pl.BlockSpec((1, D), lambda i, f: (0, 0)),
                  pl.BlockSpec((1, D), lambda i, f: (0, 0)),
                  pl.BlockSpec((None, None, D, tf), lambda i, f: (layer, which, 0, f)),
                  pl.BlockSpec((None, None, D, tf), lambda i, f: (layer, which, 0, f)),
                  pl.BlockSpec((None, None, tf, D), lambda i, f: (layer, which, f, 0))],
        out_specs=pl.BlockSpec((tm, D), lambda i, f: (i, 0)),
        out_shape=jax.ShapeDtypeStruct((M, D), F32),
        scratch_shapes=[pltpu.VMEM((tm, D), BF16), pltpu.VMEM((tm, D), F32)],
        compiler_params=_cparams(("arbitrary", "arbitrary")),
        name="ffn",
    )(x, mod, g_pre.reshape(1, D), g_post.reshape(1, D), wg, wu, wd)


def _normmod_mm_kernel(x_ref, mod_ref, gpre_ref, w_ref, o_ref, h_scr, *, slot):
    @pl.when(pl.program_id(1) == 0)
    def _():
        h_scr[...] = _modulated(x_ref[...], mod_ref, gpre_ref, slot).astype(BF16)

    o_ref[...] = _dot(h_scr[...], w_ref[...]).astype(o_ref.dtype)


def _normmod_mm(x, mod, grp, slot, g_pre, w, out_dtype, *, tm=512, tn=512):
    M, D = x.shape
    N = w.shape[1]
    tm = _tile(M, tm, SUBLANES)
    tn = _tile(N, tn, LANES)
    base, rpg = grp
    return pl.pallas_call(
        functools.partial(_normmod_mm_kernel, slot=slot),
        grid=(M // tm, N // tn),
        in_specs=[pl.BlockSpec((tm, D), lambda i, n: (i, 0)),
                  pl.BlockSpec((None, N_MOD, D), _mod_spec(base, rpg, tm)),
                  pl.BlockSpec((1, D), lambda i, n: (0, 0)),
                  pl.BlockSpec((D, tn), lambda i, n: (0, n))],
        out_specs=pl.BlockSpec((tm, tn), lambda i, n: (i, n)),
        out_shape=jax.ShapeDtypeStruct((M, N), out_dtype),
        scratch_shapes=[pltpu.VMEM((tm, D), BF16)],
        compiler_params=_cparams(("arbitrary", "arbitrary")),
        name="normmod_mm",
    )(x, mod, g_pre.reshape(1, D), w)


def _outproj_kernel(a_ref, w_ref, x_ref, mod_ref, gpost_ref, o_ref, *, slot):
    y = _dot(a_ref[...], w_ref[...])
    gate = mod_ref[3 * slot + 2:3 * slot + 3, :]
    o_ref[...] = x_ref[...] + gate * _rms(y, gpost_ref[...])


def _outproj(a, w, x, mod, grp, slot, g_post, *, tm=256):
    M, D = x.shape
    K = a.shape[1]
    tm = _tile(M, tm, SUBLANES)
    base, rpg = grp
    return pl.pallas_call(
        functools.partial(_outproj_kernel, slot=slot),
        grid=(M // tm,),
        in_specs=[pl.BlockSpec((tm, K), lambda i: (i, 0)),
                  pl.BlockSpec((K, D), lambda i: (0, 0)),
                  pl.BlockSpec((tm, D), lambda i: (i, 0)),
                  pl.BlockSpec((None, N_MOD, D), _mod_spec(base, rpg, tm)),
                  pl.BlockSpec((1, D), lambda i: (0, 0))],
        out_specs=pl.BlockSpec((tm, D), lambda i: (i, 0)),
        out_shape=jax.ShapeDtypeStruct((M, D), F32),
        compiler_params=_cparams(("arbitrary",)),
        name="outproj",
    )(a, w, x, mod, g_post.reshape(1, D))


def _nat_kernel(q_ref, k_ref, v_ref, kc_ref, vc_ref, bias_ref, o_ref, *, rows, group):
    scale = HEAD ** -0.5
    n_lat = WIN_H * GRID_W
    lane = lax.broadcasted_iota(jnp.int32, (1, LANES), 1)
    head_masks = (lane < HEAD, lane >= HEAD)
    qcol = lax.broadcasted_iota(jnp.int32, (GRID_W, n_lat), 0)
    kcol = lax.broadcasted_iota(jnp.int32, (GRID_W, n_lat), 1) % GRID_W
    cstart = jnp.clip(qcol - WIN_W // 2, 0, GRID_W - WIN_W)
    col_ok = (kcol >= cstart) & (kcol < cstart + WIN_W)
    kc = kc_ref[...]
    vc = vc_ref[...]
    zero = jnp.zeros((), BF16)

    def step(i, carry):
        chains = []
        for rr in range(group):
            r = i * group + rr
            r0 = jnp.clip(r - WIN_H // 2, 0, rows - WIN_H)
            q = q_ref[pl.ds(pl.multiple_of(r * GRID_W, GRID_W), GRID_W), :] * scale
            kw = k_ref[pl.ds(pl.multiple_of(r0 * GRID_W, GRID_W), n_lat), :]
            vw = v_ref[pl.ds(pl.multiple_of(r0 * GRID_W, GRID_W), n_lat), :]
            for h in range(2):
                chains.append((r, r0, h, jnp.where(head_masks[h], q, zero), kw, vw))
        s = [_dot_nt(q, kw) for (_, _, _, q, kw, _) in chains]
        sc = [_dot_nt(q, kc) for (_, _, _, q, _, _) in chains]
        s = [jnp.where(col_ok, si + jnp.concatenate(
                [bias_ref[h, 2 * jj - (r - r0) + WIN_H - 1] for jj in range(WIN_H // 2)], axis=1), MASK_VALUE)
             for si, (r, r0, h, _, _, _) in zip(s, chains)]
        m = [jnp.maximum(jnp.max(si, axis=-1, keepdims=True), jnp.max(ci, axis=-1, keepdims=True))
             for si, ci in zip(s, sc)]
        e = [jnp.exp(si - mi) for si, mi in zip(s, m)]
        ec = [jnp.exp(ci - mi) for ci, mi in zip(sc, m)]
        den = [jnp.sum(ei, axis=-1, keepdims=True) + jnp.sum(ci, axis=-1, keepdims=True) for ei, ci in zip(e, ec)]
        o = [(_dot(ei.astype(BF16), vw) + _dot(ci.astype(BF16), vc)) / di
             for ei, ci, di, (_, _, _, _, _, vw) in zip(e, ec, den, chains)]
        for rr in range(group):
            r = i * group + rr
            o_ref[pl.ds(pl.multiple_of(r * GRID_W, GRID_W), GRID_W), :] = jnp.where(
                head_masks[0], o[2 * rr], o[2 * rr + 1]).astype(o_ref.dtype)
        return carry

    lax.fori_loop(0, rows // group, step, 0)


def _nat_bias_table(rpb):
    edge = GRID_W - WIN_W
    padded = jnp.pad(rpb, ((0, 0), (0, 0), (edge, edge)), mode="edge")
    rows = jnp.stack([padded[:, :, GRID_W - 1 - q:2 * GRID_W - 1 - q] for q in range(GRID_W)], axis=2)
    return jnp.concatenate([rows[:, :-1], rows[:, 1:]], axis=-1).astype(F32)


def _nat_attention(qkv, qkv_c, bias, *, group=4):
    B, T, D3 = qkv.shape
    D = D3 // 3
    C = qkv_c.shape[1]
    nd = D // LANES
    rows = T // GRID_W
    assert rows >= WIN_H and rows % group == 0
    return pl.pallas_call(
        functools.partial(_nat_kernel, rows=rows, group=group),
        grid=(B, nd),
        in_specs=[pl.BlockSpec((None, T, LANES), lambda b, p: (b, 0, p)),
                  pl.BlockSpec((None, T, LANES), lambda b, p: (b, 0, nd + p)),
                  pl.BlockSpec((None, T, LANES), lambda b, p: (b, 0, 2 * nd + p)),
                  pl.BlockSpec((None, C, LANES), lambda b, p: (b, 0, nd + p)),
                  pl.BlockSpec((None, C, LANES), lambda b, p: (b, 0, 2 * nd + p)),
                  pl.BlockSpec((2, 2 * WIN_H - 2, GRID_W, 2 * GRID_W), lambda b, p: (p, 0, 0, 0))],
        out_specs=pl.BlockSpec((None, T, LANES), lambda b, p: (b, 0, p)),
        out_shape=jax.ShapeDtypeStruct((B, T, D), BF16),
        compiler_params=_cparams(("arbitrary", "arbitrary")),
        name="nat_attention",
    )(qkv, qkv, qkv, qkv_c, qkv_c, bias)


def _ctx_attn_kernel(q_ref, k_ref, v_ref, o_ref):
    scale = HEAD ** -0.5
    lane = lax.broadcasted_iota(jnp.int32, (1, LANES), 1)
    head_masks = (lane < HEAD, lane >= HEAD)
    q = q_ref[...] * scale
    k = k_ref[...]
    v = v_ref[...]
    zero = jnp.zeros((), BF16)
    outs = []
    for h in range(2):
        s = _dot_nt(jnp.where(head_masks[h], q, zero), k)
        e = jnp.exp(s - jnp.max(s, axis=-1, keepdims=True))
        outs.append(_dot(e.astype(BF16), v) / jnp.sum(e, axis=-1, keepdims=True))
    o_ref[...] = jnp.where(head_masks[0], outs[0], outs[1]).astype(o_ref.dtype)


def _ctx_attention(qkv_c):
    B, C, D3 = qkv_c.shape
    D = D3 // 3
    nd = D // LANES
    return pl.pallas_call(
        _ctx_attn_kernel,
        grid=(B, nd),
        in_specs=[pl.BlockSpec((None, C, LANES), lambda b, p: (b, 0, p)),
                  pl.BlockSpec((None, C, LANES), lambda b, p: (b, 0, nd + p)),
                  pl.BlockSpec((None, C, LANES), lambda b, p: (b, 0, 2 * nd + p))],
        out_specs=pl.BlockSpec((None, C, LANES), lambda b, p: (b, 0, p)),
        out_shape=jax.ShapeDtypeStruct((B, C, D), BF16),
        compiler_params=_cparams(("arbitrary", "arbitrary")),
        name="ctx_attention",
    )(qkv_c, qkv_c, qkv_c)


def _pad_to(w, axis, mult=LANES):
    n = w.shape[axis]
    pad = (-n) % mult
    if pad == 0:
        return w
    widths = [(0, 0)] * w.ndim
    widths[axis] = (0, pad)
    return jnp.pad(w, widths)


def _lora_in(w):
    w = _pad_to(w, 2)
    return jnp.concatenate(list(w), axis=1).astype(BF16)


def _lora_out(w):
    return _pad_to(w, 1).astype(BF16)


def _scan_inputs(q, B, L, D):
    three = lambda t: t.reshape(B, L, D)
    four = lambda t: t.reshape(2, B, L, D)
    return three(q["r"]), three(q["v"]), three(q["kk"]), four(q["lw"]), four(q["kd"]), four(q["a"])


def kernel(x, c, ctx, c_ctx, ada_w, ada_b, norm_pre, norm_post, ffn_w_gate, ffn_w_up, ffn_w_down, rwkv_mix, rwkv_w_r, rwkv_w_k, rwkv_w_v, rwkv_w_o, rwkv_w0, rwkv_w1, rwkv_w2, rwkv_a0, rwkv_a1, rwkv_a2, rwkv_v0, rwkv_v1, rwkv_v2, rwkv_k_k, rwkv_k_a, rwkv_r_k, rwkv_g1, rwkv_g2, rwkv_gn_w, rwkv_gn_b, nat_w_qkv, nat_w_o, nat_rpb):
    B, T, D = x.shape
    C = ctx.shape[1]
    depth = ada_w.shape[0]
    assert B + 1 <= 8 and D % (2 * LANES) == 0
    cvec = jnp.zeros((8, D), F32).at[:B].set(c).at[B].set(c_ctx)
    mods = _adaln(cvec, ada_w, ada_b).reshape(depth, 8, N_MOD, D)
    wg, wu, wd = (w.astype(BF16) for w in (ffn_w_gate, ffn_w_up, ffn_w_down))
    grp_l, grp_c = (0, T), (B, B * C)
    xl = x.reshape(B * T, D)
    xc = ctx.reshape(B * C, D)
    vf_l = vf_c = None
    for i in range(depth):
        last = i == depth - 1
        j = i // 2
        mod = mods[i]
        ffn = functools.partial(_ffn, mod=mod, wg=wg, wu=wu, wd=wd, layer=i)
        xl = ffn(xl, grp=grp_l, slot=0, g_pre=norm_pre[i, 0], g_post=norm_post[i, 0], which=0)
        xc = ffn(xc, grp=grp_c, slot=0, g_pre=norm_pre[i, 0], g_post=norm_post[i, 0], which=0)
        if i % 2 == 0:
            p = dict(
                mix=rwkv_mix[j], w_r=rwkv_w_r[j].astype(BF16), w_k=rwkv_w_k[j].astype(BF16),
                w_v=rwkv_w_v[j].astype(BF16), w0=rwkv_w0[j], w1=_lora_in(rwkv_w1[j]), w2=_lora_out(rwkv_w2[j]),
                a0=rwkv_a0[j], a1=_lora_in(rwkv_a1[j]), a2=_lora_out(rwkv_a2[j]),
                k_k=rwkv_k_k[j], k_a=rwkv_k_a[j], r_k=rwkv_r_k[j].reshape(D),
                g1=_lora_in(rwkv_g1[j][None]), g2=_lora_out(rwkv_g2[j][None]),
                gn_w=rwkv_gn_w[j], gn_b=rwkv_gn_b[j],
                v_res=None if j == 0 else (rwkv_v0[j - 1], _lora_in(rwkv_v1[j - 1][None]), _lora_out(rwkv_v2[j - 1][None])))
            q_c = _rwkv_prep(xc, mod, grp_c, norm_pre[i, 1], p, vf_c, C)
            q_l = _rwkv_prep(xl, mod, grp_l, norm_pre[i, 1], p, vf_l, T)
            if j == 0:
                vf_l, vf_c = q_l["v"], q_c["v"]
            s0 = jnp.zeros((2, B, D // LANES, LANES, LANES), F32)
            y_c, s_c = _wkv_scan(*_scan_inputs(q_c, B, C, D), s0)
            y_l, _ = _wkv_scan(*_scan_inputs(q_l, B, T, D), s_c)
            w_o = rwkv_w_o[j].astype(BF16)
            xl = _rwkv_out(y_l.reshape(2, B * T, D), q_l, p, w_o, xl, mod, grp_l, norm_post[i, 1])
            if not last:
                xc = _rwkv_out(y_c.reshape(2, B * C, D), q_c, p, w_o, xc, mod, grp_c, norm_post[i, 1])
        else:
            w_qkv = nat_w_qkv[j].astype(BF16)
            w_o = nat_w_o[j].astype(BF16)
            qkv_l = _normmod_mm(xl, mod, grp_l, 1, norm_pre[i, 1], w_qkv, BF16).reshape(B, T, 3 * D)
            qkv_c = _normmod_mm(xc, mod, grp_c, 1, norm_pre[i, 1], w_qkv, BF16).reshape(B, C, 3 * D)
            a_l = _nat_attention(qkv_l, qkv_c, _nat_bias_table(nat_rpb[j])).reshape(B * T, D)
            xl = _outproj(a_l, w_o, xl, mod, grp_l, 1, norm_post[i, 1])
            if not last:
                a_c = _ctx_attention(qkv_c).reshape(B * C, D)
                xc = _outproj(a_c, w_o, xc, mod, grp_c, 1, norm_post[i, 1])
        xl = ffn(xl, grp=grp_l, slot=2, g_pre=norm_pre[i, 2], g_post=norm_post[i, 2], which=1)
        if not last:
            xc = ffn(xc, grp=grp_c, slot=2, g_pre=norm_pre[i, 2], g_post=norm_post[i, 2], which=1)
    return xl.reshape(B, T, D)
```

```python
import functools

import jax
import jax.numpy as jnp
from jax import lax
from jax.experimental import pallas as pl
from jax.experimental.pallas import tpu as pltpu

F32 = jnp.float32
BF16 = jnp.bfloat16

LANES = 128
SUBLANES = 8
HEAD = 64
N_MOD = 9
MACARON_WEIGHT = 0.5
RMS_EPS = 1e-6
RWKV_GN_EPS = 64e-5
L2_EPS_SQ = 1e-24
GRID_W = 64
WIN_H = 8
WIN_W = 16
MASK_VALUE = -1e30
VMEM_LIMIT = 56 * 1024 * 1024


def _cparams(sem):
    return pltpu.CompilerParams(dimension_semantics=sem, vmem_limit_bytes=VMEM_LIMIT)


def _dot(a, b):
    return jnp.dot(a, b, preferred_element_type=F32)


def _dot_nt(a, b):
    return lax.dot_general(a, b, (((1,), (1,)), ((), ())), preferred_element_type=F32)


def _dot_tn(a, b):
    return lax.dot_general(a, b, (((0,), (0,)), ((), ())), preferred_element_type=F32)


def _split3(x):
    hi = x.astype(BF16)
    r1 = x - hi.astype(F32)
    mid = r1.astype(BF16)
    lo = (r1 - mid.astype(F32)).astype(BF16)
    return hi, mid, lo


def _dot_exact_lhs01(m01, x):
    hi, mid, lo = _split3(x)
    return _dot(m01, hi) + _dot(m01, mid) + _dot(m01, lo)


def _dot_exact_rhs01(x, m01):
    n = x.shape[0]
    parts = _dot(jnp.concatenate(_split3(x), axis=0), m01)
    return parts[:n] + parts[n:2 * n] + parts[2 * n:]


def _head_ones(n):
    r = lax.broadcasted_iota(jnp.int32, (n, n), 0) // HEAD
    c = lax.broadcasted_iota(jnp.int32, (n, n), 1) // HEAD
    return (r == c).astype(BF16)


def _softplus(x):
    return jnp.maximum(x, 0.0) + jnp.log1p(jnp.exp(-jnp.abs(x)))


def _rms(x, g):
    return x * lax.rsqrt(jnp.mean(x * x, axis=-1, keepdims=True) + RMS_EPS) * g


def _modulated(x, mod_ref, gpre_ref, slot):
    shift = mod_ref[3 * slot:3 * slot + 1, :]
    scale = mod_ref[3 * slot + 1:3 * slot + 2, :]
    return _rms(x, gpre_ref[...]) * (1 + scale) + shift


def _tile(n, want, unit):
    if n <= want:
        return n
    t = (want // unit) * unit
    while n % t:
        t -= unit
    return t


def _mod_spec(base, rows_per_group, tm):
    assert rows_per_group % tm == 0
    return lambda i, *_: (base + (i * tm) // rows_per_group, 0, 0)


def _wkv_kernel(r_ref, v_ref, kk_ref, lw_ref, kd_ref, a_ref, s0_ref, y_ref, sfin_ref, s_scr,
                *, chunk, pairs):
    z = pl.program_id(0)
    c = pl.program_id(3)
    C = chunk

    @pl.when(c == 0)
    def _():
        s_scr[...] = s0_ref[...]

    row = lax.broadcasted_iota(jnp.int32, (C, 2 * C), 0)
    pcol = lax.broadcasted_iota(jnp.int32, (C, 2 * C), 1)
    col = pcol % C
    d = (row - col) * (1 - 2 * z)
    strict = d > 0
    incl = d >= 0
    half_masks = (pcol < C, pcol >= C)
    lane = lax.broadcasted_iota(jnp.int32, (1, LANES), 1)
    head_masks = (lane < HEAD, lane >= HEAD)
    srow = lax.broadcasted_iota(jnp.int32, (LANES, LANES), 0)
    scol = lax.broadcasted_iota(jnp.int32, (LANES, LANES), 1)
    blockdiag = (srow < HEAD) == (scol < HEAD)
    eye = (row == col).astype(BF16)
    levels = []
    s = 1
    while s < C:
        levels.append((row // (2 * s) == col // (2 * s)) & (row // s != col // s))
        s *= 2

    def blockdiag2(m):
        zero = jnp.zeros_like(m)
        return jnp.concatenate([jnp.where(half_masks[0], m, zero), jnp.where(half_masks[1], m, zero)], axis=0)

    def by_head(x):
        zero = jnp.zeros_like(x)
        return jnp.concatenate([jnp.where(head_masks[0], x, zero), jnp.where(head_masks[1], x, zero)], axis=0)

    lw = lw_ref[...]
    kk = kk_ref[...]
    kd = kd_ref[...]
    L = _dot_exact_lhs01(incl[:, :C].astype(BF16), lw)
    l_end = jnp.sum(lw, axis=0, keepdims=True)
    lm = 0.5 * l_end
    e_m = jnp.exp(-lm)
    e_p = jnp.exp(lm)
    e_sh = jnp.exp(lm - L)
    at_t = -kk * jnp.exp(L - lw)
    rt_t = r_ref[...] * jnp.exp(L)
    sr_all = jnp.concatenate([at_t, rt_t], axis=0).astype(BF16)
    at_sh = at_t * e_m
    rt_sh = rt_t * e_m
    bt = (kk * a_ref[...]) * e_sh
    kt = kd * e_sh
    bk_all = jnp.concatenate([bt, kt], axis=0).astype(BF16)
    bkh_all = jnp.concatenate([bt * e_p, kt * e_p], axis=0).astype(BF16)
    v_all = v_ref[...].astype(BF16)
    s_decay = jnp.exp(l_end)

    prs = range(pairs)
    lanes = [slice(p * LANES, (p + 1) * LANES) for p in prs]
    sr = [_dot_nt(sr_all[:, lanes[p]], s_scr[p].astype(BF16)) for p in prs]
    v_h = [by_head(v_all[:, lanes[p]]) for p in prs]
    quads = [_dot_nt(jnp.concatenate([at_sh[:, lanes[p]], rt_sh[:, lanes[p]]], axis=0).astype(BF16),
                     jnp.concatenate([by_head(bk_all[:C, lanes[p]]), by_head(bk_all[C:, lanes[p]])], axis=0))
             for p in prs]
    a_ab = [jnp.where(strict, q[:C, :2 * C], 0.0).astype(BF16) for q in quads]
    a_ak = [jnp.where(strict, q[:C, 2 * C:], 0.0).astype(BF16) for q in quads]
    r_bk = [jnp.concatenate([jnp.where(incl, q[C:, :2 * C], 0.0), jnp.where(incl, q[C:, 2 * C:], 0.0)],
                            axis=1).astype(BF16) for q in quads]
    t = [eye + jnp.where(levels[0], a, jnp.zeros_like(a)) for a in a_ab]
    for lv in levels[1:]:
        x = [jnp.where(lv, a, jnp.zeros_like(a)) for a in a_ab]
        m1 = [_dot(xi, blockdiag2(ti)).astype(BF16) for xi, ti in zip(x, t)]
        m2 = [_dot(ti, blockdiag2(mi)).astype(BF16) for ti, mi in zip(t, m1)]
        t = [ti + mi for ti, mi in zip(t, m2)]
    rhs = [(sr[p][:C] + _dot(a_ak[p], v_h[p])).astype(BF16) for p in prs]
    u = [_dot(t[p], by_head(rhs[p])).astype(BF16) for p in prs]
    y = [sr[p][C:] + _dot(r_bk[p], jnp.concatenate([by_head(u[p]), v_h[p]], axis=0)) for p in prs]
    for p in prs:
        y_ref[:, lanes[p]] = y[p]
        upd = _dot_tn(jnp.concatenate([u[p], v_all[:, lanes[p]]], axis=0), bkh_all[:, lanes[p]])
        s_scr[p] = s_scr[p] * s_decay[:, lanes[p]] + jnp.where(blockdiag, upd, 0.0)

    @pl.when(c == pl.num_programs(3) - 1)
    def _():
        sfin_ref[...] = s_scr[...]


def _wkv_scan(r, v, kk, lw, kd, a, s0, *, chunk=64, pairs=16):
    B, T, D = r.shape
    pairs = min(pairs, D // LANES)
    lw_lanes = pairs * LANES
    assert T % chunk == 0 and D % lw_lanes == 0
    nc = T // chunk
    ng = D // lw_lanes

    def cidx(z, c):
        return c + z * (nc - 1 - 2 * c)

    tok_spec = pl.BlockSpec((None, chunk, lw_lanes), lambda z, b, g, c: (b, cidx(z, c), g))
    dir_spec = pl.BlockSpec((None, None, chunk, lw_lanes), lambda z, b, g, c: (z, b, cidx(z, c), g))
    st_spec = pl.BlockSpec((None, None, pairs, LANES, LANES), lambda z, b, g, c: (z, b, g, 0, 0))
    y, s_fin = pl.pallas_call(
        functools.partial(_wkv_kernel, chunk=chunk, pairs=pairs),
        grid=(2, B, ng, nc),
        in_specs=[tok_spec, tok_spec, tok_spec, dir_spec, dir_spec, dir_spec, st_spec],
        out_specs=[dir_spec, st_spec],
        out_shape=[jax.ShapeDtypeStruct((2, B, T, D), F32),
                   jax.ShapeDtypeStruct(s0.shape, F32)],
        scratch_shapes=[pltpu.VMEM((pairs, LANES, LANES), F32)],
        compiler_params=_cparams(("arbitrary", "arbitrary", "arbitrary", "arbitrary")),
        name="wkv_scan",
    )(r, v, kk, lw, kd, a, s0)
    return y, s_fin


def _rwkv_prep_kernel(*refs, seq_len, tm, has_vres):
    (x_ref, xp_ref, xn_ref, mod_ref, gpre_ref, mix_ref, wr_ref, wk_ref, wv_ref,
     w1_ref, w2_ref, w0_ref, a1_ref, a2_ref, a0_ref, g1_ref, g2_ref, kk_ref, ka_ref) = refs[:19]
    pos = 19
    if has_vres:
        v1_ref, v2_ref, v0_ref, vf_ref = refs[pos:pos + 4]
        pos += 4
    r_ref, v_ref, kkn_ref, g_ref, lw_ref, kd_ref, a_ref = refs[pos:pos + 7]
    pos += 7
    xmix_scr, tw_scr, ta_scr, tg_scr = refs[pos:pos + 4]
    tv_scr = refs[pos + 4] if has_vres else None
    i = pl.program_id(0)
    n = pl.program_id(1)

    @pl.when(n == 0)
    def _():
        h = _modulated(x_ref[...], mod_ref, gpre_ref, 1)
        t0 = i * tm
        hp = _modulated(xp_ref[...], mod_ref, gpre_ref, 1)[SUBLANES - 1:SUBLANES, :]
        hn = _modulated(xn_ref[...], mod_ref, gpre_ref, 1)[0:1, :]
        hp = jnp.where(t0 % seq_len == 0, 0.0, hp)
        hn = jnp.where((t0 + tm) % seq_len == 0, 0.0, hn)
        rowid = lax.broadcasted_iota(jnp.int32, (tm, 1), 0)
        prev = jnp.where(rowid == 0, hp, pltpu.roll(h, 1, 0))
        nxt = jnp.where(rowid == tm - 1, hn, pltpu.roll(h, tm - 1, 0))
        xx = 0.5 * (prev + nxt) - h
        mixed = lambda m: (h + xx * mix_ref[m:m + 1, :]).astype(BF16)
        xmix_scr[0] = mixed(0)
        xmix_scr[1] = mixed(2)
        xv = mixed(3)
        xmix_scr[2] = xv
        tw_scr[...] = jnp.tanh(_dot(mixed(1), w1_ref[...])).astype(BF16)
        ta_scr[...] = _dot(mixed(4), a1_ref[...]).astype(BF16)
        tg_scr[...] = jax.nn.sigmoid(_dot(mixed(5), g1_ref[...])).astype(BF16)
        if has_vres:
            tv_scr[...] = _dot(xv, v1_ref[...]).astype(BF16)

    r_ref[...] = _dot(xmix_scr[0], wr_ref[...])
    k = _dot(xmix_scr[1], wk_ref[...])
    v = _dot(xmix_scr[2], wv_ref[...])
    if has_vres:
        v = v + (vf_ref[...] - v) * jax.nn.sigmoid(v0_ref[...] + _dot(tv_scr[...], v2_ref[...]))
    v_ref[...] = v
    g_ref[...] = _dot(tg_scr[...], g2_ref[...])
    kk = k * kk_ref[...]
    ssq = _dot_exact_rhs01(kk * kk, _head_ones(kk.shape[1]))
    kkn_ref[...] = kk * lax.rsqrt(jnp.maximum(ssq, L2_EPS_SQ))
    lp = w2_ref.shape[1]
    la = a2_ref.shape[1]
    for zz in range(2):
        lora_w = _dot(tw_scr[:, zz * lp:(zz + 1) * lp], w2_ref[zz])
        w_log = -_softplus(-(w0_ref[zz:zz + 1, :] + lora_w)) - 0.5
        lw_ref[zz] = -jnp.exp(w_log)
        a = jax.nn.sigmoid(a0_ref[zz:zz + 1, :] + _dot(ta_scr[:, zz * la:(zz + 1) * la], a2_ref[zz]))
        a_ref[zz] = a
        kd_ref[zz] = k * (1 + (a - 1) * ka_ref[...])


def _rwkv_prep(x, mod, grp, g_pre, p, v_first, seq_len, *, tm=512):
    M, D = x.shape
    tm = _tile(seq_len, tm, SUBLANES)
    tn = p["w_r"].shape[2]
    base, rpg = grp
    wtile = pl.BlockSpec((None, D, tn), lambda i, n: (n, 0, 0))
    has_vres = p["v_res"] is not None
    nb = M // SUBLANES
    row = lambda i, n: (i, 0)
    col = lambda i, n: (0, n)
    full = lambda i, n: (0, 0)
    col3 = lambda i, n: (0, 0, n)
    tile = lambda i, n: (i, n)
    tile3 = lambda i, n: (0, i, n)
    lw1, la1, lg1 = p["w1"].shape[1], p["a1"].shape[1], p["g1"].shape[1]
    in_specs = [
        pl.BlockSpec((tm, D), row),
        pl.BlockSpec((SUBLANES, D), lambda i, n: (jnp.maximum(i * (tm // SUBLANES) - 1, 0), 0)),
        pl.BlockSpec((SUBLANES, D), lambda i, n: (jnp.minimum((i + 1) * (tm // SUBLANES), nb - 1), 0)),
        pl.BlockSpec((None, N_MOD, D), _mod_spec(base, rpg, tm)),
        pl.BlockSpec((1, D), full),
        pl.BlockSpec((6, D), full),
        wtile, wtile, wtile,
        pl.BlockSpec((D, lw1), full), pl.BlockSpec((2, lw1 // 2, tn), col3), pl.BlockSpec((2, tn), col),
        pl.BlockSpec((D, la1), full), pl.BlockSpec((2, la1 // 2, tn), col3), pl.BlockSpec((2, tn), col),
        pl.BlockSpec((D, lg1), full), pl.BlockSpec((lg1, tn), col),
        pl.BlockSpec((1, tn), col), pl.BlockSpec((1, tn), col),
    ]
    args = [x, x, x, mod, g_pre.reshape(1, D), p["mix"], p["w_r"], p["w_k"], p["w_v"],
            p["w1"], p["w2"], p["w0"], p["a1"], p["a2"], p["a0"], p["g1"], p["g2"][0],
            p["k_k"].reshape(1, D), p["k_a"].reshape(1, D)]
    scratch = [pltpu.VMEM((3, tm, D), BF16), pltpu.VMEM((tm, lw1), BF16),
               pltpu.VMEM((tm, la1), BF16), pltpu.VMEM((tm, lg1), BF16)]
    if has_vres:
        v0, v1, v2 = p["v_res"]
        lv1 = v1.shape[1]
        in_specs += [pl.BlockSpec((D, lv1), full), pl.BlockSpec((lv1, tn), col),
                     pl.BlockSpec((1, tn), col), pl.BlockSpec((tm, tn), tile)]
        args += [v1, v2[0], v0.reshape(1, D), v_first]
        scratch.append(pltpu.VMEM((tm, lv1), BF16))
    one = jax.ShapeDtypeStruct((M, D), F32)
    two = jax.ShapeDtypeStruct((2, M, D), F32)
    r, v, kk, g, lw, kd, a = pl.pallas_call(
        functools.partial(_rwkv_prep_kernel, seq_len=seq_len, tm=tm, has_vres=has_vres),
        grid=(M // tm, D // tn),
        in_specs=in_specs,
        out_specs=[pl.BlockSpec((tm, tn), tile)] * 4 + [pl.BlockSpec((2, tm, tn), tile3)] * 3,
        out_shape=[one] * 4 + [two] * 3,
        scratch_shapes=scratch,
        compiler_params=_cparams(("arbitrary", "arbitrary")),
        name="rwkv_prep",
    )(*args)
    return dict(r=r, v=v, kk=kk, g=g, lw=lw, kd=kd, a=a)


def _rwkv_out_kernel(y_ref, r_ref, kd_ref, v_ref, g_ref, rk_ref, gnw_ref, gnb_ref, w_ref, x_ref, mod_ref,
                     gpost_ref, o_ref, pre_scr, *, slab):
    D = x_ref.shape[1]
    ones = _head_ones(slab)
    for j in range(D // slab):
        sl = slice(j * slab, (j + 1) * slab)
        wkv = y_ref[0, :, sl] + y_ref[1, :, sl]
        mu = _dot_exact_rhs01(wkv, ones) * (1.0 / HEAD)
        cen = wkv - mu
        var = _dot_exact_rhs01(cen * cen, ones) * (1.0 / HEAD)
        o = cen * lax.rsqrt(var + RWKV_GN_EPS) * gnw_ref[:, sl] + gnb_ref[:, sl]
        coef = _dot_exact_rhs01(r_ref[:, sl] * (kd_ref[0, :, sl] + kd_ref[1, :, sl]) * rk_ref[:, sl], ones)
        pre_scr[:, sl] = ((o + coef * v_ref[:, sl]) * g_ref[:, sl]).astype(BF16)
    y = _dot(pre_scr[...], w_ref[...])
    o_ref[...] = x_ref[...] + mod_ref[5:6, :] * _rms(y, gpost_ref[...])


def _rwkv_out(y, q, p, w_o, x, mod, grp, g_post, *, tm=256, slab=256):
    M, D = x.shape
    tm = _tile(M, tm, SUBLANES)
    slab = _tile(D, slab, LANES)
    base, rpg = grp
    row = lambda i: (i, 0)
    row3 = lambda i: (0, i, 0)
    full = lambda i: (0, 0)
    return pl.pallas_call(
        functools.partial(_rwkv_out_kernel, slab=slab),
        grid=(M // tm,),
        in_specs=[pl.BlockSpec((2, tm, D), row3), pl.BlockSpec((tm, D), row), pl.BlockSpec((2, tm, D), row3),
                  pl.BlockSpec((tm, D), row), pl.BlockSpec((tm, D), row),
                  pl.BlockSpec((1, D), full), pl.BlockSpec((1, D), full), pl.BlockSpec((1, D), full),
                  pl.BlockSpec((D, D), full, pipeline_mode=pl.Buffered(1)), pl.BlockSpec((tm, D), row),
                  pl.BlockSpec((None, N_MOD, D), _mod_spec(base, rpg, tm)), pl.BlockSpec((1, D), full)],
        out_specs=pl.BlockSpec((tm, D), row),
        out_shape=jax.ShapeDtypeStruct((M, D), F32),
        scratch_shapes=[pltpu.VMEM((tm, D), BF16)],
        compiler_params=_cparams(("arbitrary",)),
        name="rwkv_out",
    )(y, q["r"], q["kd"], q["v"], q["g"], p["r_k"].reshape(1, D), p["gn_w"].reshape(1, D),
      p["gn_b"].reshape(1, D), w_o, x, mod, g_post.reshape(1, D))


def _adaln_kernel(c_ref, w_ref, b_ref, o_ref):
    cv = c_ref[...]
    cv = cv * jax.nn.sigmoid(cv)
    o_ref[...] = _dot(cv.astype(BF16), w_ref[...].astype(BF16)) + b_ref[...]


def _adaln(cvec, ada_w, ada_b, *, bn=1024):
    depth, D, N = ada_w.shape
    bn = _tile(N, bn, LANES)
    return pl.pallas_call(
        _adaln_kernel,
        grid=(depth, N // bn),
        in_specs=[pl.BlockSpec((8, D), lambda i, n: (0, 0)),
                  pl.BlockSpec((None, D, bn), lambda i, n: (i, 0, n)),
                  pl.BlockSpec((None, 1, bn), lambda i, n: (i, 0, n))],
        out_specs=pl.BlockSpec((None, 8, bn), lambda i, n: (i, 0, n)),
        out_shape=jax.ShapeDtypeStruct((depth, 8, N), F32),
        compiler_params=_cparams(("arbitrary", "arbitrary")),
        name="adaln",
    )(cvec, ada_w, ada_b.reshape(depth, 1, N))


def _ffn_kernel(x_ref, mod_ref, gpre_ref, gpost_ref, wg_ref, wu_ref, wd_ref, o_ref, h_scr, acc_scr, *, slot):
    f = pl.program_id(1)

    @pl.when(f == 0)
    def _():
        h_scr[...] = _modulated(x_ref[...], mod_ref, gpre_ref, slot).astype(BF16)
        acc_scr[...] = jnp.zeros_like(acc_scr)

    h = h_scr[...]
    g = _dot(h, wg_ref[...])
    u = _dot(h, wu_ref[...])
    a = (g * jax.nn.sigmoid(g)) * u
    acc_scr[...] += _dot(a.astype(BF16), wd_ref[...])

    @pl.when(f == pl.num_programs(1) - 1)
    def _():
        gate = mod_ref[3 * slot + 2:3 * slot + 3, :]
        o_ref[...] = x_ref[...] + MACARON_WEIGHT * gate * _rms(acc_scr[...], gpost_ref[...])


def _ffn(x, mod, grp, slot, g_pre, g_post, wg, wu, wd, layer, which, *, tm=512):
    M, D = x.shape
    nf, tf = wg.shape[2], wg.shape[4]
    F = nf * tf
    tm = _tile(M, tm, SUBLANES)
    base, rpg = grp
    return pl.pallas_call(
        functools.partial(_ffn_kernel, slot=slot),
        grid=(M // tm, F // tf),
        in_specs=[pl.BlockSpec((tm, D), lambda i, f: (i, 0)),
                  pl.BlockSpec((None, N_MOD, D), _mod_spec(base, rpg, tm)),
                  pl.BlockSpec((1, D), lambda i, f: (0, 0)),
                  pl.BlockSpec((1, D), lambda i, f: (0, 0)),
                  pl.BlockSpec((None, None, None, D, tf), lambda i, f: (layer, which, f, 0, 0)),
                  pl.BlockSpec((None, None, None, D, tf), lambda i, f: (layer, which, f, 0, 0)),
                  pl.BlockSpec((None, None, tf, D), lambda i, f: (layer, which, f, 0))],
        out_specs=pl.BlockSpec((tm, D), lambda i, f: (i, 0)),
        out_shape=jax.ShapeDtypeStruct((M, D), F32),
        scratch_shapes=[pltpu.VMEM((tm, D), BF16), pltpu.VMEM((tm, D), F32)],
        compiler_params=_cparams(("arbitrary", "arbitrary")),
        name="ffn",
    )(x, mod, g_pre.reshape(1, D), g_post.reshape(1, D), wg, wu, wd)


def _normmod_mm_kernel(x_ref, mod_ref, gpre_ref, w_ref, o_ref, h_scr, *, slot):
    @pl.when(pl.program_id(1) == 0)
    def _():
        h_scr[...] = _modulated(x_ref[...], mod_ref, gpre_ref, slot).astype(BF16)

    o_ref[...] = _dot(h_scr[...], w_ref[...]).astype(o_ref.dtype)


def _normmod_mm(x, mod, grp, slot, g_pre, w, out_dtype, *, tm=512):
    M, D = x.shape
    tn = w.shape[2]
    N = w.shape[0] * tn
    tm = _tile(M, tm, SUBLANES)
    base, rpg = grp
    return pl.pallas_call(
        functools.partial(_normmod_mm_kernel, slot=slot),
        grid=(M // tm, N // tn),
        in_specs=[pl.BlockSpec((tm, D), lambda i, n: (i, 0)),
                  pl.BlockSpec((None, N_MOD, D), _mod_spec(base, rpg, tm)),
                  pl.BlockSpec((1, D), lambda i, n: (0, 0)),
                  pl.BlockSpec((None, D, tn), lambda i, n: (n, 0, 0))],
        out_specs=pl.BlockSpec((tm, tn), lambda i, n: (i, n)),
        out_shape=jax.ShapeDtypeStruct((M, N), out_dtype),
        scratch_shapes=[pltpu.VMEM((tm, D), BF16)],
        compiler_params=_cparams(("arbitrary", "arbitrary")),
        name="normmod_mm",
    )(x, mod, g_pre.reshape(1, D), w)


def _outproj_kernel(a_ref, w_ref, x_ref, mod_ref, gpost_ref, o_ref, *, slot):
    y = _dot(a_ref[...], w_ref[...])
    gate = mod_ref[3 * slot + 2:3 * slot + 3, :]
    o_ref[...] = x_ref[...] + gate * _rms(y, gpost_ref[...])


def _outproj(a, w, x, mod, grp, slot, g_post, *, tm=512):
    M, D = x.shape
    K = a.shape[1]
    tm = _tile(M, tm, SUBLANES)
    base, rpg = grp
    return pl.pallas_call(
        functools.partial(_outproj_kernel, slot=slot),
        grid=(M // tm,),
        in_specs=[pl.BlockSpec((tm, K), lambda i: (i, 0)),
                  pl.BlockSpec((K, D), lambda i: (0, 0), pipeline_mode=pl.Buffered(1)),
                  pl.BlockSpec((tm, D), lambda i: (i, 0)),
                  pl.BlockSpec((None, N_MOD, D), _mod_spec(base, rpg, tm)),
                  pl.BlockSpec((1, D), lambda i: (0, 0))],
        out_specs=pl.BlockSpec((tm, D), lambda i: (i, 0)),
        out_shape=jax.ShapeDtypeStruct((M, D), F32),
        compiler_params=_cparams(("arbitrary",)),
        name="outproj",
    )(a, w, x, mod, g_post.reshape(1, D))


def _nat_kernel(q_ref, k_ref, v_ref, kc_ref, vc_ref, bias_ref, o_ref, *, rows, group):
    scale = HEAD ** -0.5
    n_lat = WIN_H * GRID_W
    lane = lax.broadcasted_iota(jnp.int32, (1, LANES), 1)
    head_masks = (lane < HEAD, lane >= HEAD)
    qcol = lax.broadcasted_iota(jnp.int32, (2 * GRID_W, n_lat), 0) % GRID_W
    kcol = lax.broadcasted_iota(jnp.int32, (2 * GRID_W, n_lat), 1) % GRID_W
    cstart = jnp.clip(qcol - WIN_W // 2, 0, GRID_W - WIN_W)
    col_ok = (kcol >= cstart) & (kcol < cstart + WIN_W)
    kc = kc_ref[...]
    vc = vc_ref[...]
    zero = jnp.zeros((), BF16)

    def step(i, carry):
        chains = []
        for rr in range(group):
            r = i * group + rr
            r0 = jnp.clip(r - WIN_H // 2, 0, rows - WIN_H)
            q = q_ref[pl.ds(pl.multiple_of(r * GRID_W, GRID_W), GRID_W), :] * scale
            q2 = jnp.concatenate([jnp.where(head_masks[0], q, zero), jnp.where(head_masks[1], q, zero)], axis=0)
            kw = k_ref[pl.ds(pl.multiple_of(r0 * GRID_W, GRID_W), n_lat), :]
            vw = v_ref[pl.ds(pl.multiple_of(r0 * GRID_W, GRID_W), n_lat), :]
            chains.append((r, r0, q2, kw, vw))
        s = [_dot_nt(q2, kw) for (_, _, q2, kw, _) in chains]
        sc = [_dot_nt(q2, kc) for (_, _, q2, _, _) in chains]
        s = [jnp.where(col_ok, si + jnp.concatenate(
                [jnp.concatenate([bias_ref[h, 2 * jj - (r - r0) + WIN_H - 1] for jj in range(WIN_H // 2)], axis=1)
                 for h in range(2)], axis=0), MASK_VALUE)
             for si, (r, r0, _, _, _) in zip(s, chains)]
        m = [jnp.maximum(jnp.max(si, axis=-1, keepdims=True), jnp.max(ci, axis=-1, keepdims=True))
             for si, ci in zip(s, sc)]
        e = [jnp.exp(si - mi) for si, mi in zip(s, m)]
        ec = [jnp.exp(ci - mi) for ci, mi in zip(sc, m)]
        den = [jnp.sum(ei, axis=-1, keepdims=True) + jnp.sum(ci, axis=-1, keepdims=True) for ei, ci in zip(e, ec)]
        o = [(_dot(ei.astype(BF16), vw) + _dot(ci.astype(BF16), vc)) / di
             for ei, ci, di, (_, _, _, _, vw) in zip(e, ec, den, chains)]
        for rr in range(group):
            r = i * group + rr
            o_ref[pl.ds(pl.multiple_of(r * GRID_W, GRID_W), GRID_W), :] = jnp.where(
                head_masks[0], o[rr][:GRID_W], o[rr][GRID_W:]).astype(o_ref.dtype)
        return carry

    lax.fori_loop(0, rows // group, step, 0)


def _nat_bias_table(rpb):
    edge = GRID_W - WIN_W
    padded = jnp.pad(rpb, ((0, 0), (0, 0), (edge, edge)), mode="edge")
    rows = jnp.stack([padded[:, :, GRID_W - 1 - q:2 * GRID_W - 1 - q] for q in range(GRID_W)], axis=2)
    return jnp.concatenate([rows[:, :-1], rows[:, 1:]], axis=-1).astype(F32)


def _nat_attention(qkv, qkv_c, bias, *, group=4):
    B, T, D3 = qkv.shape
    D = D3 // 3
    C = qkv_c.shape[1]
    nd = D // LANES
    rows = T // GRID_W
    assert rows >= WIN_H and rows % group == 0
    return pl.pallas_call(
        functools.partial(_nat_kernel, rows=rows, group=group),
        grid=(B, nd),
        in_specs=[pl.BlockSpec((None, T, LANES), lambda b, p: (b, 0, p)),
                  pl.BlockSpec((None, T, LANES), lambda b, p: (b, 0, nd + p)),
                  pl.BlockSpec((None, T, LANES), lambda b, p: (b, 0, 2 * nd + p)),
                  pl.BlockSpec((None, C, LANES), lambda b, p: (b, 0, nd + p)),
                  pl.BlockSpec((None, C, LANES), lambda b, p: (b, 0, 2 * nd + p)),
                  pl.BlockSpec((2, 2 * WIN_H - 2, GRID_W, 2 * GRID_W), lambda b, p: (p, 0, 0, 0))],
        out_specs=pl.BlockSpec((None, T, LANES), lambda b, p: (b, 0, p)),
        out_shape=jax.ShapeDtypeStruct((B, T, D), BF16),
        compiler_params=_cparams(("arbitrary", "arbitrary")),
        name="nat_attention",
    )(qkv, qkv, qkv, qkv_c, qkv_c, bias)


def _ctx_attn_kernel(q_ref, k_ref, v_ref, o_ref):
    scale = HEAD ** -0.5
    lane = lax.broadcasted_iota(jnp.int32, (1, LANES), 1)
    head_masks = (lane < HEAD, lane >= HEAD)
    q = q_ref[...] * scale
    k = k_ref[...]
    v = v_ref[...]
    zero = jnp.zeros((), BF16)
    outs = []
    for h in range(2):
        s = _dot_nt(jnp.where(head_masks[h], q, zero), k)
        e = jnp.exp(s - jnp.max(s, axis=-1, keepdims=True))
        outs.append(_dot(e.astype(BF16), v) / jnp.sum(e, axis=-1, keepdims=True))
    o_ref[...] = jnp.where(head_masks[0], outs[0], outs[1]).astype(o_ref.dtype)


def _ctx_attention(qkv_c):
    B, C, D3 = qkv_c.shape
    D = D3 // 3
    nd = D // LANES
    return pl.pallas_call(
        _ctx_attn_kernel,
        grid=(B, nd),
        in_specs=[pl.BlockSpec((None, C, LANES), lambda b, p: (b, 0, p)),
                  pl.BlockSpec((None, C, LANES), lambda b, p: (b, 0, nd + p)),
                  pl.BlockSpec((None, C, LANES), lambda b, p: (b, 0, 2 * nd + p))],
        out_specs=pl.BlockSpec((None, C, LANES), lambda b, p: (b, 0, p)),
        out_shape=jax.ShapeDtypeStruct((B, C, D), BF16),
        compiler_params=_cparams(("arbitrary", "arbitrary")),
        name="ctx_attention",
    )(qkv_c, qkv_c, qkv_c)


def _pad_to(w, axis, mult=LANES):
    n = w.shape[axis]
    pad = (-n) % mult
    if pad == 0:
        return w
    widths = [(0, 0)] * w.ndim
    widths[axis] = (0, pad)
    return jnp.pad(w, widths)


def _col_tiles(w, tn):
    *lead, K, N = w.shape
    tn = _tile(N, tn, LANES)
    w = w.astype(BF16).reshape(*lead, K, N // tn, tn)
    return jnp.swapaxes(w, -3, -2)


def _lora_in(w):
    w = _pad_to(w, 2)
    return jnp.concatenate(list(w), axis=1).astype(BF16)


def _lora_out(w):
    return _pad_to(w, 1).astype(BF16)


def _scan_inputs(q, B, L, D):
    three = lambda t: t.reshape(B, L, D)
    four = lambda t: t.reshape(2, B, L, D)
    return three(q["r"]), three(q["v"]), three(q["kk"]), four(q["lw"]), four(q["kd"]), four(q["a"])


def kernel(x, c, ctx, c_ctx, ada_w, ada_b, norm_pre, norm_post, ffn_w_gate, ffn_w_up, ffn_w_down, rwkv_mix, rwkv_w_r, rwkv_w_k, rwkv_w_v, rwkv_w_o, rwkv_w0, rwkv_w1, rwkv_w2, rwkv_a0, rwkv_a1, rwkv_a2, rwkv_v0, rwkv_v1, rwkv_v2, rwkv_k_k, rwkv_k_a, rwkv_r_k, rwkv_g1, rwkv_g2, rwkv_gn_w, rwkv_gn_b, nat_w_qkv, nat_w_o, nat_rpb):
    B, T, D = x.shape
    C = ctx.shape[1]
    depth = ada_w.shape[0]
    assert B + 1 <= 8 and D % (2 * LANES) == 0
    cvec = jnp.zeros((8, D), F32).at[:B].set(c).at[B].set(c_ctx)
    mods = _adaln(cvec, ada_w, ada_b).reshape(depth, 8, N_MOD, D)
    wg, wu, wd = _col_tiles(ffn_w_gate, 512), _col_tiles(ffn_w_up, 512), ffn_w_down.astype(BF16)
    grp_l, grp_c = (0, T), (B, B * C)
    xl = x.reshape(B * T, D)
    xc = ctx.reshape(B * C, D)
    vf_l = vf_c = None
    for i in range(depth):
        last = i == depth - 1
        j = i // 2
        mod = mods[i]
        ffn = functools.partial(_ffn, mod=mod, wg=wg, wu=wu, wd=wd, layer=i)
        xl = ffn(xl, grp=grp_l, slot=0, g_pre=norm_pre[i, 0], g_post=norm_post[i, 0], which=0)
        xc = ffn(xc, grp=grp_c, slot=0, g_pre=norm_pre[i, 0], g_post=norm_post[i, 0], which=0)
        if i % 2 == 0:
            p = dict(
                mix=rwkv_mix[j], w_r=_col_tiles(rwkv_w_r[j], 256), w_k=_col_tiles(rwkv_w_k[j], 256),
                w_v=_col_tiles(rwkv_w_v[j], 256), w0=rwkv_w0[j], w1=_lora_in(rwkv_w1[j]), w2=_lora_out(rwkv_w2[j]),
                a0=rwkv_a0[j], a1=_lora_in(rwkv_a1[j]), a2=_lora_out(rwkv_a2[j]),
                k_k=rwkv_k_k[j], k_a=rwkv_k_a[j], r_k=rwkv_r_k[j].reshape(D),
                g1=_lora_in(rwkv_g1[j][None]), g2=_lora_out(rwkv_g2[j][None]),
                gn_w=rwkv_gn_w[j], gn_b=rwkv_gn_b[j],
                v_res=None if j == 0 else (rwkv_v0[j - 1], _lora_in(rwkv_v1[j - 1][None]), _lora_out(rwkv_v2[j - 1][None])))
            q_c = _rwkv_prep(xc, mod, grp_c, norm_pre[i, 1], p, vf_c, C)
            q_l = _rwkv_prep(xl, mod, grp_l, norm_pre[i, 1], p, vf_l, T)
            if j == 0:
                vf_l, vf_c = q_l["v"], q_c["v"]
            s0 = jnp.zeros((2, B, D // LANES, LANES, LANES), F32)
            y_c, s_c = _wkv_scan(*_scan_inputs(q_c, B, C, D), s0)
            y_l, _ = _wkv_scan(*_scan_inputs(q_l, B, T, D), s_c)
            w_o = rwkv_w_o[j].astype(BF16)
            xl = _rwkv_out(y_l.reshape(2, B * T, D), q_l, p, w_o, xl, mod, grp_l, norm_post[i, 1])
            if not last:
                xc = _rwkv_out(y_c.reshape(2, B * C, D), q_c, p, w_o, xc, mod, grp_c, norm_post[i, 1])
        else:
            w_qkv = _col_tiles(nat_w_qkv[j], 512)
            w_o = nat_w_o[j].astype(BF16)
            qkv_l = _normmod_mm(xl, mod, grp_l, 1, norm_pre[i, 1], w_qkv, BF16).reshape(B, T, 3 * D)
            qkv_c = _normmod_mm(xc, mod, grp_c, 1, norm_pre[i, 1], w_qkv, BF16).reshape(B, C, 3 * D)
            a_l = _nat_attention(qkv_l, qkv_c, _nat_bias_table(nat_rpb[j])).reshape(B * T, D)
            xl = _outproj(a_l, w_o, xl, mod, grp_l, 1, norm_post[i, 1])
            if not last:
                a_c = _ctx_attention(qkv_c).reshape(B * C, D)
                xc = _outproj(a_c, w_o, xc, mod, grp_c, 1, norm_post[i, 1])
        xl = ffn(xl, grp=grp_l, slot=2, g_pre=norm_pre[i, 2], g_post=norm_post[i, 2], which=1)
        if not last:
            xc = ffn(xc, grp=grp_c, slot=2, g_pre=norm_pre[i, 2], g_post=norm_post[i, 2], which=1)
    return xl.reshape(B, T, D)
```

```python
import functools

import jax
import jax.numpy as jnp
from jax import lax
from jax.experimental import pallas as pl
from jax.experimental.pallas import tpu as pltpu

F32 = jnp.float32
BF16 = jnp.bfloat16

LANES = 128
SUBLANES = 8
ROW_SLAB = 32
HEAD = 64
N_MOD = 9
MACARON_WEIGHT = 0.5
RMS_EPS = 1e-6
RWKV_GN_EPS = 64e-5
L2_EPS_SQ = 1e-24
GRID_W = 64
WIN_H = 8
WIN_W = 16
MASK_VALUE = -1e30
VMEM_LIMIT = 56 * 1024 * 1024


def _cparams(sem):
    return pltpu.CompilerParams(dimension_semantics=sem, vmem_limit_bytes=VMEM_LIMIT)


def _dot(a, b):
    return jnp.dot(a, b, preferred_element_type=F32)


def _dot_nt(a, b):
    return lax.dot_general(a, b, (((1,), (1,)), ((), ())), preferred_element_type=F32)


def _dot_tn(a, b):
    return lax.dot_general(a, b, (((0,), (0,)), ((), ())), preferred_element_type=F32)


def _split3(x):
    hi = x.astype(BF16)
    r1 = x - hi.astype(F32)
    mid = r1.astype(BF16)
    lo = (r1 - mid.astype(F32)).astype(BF16)
    return hi, mid, lo


def _dot_exact_lhs01(m01, x):
    hi, mid, lo = _split3(x)
    return _dot(m01, hi) + _dot(m01, mid) + _dot(m01, lo)


def _dot_exact_rhs01(x, m01):
    n = x.shape[0]
    parts = _dot(jnp.concatenate(_split3(x), axis=0), m01)
    return parts[:n] + parts[n:2 * n] + parts[2 * n:]


def _head_ones(n):
    r = lax.broadcasted_iota(jnp.int32, (n, n), 0) // HEAD
    c = lax.broadcasted_iota(jnp.int32, (n, n), 1) // HEAD
    return (r == c).astype(BF16)


def _softplus(x):
    return jnp.maximum(x, 0.0) + jnp.log1p(jnp.exp(-jnp.abs(x)))


def _rms(x, g):
    return x * lax.rsqrt(jnp.mean(x * x, axis=-1, keepdims=True) + RMS_EPS) * g


def _modulated(x, mod_ref, gpre_ref, slot):
    shift = mod_ref[3 * slot:3 * slot + 1, :]
    scale = mod_ref[3 * slot + 1:3 * slot + 2, :]
    return _rms(x, gpre_ref[...]) * (1 + scale) + shift


def _tile(n, want, unit):
    if n <= want:
        return n
    t = (want // unit) * unit
    while n % t:
        t -= unit
    return t


def _row_slabs(rows, fn):
    slab = _tile(rows, ROW_SLAB, 2 * SUBLANES)

    def body(i, carry):
        fn(pl.ds(pl.multiple_of(i * slab, slab), slab))
        return carry

    lax.fori_loop(0, rows // slab, body, 0, unroll=min(8, rows // slab))


def _modulated_rows(x_ref, mod_ref, gpre_ref, slot, dst_ref):
    shift = mod_ref[3 * slot:3 * slot + 1, :]
    scale1 = 1 + mod_ref[3 * slot + 1:3 * slot + 2, :]
    g = gpre_ref[...]

    def slab(sl):
        dst_ref[sl, :] = (_rms(x_ref[sl, :], g) * scale1 + shift).astype(dst_ref.dtype)

    _row_slabs(x_ref.shape[0], slab)


def _gated_residual_rows(x_ref, y_ref, wgate, gpost_ref, o_ref):
    g = gpost_ref[...]

    def slab(sl):
        o_ref[sl, :] = x_ref[sl, :] + wgate * _rms(y_ref[sl, :], g)

    _row_slabs(x_ref.shape[0], slab)


def _mod_spec(base, rows_per_group, tm):
    assert rows_per_group % tm == 0
    return lambda i, *_: (base + (i * tm) // rows_per_group, 0, 0)


def _wkv_kernel(r_ref, v_ref, kk_ref, lw_ref, kd_ref, a_ref, s0_ref, y_ref, sfin_ref, s_scr,
                *, chunk, pairs):
    z = pl.program_id(0)
    c = pl.program_id(3)
    C = chunk

    @pl.when(c == 0)
    def _():
        s_scr[...] = s0_ref[...]

    row = lax.broadcasted_iota(jnp.int32, (C, 2 * C), 0)
    pcol = lax.broadcasted_iota(jnp.int32, (C, 2 * C), 1)
    col = pcol % C
    d = (row - col) * (1 - 2 * z)
    strict = d > 0
    incl = d >= 0
    half_masks = (pcol < C, pcol >= C)
    lane = lax.broadcasted_iota(jnp.int32, (1, LANES), 1)
    head_masks = (lane < HEAD, lane >= HEAD)
    srow = lax.broadcasted_iota(jnp.int32, (LANES, LANES), 0)
    scol = lax.broadcasted_iota(jnp.int32, (LANES, LANES), 1)
    blockdiag = (srow < HEAD) == (scol < HEAD)
    eye = (row == col).astype(BF16)
    levels = []
    s = 1
    while s < C:
        levels.append((row // (2 * s) == col // (2 * s)) & (row // s != col // s))
        s *= 2

    def blockdiag2(m):
        zero = jnp.zeros_like(m)
        return jnp.concatenate([jnp.where(half_masks[0], m, zero), jnp.where(half_masks[1], m, zero)], axis=0)

    def by_head(x):
        zero = jnp.zeros_like(x)
        return jnp.concatenate([jnp.where(head_masks[0], x, zero), jnp.where(head_masks[1], x, zero)], axis=0)

    lw = lw_ref[...]
    kk = kk_ref[...]
    kd = kd_ref[...]
    L = _dot_exact_lhs01(incl[:, :C].astype(BF16), lw)
    l_end = jnp.sum(lw, axis=0, keepdims=True)
    lm = 0.5 * l_end
    e_m = jnp.exp(-lm)
    e_p = jnp.exp(lm)
    e_sh = jnp.exp(lm - L)
    at_t = -kk * jnp.exp(L - lw)
    rt_t = r_ref[...] * jnp.exp(L)
    sr_all = jnp.concatenate([at_t, rt_t], axis=0).astype(BF16)
    at_sh = at_t * e_m
    rt_sh = rt_t * e_m
    bt = (kk * a_ref[...]) * e_sh
    kt = kd * e_sh
    bk_all = jnp.concatenate([bt, kt], axis=0).astype(BF16)
    bkh_all = jnp.concatenate([bt * e_p, kt * e_p], axis=0).astype(BF16)
    v_all = v_ref[...].astype(BF16)
    s_decay = jnp.exp(l_end)

    prs = range(pairs)
    lanes = [slice(p * LANES, (p + 1) * LANES) for p in prs]
    sr = [_dot_nt(sr_all[:, lanes[p]], s_scr[p].astype(BF16)) for p in prs]
    v_h = [by_head(v_all[:, lanes[p]]) for p in prs]
    quads = [_dot_nt(jnp.concatenate([at_sh[:, lanes[p]], rt_sh[:, lanes[p]]], axis=0).astype(BF16),
                     jnp.concatenate([by_head(bk_all[:C, lanes[p]]), by_head(bk_all[C:, lanes[p]])], axis=0))
             for p in prs]
    a_ab = [jnp.where(strict, q[:C, :2 * C], 0.0).astype(BF16) for q in quads]
    a_ak = [jnp.where(strict, q[:C, 2 * C:], 0.0).astype(BF16) for q in quads]
    r_bk = [jnp.concatenate([jnp.where(incl, q[C:, :2 * C], 0.0), jnp.where(incl, q[C:, 2 * C:], 0.0)],
                            axis=1).astype(BF16) for q in quads]
    t = [eye + jnp.where(levels[0], a, jnp.zeros_like(a)) for a in a_ab]
    for lv in levels[1:]:
        x = [jnp.where(lv, a, jnp.zeros_like(a)) for a in a_ab]
        m1 = [_dot(xi, blockdiag2(ti)).astype(BF16) for xi, ti in zip(x, t)]
        m2 = [_dot(ti, blockdiag2(mi)).astype(BF16) for ti, mi in zip(t, m1)]
        t = [ti + mi for ti, mi in zip(t, m2)]
    rhs = [(sr[p][:C] + _dot(a_ak[p], v_h[p])).astype(BF16) for p in prs]
    u = [_dot(t[p], by_head(rhs[p])).astype(BF16) for p in prs]
    y = [sr[p][C:] + _dot(r_bk[p], jnp.concatenate([by_head(u[p]), v_h[p]], axis=0)) for p in prs]
    for p in prs:
        y_ref[:, lanes[p]] = y[p]
        upd = _dot_tn(jnp.concatenate([u[p], v_all[:, lanes[p]]], axis=0), bkh_all[:, lanes[p]])
        s_scr[p] = s_scr[p] * s_decay[:, lanes[p]] + jnp.where(blockdiag, upd, 0.0)

    @pl.when(c == pl.num_programs(3) - 1)
    def _():
        sfin_ref[...] = s_scr[...]


def _wkv_scan(r, v, kk, lw, kd, a, s0, *, chunk=64, pairs=16):
    B, T, D = r.shape
    pairs = min(pairs, D // LANES)
    lw_lanes = pairs * LANES
    assert T % chunk == 0 and D % lw_lanes == 0
    nc = T // chunk
    ng = D // lw_lanes

    def cidx(z, c):
        return c + z * (nc - 1 - 2 * c)

    tok_spec = pl.BlockSpec((None, chunk, lw_lanes), lambda z, b, g, c: (b, cidx(z, c), g))
    dir_spec = pl.BlockSpec((None, None, chunk, lw_lanes), lambda z, b, g, c: (z, b, cidx(z, c), g))
    st_spec = pl.BlockSpec((None, None, pairs, LANES, LANES), lambda z, b, g, c: (z, b, g, 0, 0))
    y, s_fin = pl.pallas_call(
        functools.partial(_wkv_kernel, chunk=chunk, pairs=pairs),
        grid=(2, B, ng, nc),
        in_specs=[tok_spec, tok_spec, tok_spec, dir_spec, dir_spec, dir_spec, st_spec],
        out_specs=[dir_spec, st_spec],
        out_shape=[jax.ShapeDtypeStruct((2, B, T, D), F32),
                   jax.ShapeDtypeStruct(s0.shape, F32)],
        scratch_shapes=[pltpu.VMEM((pairs, LANES, LANES), F32)],
        compiler_params=_cparams(("arbitrary", "arbitrary", "arbitrary", "arbitrary")),
        name="wkv_scan",
    )(r, v, kk, lw, kd, a, s0)
    return y, s_fin


def _rwkv_prep_kernel(*refs, seq_len, tm, has_vres):
    (x_ref, xp_ref, xn_ref, mod_ref, gpre_ref, mix_ref, wr_ref, wk_ref, wv_ref,
     w1_ref, w2_ref, w0_ref, a1_ref, a2_ref, a0_ref, g1_ref, g2_ref, kk_ref, ka_ref) = refs[:19]
    pos = 19
    if has_vres:
        v1_ref, v2_ref, v0_ref, vf_ref = refs[pos:pos + 4]
        pos += 4
    r_ref, v_ref, kkn_ref, g_ref, lw_ref, kd_ref, a_ref = refs[pos:pos + 7]
    pos += 7
    xmix_scr, tw_scr, ta_scr, tg_scr = refs[pos:pos + 4]
    tv_scr = refs[pos + 4] if has_vres else None
    i = pl.program_id(0)
    n = pl.program_id(1)

    @pl.when(n == 0)
    def _():
        h = _modulated(x_ref[...], mod_ref, gpre_ref, 1)
        t0 = i * tm
        hp = _modulated(xp_ref[...], mod_ref, gpre_ref, 1)[SUBLANES - 1:SUBLANES, :]
        hn = _modulated(xn_ref[...], mod_ref, gpre_ref, 1)[0:1, :]
        hp = jnp.where(t0 % seq_len == 0, 0.0, hp)
        hn = jnp.where((t0 + tm) % seq_len == 0, 0.0, hn)
        rowid = lax.broadcasted_iota(jnp.int32, (tm, 1), 0)
        prev = jnp.where(rowid == 0, hp, pltpu.roll(h, 1, 0))
        nxt = jnp.where(rowid == tm - 1, hn, pltpu.roll(h, tm - 1, 0))
        xx = 0.5 * (prev + nxt) - h
        mixed = lambda m: (h + xx * mix_ref[m:m + 1, :]).astype(BF16)
        xmix_scr[0] = mixed(0)
        xmix_scr[1] = mixed(2)
        xv = mixed(3)
        xmix_scr[2] = xv
        tw_scr[...] = jnp.tanh(_dot(mixed(1), w1_ref[...])).astype(BF16)
        ta_scr[...] = _dot(mixed(4), a1_ref[...]).astype(BF16)
        tg_scr[...] = jax.nn.sigmoid(_dot(mixed(5), g1_ref[...])).astype(BF16)
        if has_vres:
            tv_scr[...] = _dot(xv, v1_ref[...]).astype(BF16)

    r_ref[...] = _dot(xmix_scr[0], wr_ref[...])
    k = _dot(xmix_scr[1], wk_ref[...])
    v = _dot(xmix_scr[2], wv_ref[...])
    if has_vres:
        v = v + (vf_ref[...] - v) * jax.nn.sigmoid(v0_ref[...] + _dot(tv_scr[...], v2_ref[...]))
    v_ref[...] = v
    g_ref[...] = _dot(tg_scr[...], g2_ref[...])
    kk = k * kk_ref[...]
    ssq = _dot_exact_rhs01(kk * kk, _head_ones(kk.shape[1]))
    kkn_ref[...] = kk * lax.rsqrt(jnp.maximum(ssq, L2_EPS_SQ))
    lp = w2_ref.shape[1]
    la = a2_ref.shape[1]
    for zz in range(2):
        lora_w = _dot(tw_scr[:, zz * lp:(zz + 1) * lp], w2_ref[zz])
        w_log = -_softplus(-(w0_ref[zz:zz + 1, :] + lora_w)) - 0.5
        lw_ref[zz] = -jnp.exp(w_log)
        a = jax.nn.sigmoid(a0_ref[zz:zz + 1, :] + _dot(ta_scr[:, zz * la:(zz + 1) * la], a2_ref[zz]))
        a_ref[zz] = a
        kd_ref[zz] = k * (1 + (a - 1) * ka_ref[...])


def _rwkv_prep(x, mod, grp, g_pre, p, v_first, seq_len, *, tm=512, tn=256):
    M, D = x.shape
    tm = _tile(seq_len, tm, SUBLANES)
    tn = _tile(D, tn, LANES)
    base, rpg = grp
    wtile = pl.BlockSpec((D, tn), lambda i, n: (0, n))
    has_vres = p["v_res"] is not None
    nb = M // SUBLANES
    row = lambda i, n: (i, 0)
    col = lambda i, n: (0, n)
    full = lambda i, n: (0, 0)
    col3 = lambda i, n: (0, 0, n)
    tile = lambda i, n: (i, n)
    tile3 = lambda i, n: (0, i, n)
    lw1, la1, lg1 = p["w1"].shape[1], p["a1"].shape[1], p["g1"].shape[1]
    in_specs = [
        pl.BlockSpec((tm, D), row),
        pl.BlockSpec((SUBLANES, D), lambda i, n: (jnp.maximum(i * (tm // SUBLANES) - 1, 0), 0)),
        pl.BlockSpec((SUBLANES, D), lambda i, n: (jnp.minimum((i + 1) * (tm // SUBLANES), nb - 1), 0)),
        pl.BlockSpec((None, N_MOD, D), _mod_spec(base, rpg, tm)),
        pl.BlockSpec((1, D), full),
        pl.BlockSpec((6, D), full),
        wtile, wtile, wtile,
        pl.BlockSpec((D, lw1), full), pl.BlockSpec((2, lw1 // 2, tn), col3), pl.BlockSpec((2, tn), col),
        pl.BlockSpec((D, la1), full), pl.BlockSpec((2, la1 // 2, tn), col3), pl.BlockSpec((2, tn), col),
        pl.BlockSpec((D, lg1), full), pl.BlockSpec((lg1, tn), col),
        pl.BlockSpec((1, tn), col), pl.BlockSpec((1, tn), col),
    ]
    args = [x, x, x, mod, g_pre.reshape(1, D), p["mix"], p["w_r"], p["w_k"], p["w_v"],
            p["w1"], p["w2"], p["w0"], p["a1"], p["a2"], p["a0"], p["g1"], p["g2"][0],
            p["k_k"].reshape(1, D), p["k_a"].reshape(1, D)]
    scratch = [pltpu.VMEM((3, tm, D), BF16), pltpu.VMEM((tm, lw1), BF16),
               pltpu.VMEM((tm, la1), BF16), pltpu.VMEM((tm, lg1), BF16)]
    if has_vres:
        v0, v1, v2 = p["v_res"]
        lv1 = v1.shape[1]
        in_specs += [pl.BlockSpec((D, lv1), full), pl.BlockSpec((lv1, tn), col),
                     pl.BlockSpec((1, tn), col), pl.BlockSpec((tm, tn), tile)]
        args += [v1, v2[0], v0.reshape(1, D), v_first]
        scratch.append(pltpu.VMEM((tm, lv1), BF16))
    one = jax.ShapeDtypeStruct((M, D), F32)
    two = jax.ShapeDtypeStruct((2, M, D), F32)
    r, v, kk, g, lw, kd, a = pl.pallas_call(
        functools.partial(_rwkv_prep_kernel, seq_len=seq_len, tm=tm, has_vres=has_vres),
        grid=(M // tm, D // tn),
        in_specs=in_specs,
        out_specs=[pl.BlockSpec((tm, tn), tile)] * 4 + [pl.BlockSpec((2, tm, tn), tile3)] * 3,
        out_shape=[one] * 4 + [two] * 3,
        scratch_shapes=scratch,
        compiler_params=_cparams(("arbitrary", "arbitrary")),
        name="rwkv_prep",
    )(*args)
    return dict(r=r, v=v, kk=kk, g=g, lw=lw, kd=kd, a=a)


def _rwkv_out_kernel(y_ref, r_ref, kd_ref, v_ref, g_ref, rk_ref, gnw_ref, gnb_ref, w_ref, x_ref, mod_ref,
                     gpost_ref, o_ref, pre_scr, *, slab):
    D = x_ref.shape[1]
    ones = _head_ones(slab)
    for j in range(D // slab):
        sl = slice(j * slab, (j + 1) * slab)
        wkv = y_ref[0, :, sl] + y_ref[1, :, sl]
        mu = _dot_exact_rhs01(wkv, ones) * (1.0 / HEAD)
        cen = wkv - mu
        var = _dot_exact_rhs01(cen * cen, ones) * (1.0 / HEAD)
        o = cen * lax.rsqrt(var + RWKV_GN_EPS) * gnw_ref[:, sl] + gnb_ref[:, sl]
        coef = _dot_exact_rhs01(r_ref[:, sl] * (kd_ref[0, :, sl] + kd_ref[1, :, sl]) * rk_ref[:, sl], ones)
        pre_scr[:, sl] = ((o + coef * v_ref[:, sl]) * g_ref[:, sl]).astype(BF16)
    y = _dot(pre_scr[...], w_ref[...])
    o_ref[...] = x_ref[...] + mod_ref[5:6, :] * _rms(y, gpost_ref[...])


def _rwkv_out(y, q, p, w_o, x, mod, grp, g_post, *, tm=256, slab=256):
    M, D = x.shape
    tm = _tile(M, tm, SUBLANES)
    slab = _tile(D, slab, LANES)
    base, rpg = grp
    row = lambda i: (i, 0)
    row3 = lambda i: (0, i, 0)
    full = lambda i: (0, 0)
    return pl.pallas_call(
        functools.partial(_rwkv_out_kernel, slab=slab),
        grid=(M // tm,),
        in_specs=[pl.BlockSpec((2, tm, D), row3), pl.BlockSpec((tm, D), row), pl.BlockSpec((2, tm, D), row3),
                  pl.BlockSpec((tm, D), row), pl.BlockSpec((tm, D), row),
                  pl.BlockSpec((1, D), full), pl.BlockSpec((1, D), full), pl.BlockSpec((1, D), full),
                  pl.BlockSpec((D, D), full, pipeline_mode=pl.Buffered(1)), pl.BlockSpec((tm, D), row),
                  pl.BlockSpec((None, N_MOD, D), _mod_spec(base, rpg, tm)), pl.BlockSpec((1, D), full)],
        out_specs=pl.BlockSpec((tm, D), row),
        out_shape=jax.ShapeDtypeStruct((M, D), F32),
        scratch_shapes=[pltpu.VMEM((tm, D), BF16)],
        compiler_params=_cparams(("arbitrary",)),
        name="rwkv_out",
    )(y, q["r"], q["kd"], q["v"], q["g"], p["r_k"].reshape(1, D), p["gn_w"].reshape(1, D),
      p["gn_b"].reshape(1, D), w_o, x, mod, g_post.reshape(1, D))


def _adaln_kernel(c_ref, w_ref, b_ref, o_ref):
    cv = c_ref[...]
    cv = cv * jax.nn.sigmoid(cv)
    o_ref[...] = _dot(cv.astype(BF16), w_ref[...].astype(BF16)) + b_ref[...]


def _adaln(cvec, ada_w, ada_b, *, bn=1024):
    depth, D, N = ada_w.shape
    bn = _tile(N, bn, LANES)
    return pl.pallas_call(
        _adaln_kernel,
        grid=(depth, N // bn),
        in_specs=[pl.BlockSpec((8, D), lambda i, n: (0, 0)),
                  pl.BlockSpec((None, D, bn), lambda i, n: (i, 0, n)),
                  pl.BlockSpec((None, 1, bn), lambda i, n: (i, 0, n))],
        out_specs=pl.BlockSpec((None, 8, bn), lambda i, n: (i, 0, n)),
        out_shape=jax.ShapeDtypeStruct((depth, 8, N), F32),
        compiler_params=_cparams(("arbitrary", "arbitrary")),
        name="adaln",
    )(cvec, ada_w, ada_b.reshape(depth, 1, N))


def _ffn_kernel(x_ref, mod_ref, gpre_ref, gpost_ref, wg_ref, wu_ref, wd_ref, o_ref, h_scr, acc_scr, *, slot):
    f = pl.program_id(1)

    @pl.when(f == 0)
    def _():
        _modulated_rows(x_ref, mod_ref, gpre_ref, slot, h_scr)
        acc_scr[...] = jnp.zeros_like(acc_scr)

    h = h_scr[...]
    g = _dot(h, wg_ref[...])
    u = _dot(h, wu_ref[...])
    a = (g * jax.nn.sigmoid(g)) * u
    acc_scr[...] += _dot(a.astype(BF16), wd_ref[...])

    @pl.when(f == pl.num_programs(1) - 1)
    def _():
        gate = mod_ref[3 * slot + 2:3 * slot + 3, :]
        _gated_residual_rows(x_ref, acc_scr, MACARON_WEIGHT * gate, gpost_ref, o_ref)


def _ffn(x, mod, grp, slot, g_pre, g_post, wg, wu, wd, layer, which, *, tm=512, tf=512):
    M, D = x.shape
    F = wg.shape[-1]
    tm = _tile(M, tm, SUBLANES)
    tf = _tile(F, tf, LANES)
    base, rpg = grp
    return pl.pallas_call(
        functools.partial(_ffn_kernel, slot=slot),
        grid=(M // tm, F // tf),
        in_specs=[pl.BlockSpec((tm, D), lambda i, f: (i, 0)),
                  pl.BlockSpec((None, N_MOD, D), _mod_spec(base, rpg, tm)),
                  pl.BlockSpec((1, D), lambda i, f: (0, 0)),
                  pl.BlockSpec((1, D), lambda i, f: (0, 0)),
                  pl.BlockSpec((None, None, D, tf), lambda i, f: (layer, which, 0, f)),
                  pl.BlockSpec((None, None, D, tf), lambda i, f: (layer, which, 0, f)),
                  pl.BlockSpec((None, None, tf, D), lambda i, f: (layer, which, f, 0))],
        out_specs=pl.BlockSpec((tm, D), lambda i, f: (i, 0)),
        out_shape=jax.ShapeDtypeStruct((M, D), F32),
        scratch_shapes=[pltpu.VMEM((tm, D), BF16), pltpu.VMEM((tm, D), F32)],
        compiler_params=_cparams(("arbitrary", "arbitrary")),
        name="ffn",
    )(x, mod, g_pre.reshape(1, D), g_post.reshape(1, D), wg, wu, wd)


def _normmod_mm_kernel(x_ref, mod_ref, gpre_ref, w_ref, o_ref, h_scr, *, slot):
    @pl.when(pl.program_id(1) == 0)
    def _():
        _modulated_rows(x_ref, mod_ref, gpre_ref, slot, h_scr)

    o_ref[...] = _dot(h_scr[...], w_ref[...]).astype(o_ref.dtype)


def _normmod_mm(x, mod, grp, slot, g_pre, w, out_dtype, *, tm=1024, tn=512):
    M, D = x.shape
    N = w.shape[1]
    tm = _tile(min(M, grp[1]), tm, SUBLANES)
    tn = _tile(N, tn, LANES)
    base, rpg = grp
    return pl.pallas_call(
        functools.partial(_normmod_mm_kernel, slot=slot),
        grid=(M // tm, N // tn),
        in_specs=[pl.BlockSpec((tm, D), lambda i, n: (i, 0)),
                  pl.BlockSpec((None, N_MOD, D), _mod_spec(base, rpg, tm)),
                  pl.BlockSpec((1, D), lambda i, n: (0, 0)),
                  pl.BlockSpec((D, tn), lambda i, n: (0, n))],
        out_specs=pl.BlockSpec((tm, tn), lambda i, n: (i, n)),
        out_shape=jax.ShapeDtypeStruct((M, N), out_dtype),
        scratch_shapes=[pltpu.VMEM((tm, D), BF16)],
        compiler_params=_cparams(("arbitrary", "arbitrary")),
        name="normmod_mm",
    )(x, mod, g_pre.reshape(1, D), w)


def _outproj_kernel(a_ref, w_ref, x_ref, mod_ref, gpost_ref, o_ref, *, slot):
    y = _dot(a_ref[...], w_ref[...])
    gate = mod_ref[3 * slot + 2:3 * slot + 3, :]
    o_ref[...] = x_ref[...] + gate * _rms(y, gpost_ref[...])


def _outproj(a, w, x, mod, grp, slot, g_post, *, tm=512):
    M, D = x.shape
    K = a.shape[1]
    tm = _tile(M, tm, SUBLANES)
    base, rpg = grp
    return pl.pallas_call(
        functools.partial(_outproj_kernel, slot=slot),
        grid=(M // tm,),
        in_specs=[pl.BlockSpec((tm, K), lambda i: (i, 0)),
                  pl.BlockSpec((K, D), lambda i: (0, 0), pipeline_mode=pl.Buffered(1)),
                  pl.BlockSpec((tm, D), lambda i: (i, 0)),
                  pl.BlockSpec((None, N_MOD, D), _mod_spec(base, rpg, tm)),
                  pl.BlockSpec((1, D), lambda i: (0, 0))],
        out_specs=pl.BlockSpec((tm, D), lambda i: (i, 0)),
        out_shape=jax.ShapeDtypeStruct((M, D), F32),
        compiler_params=_cparams(("arbitrary",)),
        name="outproj",
    )(a, w, x, mod, g_post.reshape(1, D))


def _nat_kernel(q_ref, k_ref, v_ref, kc_ref, vc_ref, bias_ref, o_ref, *, rows, group):
    scale = HEAD ** -0.5
    n_lat = WIN_H * GRID_W
    lane = lax.broadcasted_iota(jnp.int32, (1, LANES), 1)
    head_masks = (lane < HEAD, lane >= HEAD)
    qcol = lax.broadcasted_iota(jnp.int32, (2 * GRID_W, n_lat), 0) % GRID_W
    kcol = lax.broadcasted_iota(jnp.int32, (2 * GRID_W, n_lat), 1) % GRID_W
    cstart = jnp.clip(qcol - WIN_W // 2, 0, GRID_W - WIN_W)
    col_ok = (kcol >= cstart) & (kcol < cstart + WIN_W)
    kc = kc_ref[...]
    vc = vc_ref[...]
    zero = jnp.zeros((), BF16)

    def step(i, carry):
        chains = []
        for rr in range(group):
            r = i * group + rr
            r0 = jnp.clip(r - WIN_H // 2, 0, rows - WIN_H)
            q = q_ref[pl.ds(pl.multiple_of(r * GRID_W, GRID_W), GRID_W), :] * scale
            q2 = jnp.concatenate([jnp.where(head_masks[0], q, zero), jnp.where(head_masks[1], q, zero)], axis=0)
            kw = k_ref[pl.ds(pl.multiple_of(r0 * GRID_W, GRID_W), n_lat), :]
            vw = v_ref[pl.ds(pl.multiple_of(r0 * GRID_W, GRID_W), n_lat), :]
            chains.append((r, r0, q2, kw, vw))
        s = [_dot_nt(q2, kw) for (_, _, q2, kw, _) in chains]
        sc = [_dot_nt(q2, kc) for (_, _, q2, _, _) in chains]
        s = [jnp.where(col_ok, si + jnp.concatenate(
                [jnp.concatenate([bias_ref[h, 2 * jj - (r - r0) + WIN_H - 1] for jj in range(WIN_H // 2)], axis=1)
                 for h in range(2)], axis=0), MASK_VALUE)
             for si, (r, r0, _, _, _) in zip(s, chains)]
        m = [jnp.maximum(jnp.max(si, axis=-1, keepdims=True), jnp.max(ci, axis=-1, keepdims=True))
             for si, ci in zip(s, sc)]
        e = [jnp.exp(si - mi) for si, mi in zip(s, m)]
        ec = [jnp.exp(ci - mi) for ci, mi in zip(sc, m)]
        den = [jnp.sum(ei, axis=-1, keepdims=True) + jnp.sum(ci, axis=-1, keepdims=True) for ei, ci in zip(e, ec)]
        o = [(_dot(ei.astype(BF16), vw) + _dot(ci.astype(BF16), vc)) / di
             for ei, ci, di, (_, _, _, _, vw) in zip(e, ec, den, chains)]
        for rr in range(group):
            r = i * group + rr
            o_ref[pl.ds(pl.multiple_of(r * GRID_W, GRID_W), GRID_W), :] = jnp.where(
                head_masks[0], o[rr][:GRID_W], o[rr][GRID_W:]).astype(o_ref.dtype)
        return carry

    lax.fori_loop(0, rows // group, step, 0)


def _nat_bias_table(rpb):
    edge = GRID_W - WIN_W
    padded = jnp.pad(rpb, ((0, 0), (0, 0), (edge, edge)), mode="edge")
    rows = jnp.stack([padded[:, :, GRID_W - 1 - q:2 * GRID_W - 1 - q] for q in range(GRID_W)], axis=2)
    return jnp.concatenate([rows[:, :-1], rows[:, 1:]], axis=-1).astype(F32)


def _nat_attention(qkv, qkv_c, bias, *, group=4):
    B, T, D3 = qkv.shape
    D = D3 // 3
    C = qkv_c.shape[1]
    nd = D // LANES
    rows = T // GRID_W
    assert rows >= WIN_H and rows % group == 0
    return pl.pallas_call(
        functools.partial(_nat_kernel, rows=rows, group=group),
        grid=(B, nd),
        in_specs=[pl.BlockSpec((None, T, LANES), lambda b, p: (b, 0, p)),
                  pl.BlockSpec((None, T, LANES), lambda b, p: (b, 0, nd + p)),
                  pl.BlockSpec((None, T, LANES), lambda b, p: (b, 0, 2 * nd + p)),
                  pl.BlockSpec((None, C, LANES), lambda b, p: (b, 0, nd + p)),
                  pl.BlockSpec((None, C, LANES), lambda b, p: (b, 0, 2 * nd + p)),
                  pl.BlockSpec((2, 2 * WIN_H - 2, GRID_W, 2 * GRID_W), lambda b, p: (p, 0, 0, 0))],
        out_specs=pl.BlockSpec((None, T, LANES), lambda b, p: (b, 0, p)),
        out_shape=jax.ShapeDtypeStruct((B, T, D), BF16),
        compiler_params=_cparams(("arbitrary", "arbitrary")),
        name="nat_attention",
    )(qkv, qkv, qkv, qkv_c, qkv_c, bias)


def _ctx_attn_kernel(q_ref, k_ref, v_ref, o_ref):
    scale = HEAD ** -0.5
    lane = lax.broadcasted_iota(jnp.int32, (1, LANES), 1)
    head_masks = (lane < HEAD, lane >= HEAD)
    q = q_ref[...] * scale
    k = k_ref[...]
    v = v_ref[...]
    zero = jnp.zeros((), BF16)
    outs = []
    for h in range(2):
        s = _dot_nt(jnp.where(head_masks[h], q, zero), k)
        e = jnp.exp(s - jnp.max(s, axis=-1, keepdims=True))
        outs.append(_dot(e.astype(BF16), v) / jnp.sum(e, axis=-1, keepdims=True))
    o_ref[...] = jnp.where(head_masks[0], outs[0], outs[1]).astype(o_ref.dtype)


def _ctx_attention(qkv_c):
    B, C, D3 = qkv_c.shape
    D = D3 // 3
    nd = D // LANES
    return pl.pallas_call(
        _ctx_attn_kernel,
        grid=(B, nd),
        in_specs=[pl.BlockSpec((None, C, LANES), lambda b, p: (b, 0, p)),
                  pl.BlockSpec((None, C, LANES), lambda b, p: (b, 0, nd + p)),
                  pl.BlockSpec((None, C, LANES), lambda b, p: (b, 0, 2 * nd + p))],
        out_specs=pl.BlockSpec((None, C, LANES), lambda b, p: (b, 0, p)),
        out_shape=jax.ShapeDtypeStruct((B, C, D), BF16),
        compiler_params=_cparams(("arbitrary", "arbitrary")),
        name="ctx_attention",
    )(qkv_c, qkv_c, qkv_c)


def _pad_to(w, axis, mult=LANES):
    n = w.shape[axis]
    pad = (-n) % mult
    if pad == 0:
        return w
    widths = [(0, 0)] * w.ndim
    widths[axis] = (0, pad)
    return jnp.pad(w, widths)


def _lora_in(w):
    w = _pad_to(w, 2)
    return jnp.concatenate(list(w), axis=1).astype(BF16)


def _lora_out(w):
    return _pad_to(w, 1).astype(BF16)


def _scan_inputs(q, B, L, D):
    three = lambda t: t.reshape(B, L, D)
    four = lambda t: t.reshape(2, B, L, D)
    return three(q["r"]), three(q["v"]), three(q["kk"]), four(q["lw"]), four(q["kd"]), four(q["a"])


def kernel(x, c, ctx, c_ctx, ada_w, ada_b, norm_pre, norm_post, ffn_w_gate, ffn_w_up, ffn_w_down, rwkv_mix, rwkv_w_r, rwkv_w_k, rwkv_w_v, rwkv_w_o, rwkv_w0, rwkv_w1, rwkv_w2, rwkv_a0, rwkv_a1, rwkv_a2, rwkv_v0, rwkv_v1, rwkv_v2, rwkv_k_k, rwkv_k_a, rwkv_r_k, rwkv_g1, rwkv_g2, rwkv_gn_w, rwkv_gn_b, nat_w_qkv, nat_w_o, nat_rpb):
    B, T, D = x.shape
    C = ctx.shape[1]
    depth = ada_w.shape[0]
    assert B + 1 <= 8 and D % (2 * LANES) == 0
    cvec = jnp.zeros((8, D), F32).at[:B].set(c).at[B].set(c_ctx)
    mods = _adaln(cvec, ada_w, ada_b).reshape(depth, 8, N_MOD, D)
    wg, wu, wd = (w.astype(BF16) for w in (ffn_w_gate, ffn_w_up, ffn_w_down))
    grp_l, grp_c = (0, T), (B, B * C)
    xl = x.reshape(B * T, D)
    xc = ctx.reshape(B * C, D)
    vf_l = vf_c = None
    for i in range(depth):
        last = i == depth - 1
        j = i // 2
        mod = mods[i]
        ffn = functools.partial(_ffn, mod=mod, wg=wg, wu=wu, wd=wd, layer=i)
        xl = ffn(xl, grp=grp_l, slot=0, g_pre=norm_pre[i, 0], g_post=norm_post[i, 0], which=0)
        xc = ffn(xc, grp=grp_c, slot=0, g_pre=norm_pre[i, 0], g_post=norm_post[i, 0], which=0)
        if i % 2 == 0:
            p = dict(
                mix=rwkv_mix[j], w_r=rwkv_w_r[j].astype(BF16), w_k=rwkv_w_k[j].astype(BF16),
                w_v=rwkv_w_v[j].astype(BF16), w0=rwkv_w0[j], w1=_lora_in(rwkv_w1[j]), w2=_lora_out(rwkv_w2[j]),
                a0=rwkv_a0[j], a1=_lora_in(rwkv_a1[j]), a2=_lora_out(rwkv_a2[j]),
                k_k=rwkv_k_k[j], k_a=rwkv_k_a[j], r_k=rwkv_r_k[j].reshape(D),
                g1=_lora_in(rwkv_g1[j][None]), g2=_lora_out(rwkv_g2[j][None]),
                gn_w=rwkv_gn_w[j], gn_b=rwkv_gn_b[j],
                v_res=None if j == 0 else (rwkv_v0[j - 1], _lora_in(rwkv_v1[j - 1][None]), _lora_out(rwkv_v2[j - 1][None])))
            q_c = _rwkv_prep(xc, mod, grp_c, norm_pre[i, 1], p, vf_c, C)
            q_l = _rwkv_prep(xl, mod, grp_l, norm_pre[i, 1], p, vf_l, T)
            if j == 0:
                vf_l, vf_c = q_l["v"], q_c["v"]
            s0 = jnp.zeros((2, B, D // LANES, LANES, LANES), F32)
            y_c, s_c = _wkv_scan(*_scan_inputs(q_c, B, C, D), s0)
            y_l, _ = _wkv_scan(*_scan_inputs(q_l, B, T, D), s_c)
            w_o = rwkv_w_o[j].astype(BF16)
            xl = _rwkv_out(y_l.reshape(2, B * T, D), q_l, p, w_o, xl, mod, grp_l, norm_post[i, 1])
            if not last:
                xc = _rwkv_out(y_c.reshape(2, B * C, D), q_c, p, w_o, xc, mod, grp_c, norm_post[i, 1])
        else:
            w_qkv = nat_w_qkv[j].astype(BF16)
            w_o = nat_w_o[j].astype(BF16)
            qkv_l = _normmod_mm(xl, mod, grp_l, 1, norm_pre[i, 1], w_qkv, BF16).reshape(B, T, 3 * D)
            qkv_c = _normmod_mm(xc, mod, grp_c, 1, norm_pre[i, 1], w_qkv, BF16).reshape(B, C, 3 * D)
            a_l = _nat_attention(qkv_l, qkv_c, _nat_bias_table(nat_rpb[j])).reshape(B * T, D)
            xl = _outproj(a_l, w_o, xl, mod, grp_l, 1, norm_post[i, 1])
            if not last:
                a_c = _ctx_attention(qkv_c).reshape(B * C, D)
                xc = _outproj(a_c, w_o, xc, mod, grp_c, 1, norm_post[i, 1])
        xl = ffn(xl, grp=grp_l, slot=2, g_pre=norm_pre[i, 2], g_post=norm_post[i, 2], which=1)
        if not last:
            xc = ffn(xc, grp=grp_c, slot=2, g_pre=norm_pre[i, 2], g_post=norm_post[i, 2], which=1)
    return xl.reshape(B, T, D)
```

```python
import functools

import jax
import jax.numpy as jnp
from jax import lax
from jax.experimental import pallas as pl
from jax.experimental.pallas import tpu as pltpu

F32 = jnp.float32
BF16 = jnp.bfloat16

LANES = 128
SUBLANES = 8
ROW_SLAB = 32
HEAD = 64
N_MOD = 9
MACARON_WEIGHT = 0.5
RMS_EPS = 1e-6
RWKV_GN_EPS = 64e-5
L2_EPS_SQ = 1e-24
GRID_W = 64
WIN_H = 8
WIN_W = 16
MASK_VALUE = -1e30
VMEM_LIMIT = 56 * 1024 * 1024


def _cparams(sem):
    return pltpu.CompilerParams(dimension_semantics=sem, vmem_limit_bytes=VMEM_LIMIT)


def _dot(a, b):
    return jnp.dot(a, b, preferred_element_type=F32)


def _dot_nt(a, b):
    return lax.dot_general(a, b, (((1,), (1,)), ((), ())), preferred_element_type=F32)


def _dot_tn(a, b):
    return lax.dot_general(a, b, (((0,), (0,)), ((), ())), preferred_element_type=F32)


def _split3(x):
    hi = x.astype(BF16)
    r1 = x - hi.astype(F32)
    mid = r1.astype(BF16)
    lo = (r1 - mid.astype(F32)).astype(BF16)
    return hi, mid, lo


def _dot_exact_lhs01(m01, x):
    hi, mid, lo = _split3(x)
    return _dot(m01, hi) + _dot(m01, mid) + _dot(m01, lo)


def _dot_exact_rhs01(x, m01):
    n = x.shape[0]
    parts = _dot(jnp.concatenate(_split3(x), axis=0), m01)
    return parts[:n] + parts[n:2 * n] + parts[2 * n:]


def _head_ones(n):
    r = lax.broadcasted_iota(jnp.int32, (n, n), 0) // HEAD
    c = lax.broadcasted_iota(jnp.int32, (n, n), 1) // HEAD
    return (r == c).astype(BF16)


def _softplus(x):
    return jnp.maximum(x, 0.0) + jnp.log1p(jnp.exp(-jnp.abs(x)))


def _rms(x, g):
    return x * lax.rsqrt(jnp.mean(x * x, axis=-1, keepdims=True) + RMS_EPS) * g


def _modulated(x, mod_ref, gpre_ref, slot):
    shift = mod_ref[3 * slot:3 * slot + 1, :]
    scale = mod_ref[3 * slot + 1:3 * slot + 2, :]
    return _rms(x, gpre_ref[...]) * (1 + scale) + shift


def _tile(n, want, unit):
    if n <= want:
        return n
    t = (want // unit) * unit
    while n % t:
        t -= unit
    return t


def _row_slabs(rows, fn):
    slab = _tile(rows, ROW_SLAB, 2 * SUBLANES)

    def body(i, carry):
        fn(pl.ds(pl.multiple_of(i * slab, slab), slab))
        return carry

    lax.fori_loop(0, rows // slab, body, 0, unroll=min(8, rows // slab))


def _modulated_rows(x_ref, mod_ref, gpre_ref, slot, dst_ref):
    shift = mod_ref[3 * slot:3 * slot + 1, :]
    scale1 = 1 + mod_ref[3 * slot + 1:3 * slot + 2, :]
    g = gpre_ref[...]

    def slab(sl):
        dst_ref[sl, :] = (_rms(x_ref[sl, :], g) * scale1 + shift).astype(dst_ref.dtype)

    _row_slabs(x_ref.shape[0], slab)


def _gated_residual_rows(x_ref, y_ref, wgate, gpost_ref, o_ref):
    g = gpost_ref[...]

    def slab(sl):
        o_ref[sl, :] = x_ref[sl, :] + wgate * _rms(y_ref[sl, :], g)

    _row_slabs(x_ref.shape[0], slab)


def _mod_spec(base, rows_per_group, tm):
    assert rows_per_group % tm == 0
    return lambda i, *_: (base + (i * tm) // rows_per_group, 0, 0)


def _wkv_kernel(r_ref, v_ref, kk_ref, lw_ref, kd_ref, a_ref, s0_ref, y_ref, sfin_ref, s_scr,
                *, chunk, pairs, nb):
    z = pl.program_id(0)
    c = pl.program_id(3)
    C = chunk
    side_by_side = lambda ref: jnp.concatenate([ref[b] for b in range(nb)], axis=1)

    @pl.when(c == 0)
    def _():
        for b in range(nb):
            s_scr[b * pairs:(b + 1) * pairs] = s0_ref[b]

    row = lax.broadcasted_iota(jnp.int32, (C, 2 * C), 0)
    pcol = lax.broadcasted_iota(jnp.int32, (C, 2 * C), 1)
    col = pcol % C
    d = (row - col) * (1 - 2 * z)
    strict = d > 0
    incl = d >= 0
    half_masks = (pcol < C, pcol >= C)
    lane = lax.broadcasted_iota(jnp.int32, (1, LANES), 1)
    head_masks = (lane < HEAD, lane >= HEAD)
    srow = lax.broadcasted_iota(jnp.int32, (LANES, LANES), 0)
    scol = lax.broadcasted_iota(jnp.int32, (LANES, LANES), 1)
    blockdiag = (srow < HEAD) == (scol < HEAD)
    eye = (row == col).astype(BF16)
    levels = []
    s = 1
    while s < C:
        levels.append((row // (2 * s) == col // (2 * s)) & (row // s != col // s))
        s *= 2

    def blockdiag2(m):
        zero = jnp.zeros_like(m)
        return jnp.concatenate([jnp.where(half_masks[0], m, zero), jnp.where(half_masks[1], m, zero)], axis=0)

    def by_head(x):
        zero = jnp.zeros_like(x)
        return jnp.concatenate([jnp.where(head_masks[0], x, zero), jnp.where(head_masks[1], x, zero)], axis=0)

    lw = side_by_side(lw_ref)
    kk = side_by_side(kk_ref)
    kd = side_by_side(kd_ref)
    L = _dot_exact_lhs01(incl[:, :C].astype(BF16), lw)
    l_end = jnp.sum(lw, axis=0, keepdims=True)
    lm = 0.5 * l_end
    e_m = jnp.exp(-lm)
    e_p = jnp.exp(lm)
    e_sh = jnp.exp(lm - L)
    at_t = -kk * jnp.exp(L - lw)
    rt_t = side_by_side(r_ref) * jnp.exp(L)
    sr_all = jnp.concatenate([at_t, rt_t], axis=0).astype(BF16)
    at_sh = at_t * e_m
    rt_sh = rt_t * e_m
    bt = (kk * side_by_side(a_ref)) * e_sh
    kt = kd * e_sh
    bk_all = jnp.concatenate([bt, kt], axis=0).astype(BF16)
    bkh_all = jnp.concatenate([bt * e_p, kt * e_p], axis=0).astype(BF16)
    v_all = side_by_side(v_ref).astype(BF16)
    s_decay = jnp.exp(l_end)

    prs = range(nb * pairs)
    lanes = [slice(p * LANES, (p + 1) * LANES) for p in prs]
    sr = [_dot_nt(sr_all[:, lanes[p]], s_scr[p].astype(BF16)) for p in prs]
    v_h = [by_head(v_all[:, lanes[p]]) for p in prs]
    quads = [_dot_nt(jnp.concatenate([at_sh[:, lanes[p]], rt_sh[:, lanes[p]]], axis=0).astype(BF16),
                     jnp.concatenate([by_head(bk_all[:C, lanes[p]]), by_head(bk_all[C:, lanes[p]])], axis=0))
             for p in prs]
    a_ab = [jnp.where(strict, q[:C, :2 * C], 0.0).astype(BF16) for q in quads]
    a_ak = [jnp.where(strict, q[:C, 2 * C:], 0.0).astype(BF16) for q in quads]
    r_bk = [jnp.concatenate([jnp.where(incl, q[C:, :2 * C], 0.0), jnp.where(incl, q[C:, 2 * C:], 0.0)],
                            axis=1).astype(BF16) for q in quads]
    t = [eye + jnp.where(levels[0], a, jnp.zeros_like(a)) for a in a_ab]
    for lv in levels[1:]:
        x = [jnp.where(lv, a, jnp.zeros_like(a)) for a in a_ab]
        m1 = [_dot(xi, blockdiag2(ti)).astype(BF16) for xi, ti in zip(x, t)]
        m2 = [_dot(ti, blockdiag2(mi)).astype(BF16) for ti, mi in zip(t, m1)]
        t = [ti + mi for ti, mi in zip(t, m2)]
    rhs = [(sr[p][:C] + _dot(a_ak[p], v_h[p])).astype(BF16) for p in prs]
    u = [_dot(t[p], by_head(rhs[p])).astype(BF16) for p in prs]
    y = [sr[p][C:] + _dot(r_bk[p], jnp.concatenate([by_head(u[p]), v_h[p]], axis=0)) for p in prs]
    for p in prs:
        y_ref[p // pairs, :, lanes[p % pairs]] = y[p]
        upd = _dot_tn(jnp.concatenate([u[p], v_all[:, lanes[p]]], axis=0), bkh_all[:, lanes[p]])
        s_scr[p] = s_scr[p] * s_decay[:, lanes[p]] + jnp.where(blockdiag, upd, 0.0)

    @pl.when(c == pl.num_programs(3) - 1)
    def _():
        for b in range(nb):
            sfin_ref[b] = s_scr[b * pairs:(b + 1) * pairs]


def _wkv_scan(r, v, kk, lw, kd, a, s0, *, chunk=64, pairs=16, nb=2):
    B, T, D = r.shape
    pairs = min(pairs, D // LANES)
    nb = nb if B % nb == 0 else 1
    lw_lanes = pairs * LANES
    assert T % chunk == 0 and D % lw_lanes == 0
    nc = T // chunk
    ng = D // lw_lanes

    def cidx(z, c):
        return c + z * (nc - 1 - 2 * c)

    tok_spec = pl.BlockSpec((nb, chunk, lw_lanes), lambda z, b, g, c: (b, cidx(z, c), g))
    dir_spec = pl.BlockSpec((None, nb, chunk, lw_lanes), lambda z, b, g, c: (z, b, cidx(z, c), g))
    st_spec = pl.BlockSpec((None, nb, pairs, LANES, LANES), lambda z, b, g, c: (z, b, g, 0, 0))
    y, s_fin = pl.pallas_call(
        functools.partial(_wkv_kernel, chunk=chunk, pairs=pairs, nb=nb),
        grid=(2, B // nb, ng, nc),
        in_specs=[tok_spec, tok_spec, tok_spec, dir_spec, dir_spec, dir_spec, st_spec],
        out_specs=[dir_spec, st_spec],
        out_shape=[jax.ShapeDtypeStruct((2, B, T, D), F32),
                   jax.ShapeDtypeStruct(s0.shape, F32)],
        scratch_shapes=[pltpu.VMEM((nb * pairs, LANES, LANES), F32)],
        compiler_params=_cparams(("arbitrary", "arbitrary", "arbitrary", "arbitrary")),
        name="wkv_scan",
    )(r, v, kk, lw, kd, a, s0)
    return y, s_fin


def _rwkv_prep_kernel(*refs, seq_len, tm, has_vres):
    (x_ref, xp_ref, xn_ref, mod_ref, gpre_ref, mix_ref, wr_ref, wk_ref, wv_ref,
     w1_ref, w2_ref, w0_ref, a1_ref, a2_ref, a0_ref, g1_ref, g2_ref, kk_ref, ka_ref) = refs[:19]
    pos = 19
    if has_vres:
        v1_ref, v2_ref, v0_ref, vf_ref = refs[pos:pos + 4]
        pos += 4
    r_ref, v_ref, kkn_ref, g_ref, lw_ref, kd_ref, a_ref = refs[pos:pos + 7]
    pos += 7
    xmix_scr, tw_scr, ta_scr, tg_scr = refs[pos:pos + 4]
    tv_scr = refs[pos + 4] if has_vres else None
    i = pl.program_id(0)
    n = pl.program_id(1)

    @pl.when(n == 0)
    def _():
        h = _modulated(x_ref[...], mod_ref, gpre_ref, 1)
        t0 = i * tm
        hp = _modulated(xp_ref[...], mod_ref, gpre_ref, 1)[SUBLANES - 1:SUBLANES, :]
        hn = _modulated(xn_ref[...], mod_ref, gpre_ref, 1)[0:1, :]
        hp = jnp.where(t0 % seq_len == 0, 0.0, hp)
        hn = jnp.where((t0 + tm) % seq_len == 0, 0.0, hn)
        rowid = lax.broadcasted_iota(jnp.int32, (tm, 1), 0)
        prev = jnp.where(rowid == 0, hp, pltpu.roll(h, 1, 0))
        nxt = jnp.where(rowid == tm - 1, hn, pltpu.roll(h, tm - 1, 0))
        xx = 0.5 * (prev + nxt) - h
        mixed = lambda m: (h + xx * mix_ref[m:m + 1, :]).astype(BF16)
        xmix_scr[0] = mixed(0)
        xmix_scr[1] = mixed(2)
        xv = mixed(3)
        xmix_scr[2] = xv
        tw_scr[...] = jnp.tanh(_dot(mixed(1), w1_ref[...])).astype(BF16)
        ta_scr[...] = _dot(mixed(4), a1_ref[...]).astype(BF16)
        tg_scr[...] = jax.nn.sigmoid(_dot(mixed(5), g1_ref[...])).astype(BF16)
        if has_vres:
            tv_scr[...] = _dot(xv, v1_ref[...]).astype(BF16)

    r_ref[...] = _dot(xmix_scr[0], wr_ref[...])
    k = _dot(xmix_scr[1], wk_ref[...])
    v = _dot(xmix_scr[2], wv_ref[...])
    if has_vres:
        v = v + (vf_ref[...] - v) * jax.nn.sigmoid(v0_ref[...] + _dot(tv_scr[...], v2_ref[...]))
    v_ref[...] = v
    g_ref[...] = _dot(tg_scr[...], g2_ref[...])
    kk = k * kk_ref[...]
    ssq = _dot_exact_rhs01(kk * kk, _head_ones(kk.shape[1]))
    kkn_ref[...] = kk * lax.rsqrt(jnp.maximum(ssq, L2_EPS_SQ))
    lp = w2_ref.shape[1]
    la = a2_ref.shape[1]
    for zz in range(2):
        lora_w = _dot(tw_scr[:, zz * lp:(zz + 1) * lp], w2_ref[zz])
        w_log = -_softplus(-(w0_ref[zz:zz + 1, :] + lora_w)) - 0.5
        lw_ref[zz] = -jnp.exp(w_log)
        a = jax.nn.sigmoid(a0_ref[zz:zz + 1, :] + _dot(ta_scr[:, zz * la:(zz + 1) * la], a2_ref[zz]))
        a_ref[zz] = a
        kd_ref[zz] = k * (1 + (a - 1) * ka_ref[...])


def _rwkv_prep(x, mod, grp, g_pre, p, v_first, seq_len, *, tm=512, tn=256):
    M, D = x.shape
    tm = _tile(seq_len, tm, SUBLANES)
    tn = _tile(D, tn, LANES)
    base, rpg = grp
    wtile = pl.BlockSpec((D, tn), lambda i, n: (0, n))
    has_vres = p["v_res"] is not None
    nb = M // SUBLANES
    row = lambda i, n: (i, 0)
    col = lambda i, n: (0, n)
    full = lambda i, n: (0, 0)
    col3 = lambda i, n: (0, 0, n)
    tile = lambda i, n: (i, n)
    tile3 = lambda i, n: (0, i, n)
    lw1, la1, lg1 = p["w1"].shape[1], p["a1"].shape[1], p["g1"].shape[1]
    in_specs = [
        pl.BlockSpec((tm, D), row),
        pl.BlockSpec((SUBLANES, D), lambda i, n: (jnp.maximum(i * (tm // SUBLANES) - 1, 0), 0)),
        pl.BlockSpec((SUBLANES, D), lambda i, n: (jnp.minimum((i + 1) * (tm // SUBLANES), nb - 1), 0)),
        pl.BlockSpec((None, N_MOD, D), _mod_spec(base, rpg, tm)),
        pl.BlockSpec((1, D), full),
        pl.BlockSpec((6, D), full),
        wtile, wtile, wtile,
        pl.BlockSpec((D, lw1), full), pl.BlockSpec((2, lw1 // 2, tn), col3), pl.BlockSpec((2, tn), col),
        pl.BlockSpec((D, la1), full), pl.BlockSpec((2, la1 // 2, tn), col3), pl.BlockSpec((2, tn), col),
        pl.BlockSpec((D, lg1), full), pl.BlockSpec((lg1, tn), col),
        pl.BlockSpec((1, tn), col), pl.BlockSpec((1, tn), col),
    ]
    args = [x, x, x, mod, g_pre.reshape(1, D), p["mix"], p["w_r"], p["w_k"], p["w_v"],
            p["w1"], p["w2"], p["w0"], p["a1"], p["a2"], p["a0"], p["g1"], p["g2"][0],
            p["k_k"].reshape(1, D), p["k_a"].reshape(1, D)]
    scratch = [pltpu.VMEM((3, tm, D), BF16), pltpu.VMEM((tm, lw1), BF16),
               pltpu.VMEM((tm, la1), BF16), pltpu.VMEM((tm, lg1), BF16)]
    if has_vres:
        v0, v1, v2 = p["v_res"]
        lv1 = v1.shape[1]
        in_specs += [pl.BlockSpec((D, lv1), full), pl.BlockSpec((lv1, tn), col),
                     pl.BlockSpec((1, tn), col), pl.BlockSpec((tm, tn), tile)]
        args += [v1, v2[0], v0.reshape(1, D), v_first]
        scratch.append(pltpu.VMEM((tm, lv1), BF16))
    one = jax.ShapeDtypeStruct((M, D), F32)
    two = jax.ShapeDtypeStruct((2, M, D), F32)
    r, v, kk, g, lw, kd, a = pl.pallas_call(
        functools.partial(_rwkv_prep_kernel, seq_len=seq_len, tm=tm, has_vres=has_vres),
        grid=(M // tm, D // tn),
        in_specs=in_specs,
        out_specs=[pl.BlockSpec((tm, tn), tile)] * 4 + [pl.BlockSpec((2, tm, tn), tile3)] * 3,
        out_shape=[one] * 4 + [two] * 3,
        scratch_shapes=scratch,
        compiler_params=_cparams(("arbitrary", "arbitrary")),
        name="rwkv_prep",
    )(*args)
    return dict(r=r, v=v, kk=kk, g=g, lw=lw, kd=kd, a=a)


def _rwkv_out_kernel(y_ref, r_ref, kd_ref, v_ref, g_ref, rk_ref, gnw_ref, gnb_ref, w_ref, x_ref, mod_ref,
                     gpost_ref, o_ref, pre_scr, *, slab):
    D = x_ref.shape[1]
    ones = _head_ones(slab)
    for j in range(D // slab):
        sl = slice(j * slab, (j + 1) * slab)
        wkv = y_ref[0, :, sl] + y_ref[1, :, sl]
        mu = _dot_exact_rhs01(wkv, ones) * (1.0 / HEAD)
        cen = wkv - mu
        var = _dot_exact_rhs01(cen * cen, ones) * (1.0 / HEAD)
        o = cen * lax.rsqrt(var + RWKV_GN_EPS) * gnw_ref[:, sl] + gnb_ref[:, sl]
        coef = _dot_exact_rhs01(r_ref[:, sl] * (kd_ref[0, :, sl] + kd_ref[1, :, sl]) * rk_ref[:, sl], ones)
        pre_scr[:, sl] = ((o + coef * v_ref[:, sl]) * g_ref[:, sl]).astype(BF16)
    y = _dot(pre_scr[...], w_ref[...])
    o_ref[...] = x_ref[...] + mod_ref[5:6, :] * _rms(y, gpost_ref[...])


def _rwkv_out(y, q, p, w_o, x, mod, grp, g_post, *, tm=256, slab=256):
    M, D = x.shape
    tm = _tile(M, tm, SUBLANES)
    slab = _tile(D, slab, LANES)
    base, rpg = grp
    row = lambda i: (i, 0)
    row3 = lambda i: (0, i, 0)
    full = lambda i: (0, 0)
    return pl.pallas_call(
        functools.partial(_rwkv_out_kernel, slab=slab),
        grid=(M // tm,),
        in_specs=[pl.BlockSpec((2, tm, D), row3), pl.BlockSpec((tm, D), row), pl.BlockSpec((2, tm, D), row3),
                  pl.BlockSpec((tm, D), row), pl.BlockSpec((tm, D), row),
                  pl.BlockSpec((1, D), full), pl.BlockSpec((1, D), full), pl.BlockSpec((1, D), full),
                  pl.BlockSpec((D, D), full, pipeline_mode=pl.Buffered(1)), pl.BlockSpec((tm, D), row),
                  pl.BlockSpec((None, N_MOD, D), _mod_spec(base, rpg, tm)), pl.BlockSpec((1, D), full)],
        out_specs=pl.BlockSpec((tm, D), row),
        out_shape=jax.ShapeDtypeStruct((M, D), F32),
        scratch_shapes=[pltpu.VMEM((tm, D), BF16)],
        compiler_params=_cparams(("arbitrary",)),
        name="rwkv_out",
    )(y, q["r"], q["kd"], q["v"], q["g"], p["r_k"].reshape(1, D), p["gn_w"].reshape(1, D),
      p["gn_b"].reshape(1, D), w_o, x, mod, g_post.reshape(1, D))


def _adaln_kernel(c_ref, w_ref, b_ref, o_ref):
    cv = c_ref[...]
    cv = cv * jax.nn.sigmoid(cv)
    o_ref[...] = _dot(cv.astype(BF16), w_ref[...].astype(BF16)) + b_ref[...]


def _adaln(cvec, ada_w, ada_b, *, bn=1024):
    depth, D, N = ada_w.shape
    bn = _tile(N, bn, LANES)
    return pl.pallas_call(
        _adaln_kernel,
        grid=(depth, N // bn),
        in_specs=[pl.BlockSpec((8, D), lambda i, n: (0, 0)),
                  pl.BlockSpec((None, D, bn), lambda i, n: (i, 0, n)),
                  pl.BlockSpec((None, 1, bn), lambda i, n: (i, 0, n))],
        out_specs=pl.BlockSpec((None, 8, bn), lambda i, n: (i, 0, n)),
        out_shape=jax.ShapeDtypeStruct((depth, 8, N), F32),
        compiler_params=_cparams(("arbitrary", "arbitrary")),
        name="adaln",
    )(cvec, ada_w, ada_b.reshape(depth, 1, N))


def _ffn_kernel(x_ref, mod_ref, gpre_ref, gpost_ref, wg_ref, wu_ref, wd_ref, o_ref, h_scr, acc_scr, *, slot):
    f = pl.program_id(1)

    @pl.when(f == 0)
    def _():
        _modulated_rows(x_ref, mod_ref, gpre_ref, slot, h_scr)
        acc_scr[...] = jnp.zeros_like(acc_scr)

    h = h_scr[...]
    g = _dot(h, wg_ref[...])
    u = _dot(h, wu_ref[...])
    a = (g * jax.nn.sigmoid(g)) * u
    acc_scr[...] += _dot(a.astype(BF16), wd_ref[...])

    @pl.when(f == pl.num_programs(1) - 1)
    def _():
        gate = mod_ref[3 * slot + 2:3 * slot + 3, :]
        _gated_residual_rows(x_ref, acc_scr, MACARON_WEIGHT * gate, gpost_ref, o_ref)


def _ffn(x, mod, grp, slot, g_pre, g_post, wg, wu, wd, layer, which, *, tm=512, tf=512):
    M, D = x.shape
    F = wg.shape[-1]
    tm = _tile(M, tm, SUBLANES)
    tf = _tile(F, tf, LANES)
    base, rpg = grp
    return pl.pallas_call(
        functools.partial(_ffn_kernel, slot=slot),
        grid=(M // tm, F // tf),
        in_specs=[pl.BlockSpec((tm, D), lambda i, f: (i, 0)),
                  pl.BlockSpec((None, N_MOD, D), _mod_spec(base, rpg, tm)),
                  pl.BlockSpec((1, D), lambda i, f: (0, 0)),
                  pl.BlockSpec((1, D), lambda i, f: (0, 0)),
                  pl.BlockSpec((None, None, D, tf), lambda i, f: (layer, which, 0, f)),
                  pl.BlockSpec((None, None, D, tf), lambda i, f: (layer, which, 0, f)),
                  pl.BlockSpec((None, None, tf, D), lambda i, f: (layer, which, f, 0))],
        out_specs=pl.BlockSpec((tm, D), lambda i, f: (i, 0)),
        out_shape=jax.ShapeDtypeStruct((M, D), F32),
        scratch_shapes=[pltpu.VMEM((tm, D), BF16), pltpu.VMEM((tm, D), F32)],
        compiler_params=_cparams(("arbitrary", "arbitrary")),
        name="ffn",
    )(x, mod, g_pre.reshape(1, D), g_post.reshape(1, D), wg, wu, wd)


def _normmod_mm_kernel(x_ref, mod_ref, gpre_ref, w_ref, o_ref, h_scr, *, slot):
    @pl.when(pl.program_id(1) == 0)
    def _():
        _modulated_rows(x_ref, mod_ref, gpre_ref, slot, h_scr)

    o_ref[...] = _dot(h_scr[...], w_ref[...]).astype(o_ref.dtype)


def _normmod_mm(x, mod, grp, slot, g_pre, w, out_dtype, *, tm=1024, tn=512):
    M, D = x.shape
    N = w.shape[1]
    tm = _tile(min(M, grp[1]), tm, SUBLANES)
    tn = _tile(N, tn, LANES)
    base, rpg = grp
    return pl.pallas_call(
        functools.partial(_normmod_mm_kernel, slot=slot),
        grid=(M // tm, N // tn),
        in_specs=[pl.BlockSpec((tm, D), lambda i, n: (i, 0)),
                  pl.BlockSpec((None, N_MOD, D), _mod_spec(base, rpg, tm)),
                  pl.BlockSpec((1, D), lambda i, n: (0, 0)),
                  pl.BlockSpec((D, tn), lambda i, n: (0, n))],
        out_specs=pl.BlockSpec((tm, tn), lambda i, n: (i, n)),
        out_shape=jax.ShapeDtypeStruct((M, N), out_dtype),
        scratch_shapes=[pltpu.VMEM((tm, D), BF16)],
        compiler_params=_cparams(("arbitrary", "arbitrary")),
        name="normmod_mm",
    )(x, mod, g_pre.reshape(1, D), w)


def _outproj_kernel(a_ref, w_ref, x_ref, mod_ref, gpost_ref, o_ref, *, slot):
    y = _dot(a_ref[...], w_ref[...])
    gate = mod_ref[3 * slot + 2:3 * slot + 3, :]
    o_ref[...] = x_ref[...] + gate * _rms(y, gpost_ref[...])


def _outproj(a, w, x, mod, grp, slot, g_post, *, tm=512):
    M, D = x.shape
    K = a.shape[1]
    tm = _tile(M, tm, SUBLANES)
    base, rpg = grp
    return pl.pallas_call(
        functools.partial(_outproj_kernel, slot=slot),
        grid=(M // tm,),
        in_specs=[pl.BlockSpec((tm, K), lambda i: (i, 0)),
                  pl.BlockSpec((K, D), lambda i: (0, 0), pipeline_mode=pl.Buffered(1)),
                  pl.BlockSpec((tm, D), lambda i: (i, 0)),
                  pl.BlockSpec((None, N_MOD, D), _mod_spec(base, rpg, tm)),
                  pl.BlockSpec((1, D), lambda i: (0, 0))],
        out_specs=pl.BlockSpec((tm, D), lambda i: (i, 0)),
        out_shape=jax.ShapeDtypeStruct((M, D), F32),
        compiler_params=_cparams(("arbitrary",)),
        name="outproj",
    )(a, w, x, mod, g_post.reshape(1, D))


def _nat_kernel(q_ref, k_ref, v_ref, kc_ref, vc_ref, bias_ref, o_ref, *, rows, group):
    scale = HEAD ** -0.5
    n_lat = WIN_H * GRID_W
    lane = lax.broadcasted_iota(jnp.int32, (1, LANES), 1)
    head_masks = (lane < HEAD, lane >= HEAD)
    qcol = lax.broadcasted_iota(jnp.int32, (2 * GRID_W, n_lat), 0) % GRID_W
    kcol = lax.broadcasted_iota(jnp.int32, (2 * GRID_W, n_lat), 1) % GRID_W
    cstart = jnp.clip(qcol - WIN_W // 2, 0, GRID_W - WIN_W)
    col_ok = (kcol >= cstart) & (kcol < cstart + WIN_W)
    kc = kc_ref[...]
    vc = vc_ref[...]
    zero = jnp.zeros((), BF16)

    def step(i, carry):
        chains = []
        for rr in range(group):
            r = i * group + rr
            r0 = jnp.clip(r - WIN_H // 2, 0, rows - WIN_H)
            q = q_ref[pl.ds(pl.multiple_of(r * GRID_W, GRID_W), GRID_W), :] * scale
            q2 = jnp.concatenate([jnp.where(head_masks[0], q, zero), jnp.where(head_masks[1], q, zero)], axis=0)
            kw = k_ref[pl.ds(pl.multiple_of(r0 * GRID_W, GRID_W), n_lat), :]
            vw = v_ref[pl.ds(pl.multiple_of(r0 * GRID_W, GRID_W), n_lat), :]
            chains.append((r, r0, q2, kw, vw))
        s = [_dot_nt(q2, kw) for (_, _, q2, kw, _) in chains]
        sc = [_dot_nt(q2, kc) for (_, _, q2, _, _) in chains]
        s = [jnp.where(col_ok, si + jnp.concatenate(
                [jnp.concatenate([bias_ref[h, 2 * jj - (r - r0) + WIN_H - 1] for jj in range(WIN_H // 2)], axis=1)
                 for h in range(2)], axis=0), MASK_VALUE)
             for si, (r, r0, _, _, _) in zip(s, chains)]
        m = [jnp.maximum(jnp.max(si, axis=-1, keepdims=True), jnp.max(ci, axis=-1, keepdims=True))
             for si, ci in zip(s, sc)]
        e = [jnp.exp(si - mi) for si, mi in zip(s, m)]
        ec = [jnp.exp(ci - mi) for ci, mi in zip(sc, m)]
        den = [jnp.sum(ei, axis=-1, keepdims=True) + jnp.sum(ci, axis=-1, keepdims=True) for ei, ci in zip(e, ec)]
        o = [(_dot(ei.astype(BF16), vw) + _dot(ci.astype(BF16), vc)) / di
             for ei, ci, di, (_, _, _, _, vw) in zip(e, ec, den, chains)]
        for rr in range(group):
            r = i * group + rr
            o_ref[pl.ds(pl.multiple_of(r * GRID_W, GRID_W), GRID_W), :] = jnp.where(
                head_masks[0], o[rr][:GRID_W], o[rr][GRID_W:]).astype(o_ref.dtype)
        return carry

    lax.fori_loop(0, rows // group, step, 0)


def _nat_bias_table(rpb):
    edge = GRID_W - WIN_W
    padded = jnp.pad(rpb, ((0, 0), (0, 0), (edge, edge)), mode="edge")
    rows = jnp.stack([padded[:, :, GRID_W - 1 - q:2 * GRID_W - 1 - q] for q in range(GRID_W)], axis=2)
    return jnp.concatenate([rows[:, :-1], rows[:, 1:]], axis=-1).astype(F32)


def _nat_attention(qkv, qkv_c, bias, *, group=4):
    B, T, D3 = qkv.shape
    D = D3 // 3
    C = qkv_c.shape[1]
    nd = D // LANES
    rows = T // GRID_W
    assert rows >= WIN_H and rows % group == 0
    return pl.pallas_call(
        functools.partial(_nat_kernel, rows=rows, group=group),
        grid=(B, nd),
        in_specs=[pl.BlockSpec((None, T, LANES), lambda b, p: (b, 0, p)),
                  pl.BlockSpec((None, T, LANES), lambda b, p: (b, 0, nd + p)),
                  pl.BlockSpec((None, T, LANES), lambda b, p: (b, 0, 2 * nd + p)),
                  pl.BlockSpec((None, C, LANES), lambda b, p: (b, 0, nd + p)),
                  pl.BlockSpec((None, C, LANES), lambda b, p: (b, 0, 2 * nd + p)),
                  pl.BlockSpec((2, 2 * WIN_H - 2, GRID_W, 2 * GRID_W), lambda b, p: (p, 0, 0, 0))],
        out_specs=pl.BlockSpec((None, T, LANES), lambda b, p: (b, 0, p)),
        out_shape=jax.ShapeDtypeStruct((B, T, D), BF16),
        compiler_params=_cparams(("arbitrary", "arbitrary")),
        name="nat_attention",
    )(qkv, qkv, qkv, qkv_c, qkv_c, bias)


def _ctx_attn_kernel(q_ref, k_ref, v_ref, o_ref):
    scale = HEAD ** -0.5
    lane = lax.broadcasted_iota(jnp.int32, (1, LANES), 1)
    head_masks = (lane < HEAD, lane >= HEAD)
    q = q_ref[...] * scale
    k = k_ref[...]
    v = v_ref[...]
    zero = jnp.zeros((), BF16)
    outs = []
    for h in range(2):
        s = _dot_nt(jnp.where(head_masks[h], q, zero), k)
        e = jnp.exp(s - jnp.max(s, axis=-1, keepdims=True))
        outs.append(_dot(e.astype(BF16), v) / jnp.sum(e, axis=-1, keepdims=True))
    o_ref[...] = jnp.where(head_masks[0], outs[0], outs[1]).astype(o_ref.dtype)


def _ctx_attention(qkv_c):
    B, C, D3 = qkv_c.shape
    D = D3 // 3
    nd = D // LANES
    return pl.pallas_call(
        _ctx_attn_kernel,
        grid=(B, nd),
        in_specs=[pl.BlockSpec((None, C, LANES), lambda b, p: (b, 0, p)),
                  pl.BlockSpec((None, C, LANES), lambda b, p: (b, 0, nd + p)),
                  pl.BlockSpec((None, C, LANES), lambda b, p: (b, 0, 2 * nd + p))],
        out_specs=pl.BlockSpec((None, C, LANES), lambda b, p: (b, 0, p)),
        out_shape=jax.ShapeDtypeStruct((B, C, D), BF16),
        compiler_params=_cparams(("arbitrary", "arbitrary")),
        name="ctx_attention",
    )(qkv_c, qkv_c, qkv_c)


def _pad_to(w, axis, mult=LANES):
    n = w.shape[axis]
    pad = (-n) % mult
    if pad == 0:
        return w
    widths = [(0, 0)] * w.ndim
    widths[axis] = (0, pad)
    return jnp.pad(w, widths)


def _lora_in(w):
    w = _pad_to(w, 2)
    return jnp.concatenate(list(w), axis=1).astype(BF16)


def _lora_out(w):
    return _pad_to(w, 1).astype(BF16)


def _scan_inputs(q, B, L, D):
    three = lambda t: t.reshape(B, L, D)
    four = lambda t: t.reshape(2, B, L, D)
    return three(q["r"]), three(q["v"]), three(q["kk"]), four(q["lw"]), four(q["kd"]), four(q["a"])


def kernel(x, c, ctx, c_ctx, ada_w, ada_b, norm_pre, norm_post, ffn_w_gate, ffn_w_up, ffn_w_down, rwkv_mix, rwkv_w_r, rwkv_w_k, rwkv_w_v, rwkv_w_o, rwkv_w0, rwkv_w1, rwkv_w2, rwkv_a0, rwkv_a1, rwkv_a2, rwkv_v0, rwkv_v1, rwkv_v2, rwkv_k_k, rwkv_k_a, rwkv_r_k, rwkv_g1, rwkv_g2, rwkv_gn_w, rwkv_gn_b, nat_w_qkv, nat_w_o, nat_rpb):
    B, T, D = x.shape
    C = ctx.shape[1]
    depth = ada_w.shape[0]
    assert B + 1 <= 8 and D % (2 * LANES) == 0
    cvec = jnp.zeros((8, D), F32).at[:B].set(c).at[B].set(c_ctx)
    mods = _adaln(cvec, ada_w, ada_b).reshape(depth, 8, N_MOD, D)
    wg, wu, wd = (w.astype(BF16) for w in (ffn_w_gate, ffn_w_up, ffn_w_down))
    grp_l, grp_c = (0, T), (B, B * C)
    xl = x.reshape(B * T, D)
    xc = ctx.reshape(B * C, D)
    vf_l = vf_c = None
    for i in range(depth):
        last = i == depth - 1
        j = i // 2
        mod = mods[i]
        ffn = functools.partial(_ffn, mod=mod, wg=wg, wu=wu, wd=wd, layer=i)
        xl = ffn(xl, grp=grp_l, slot=0, g_pre=norm_pre[i, 0], g_post=norm_post[i, 0], which=0)
        xc = ffn(xc, grp=grp_c, slot=0, g_pre=norm_pre[i, 0], g_post=norm_post[i, 0], which=0)
        if i % 2 == 0:
            p = dict(
                mix=rwkv_mix[j], w_r=rwkv_w_r[j].astype(BF16), w_k=rwkv_w_k[j].astype(BF16),
                w_v=rwkv_w_v[j].astype(BF16), w0=rwkv_w0[j], w1=_lora_in(rwkv_w1[j]), w2=_lora_out(rwkv_w2[j]),
                a0=rwkv_a0[j], a1=_lora_in(rwkv_a1[j]), a2=_lora_out(rwkv_a2[j]),
                k_k=rwkv_k_k[j], k_a=rwkv_k_a[j], r_k=rwkv_r_k[j].reshape(D),
                g1=_lora_in(rwkv_g1[j][None]), g2=_lora_out(rwkv_g2[j][None]),
                gn_w=rwkv_gn_w[j], gn_b=rwkv_gn_b[j],
                v_res=None if j == 0 else (rwkv_v0[j - 1], _lora_in(rwkv_v1[j - 1][None]), _lora_out(rwkv_v2[j - 1][None])))
            q_c = _rwkv_prep(xc, mod, grp_c, norm_pre[i, 1], p, vf_c, C)
            q_l = _rwkv_prep(xl, mod, grp_l, norm_pre[i, 1], p, vf_l, T)
            if j == 0:
                vf_l, vf_c = q_l["v"], q_c["v"]
            s0 = jnp.zeros((2, B, D // LANES, LANES, LANES), F32)
            y_c, s_c = _wkv_scan(*_scan_inputs(q_c, B, C, D), s0)
            y_l, _ = _wkv_scan(*_scan_inputs(q_l, B, T, D), s_c)
            w_o = rwkv_w_o[j].astype(BF16)
            xl = _rwkv_out(y_l.reshape(2, B * T, D), q_l, p, w_o, xl, mod, grp_l, norm_post[i, 1])
            if not last:
                xc = _rwkv_out(y_c.reshape(2, B * C, D), q_c, p, w_o, xc, mod, grp_c, norm_post[i, 1])
        else:
            w_qkv = nat_w_qkv[j].astype(BF16)
            w_o = nat_w_o[j].astype(BF16)
            qkv_l = _normmod_mm(xl, mod, grp_l, 1, norm_pre[i, 1], w_qkv, BF16).reshape(B, T, 3 * D)
            qkv_c = _normmod_mm(xc, mod, grp_c, 1, norm_pre[i, 1], w_qkv, BF16).reshape(B, C, 3 * D)
            a_l = _nat_attention(qkv_l, qkv_c, _nat_bias_table(nat_rpb[j])).reshape(B * T, D)
            xl = _outproj(a_l, w_o, xl, mod, grp_l, 1, norm_post[i, 1])
            if not last:
                a_c = _ctx_attention(qkv_c).reshape(B * C, D)
                xc = _outproj(a_c, w_o, xc, mod, grp_c, 1, norm_post[i, 1])
        xl = ffn(xl, grp=grp_l, slot=2, g_pre=norm_pre[i, 2], g_post=norm_post[i, 2], which=1)
        if not last:
            xc = ffn(xc, grp=grp_c, slot=2, g_pre=norm_pre[i, 2], g_post=norm_post[i, 2], which=1)
    return xl.reshape(B, T, D)
```

```python
import functools

import jax
import jax.numpy as jnp
from jax import lax
from jax.experimental import pallas as pl
from jax.experimental.pallas import tpu as pltpu

F32 = jnp.float32
BF16 = jnp.bfloat16

LANES = 128
SUBLANES = 8
ROW_SLAB = 32
HEAD = 64
N_MOD = 9
MACARON_WEIGHT = 0.5
RMS_EPS = 1e-6
RWKV_GN_EPS = 64e-5
L2_EPS_SQ = 1e-24
GRID_W = 64
WIN_H = 8
WIN_W = 16
MASK_VALUE = -1e30
VMEM_LIMIT = 56 * 1024 * 1024


def _cparams(sem):
    return pltpu.CompilerParams(dimension_semantics=sem, vmem_limit_bytes=VMEM_LIMIT)


def _dot(a, b):
    return jnp.dot(a, b, preferred_element_type=F32)


def _dot_nt(a, b):
    return lax.dot_general(a, b, (((1,), (1,)), ((), ())), preferred_element_type=F32)


def _dot_tn(a, b):
    return lax.dot_general(a, b, (((0,), (0,)), ((), ())), preferred_element_type=F32)


def _split3(x):
    hi = x.astype(BF16)
    r1 = x - hi.astype(F32)
    mid = r1.astype(BF16)
    lo = (r1 - mid.astype(F32)).astype(BF16)
    return hi, mid, lo


def _dot_exact_lhs01(m01, x):
    hi, mid, lo = _split3(x)
    return _dot(m01, hi) + _dot(m01, mid) + _dot(m01, lo)


def _dot_exact_rhs01(x, m01):
    n = x.shape[0]
    parts = _dot(jnp.concatenate(_split3(x), axis=0), m01)
    return parts[:n] + parts[n:2 * n] + parts[2 * n:]


def _head_ones(n):
    r = lax.broadcasted_iota(jnp.int32, (n, n), 0) // HEAD
    c = lax.broadcasted_iota(jnp.int32, (n, n), 1) // HEAD
    return (r == c).astype(BF16)


def _softplus(x):
    return jnp.maximum(x, 0.0) + jnp.log1p(jnp.exp(-jnp.abs(x)))


def _rms(x, g):
    return x * lax.rsqrt(jnp.mean(x * x, axis=-1, keepdims=True) + RMS_EPS) * g


def _modulated(x, mod_ref, gpre_ref, slot):
    shift = mod_ref[3 * slot:3 * slot + 1, :]
    scale = mod_ref[3 * slot + 1:3 * slot + 2, :]
    return _rms(x, gpre_ref[...]) * (1 + scale) + shift


def _tile(n, want, unit):
    if n <= want:
        return n
    t = (want // unit) * unit
    while n % t:
        t -= unit
    return t


def _row_slabs(rows, fn):
    slab = _tile(rows, ROW_SLAB, 2 * SUBLANES)

    def body(i, carry):
        fn(pl.ds(pl.multiple_of(i * slab, slab), slab))
        return carry

    lax.fori_loop(0, rows // slab, body, 0, unroll=min(8, rows // slab))


def _modulated_rows(x_ref, mod_ref, gpre_ref, slot, dst_ref):
    shift = mod_ref[3 * slot:3 * slot + 1, :]
    scale1 = 1 + mod_ref[3 * slot + 1:3 * slot + 2, :]
    g = gpre_ref[...]

    def slab(sl):
        dst_ref[sl, :] = (_rms(x_ref[sl, :], g) * scale1 + shift).astype(dst_ref.dtype)

    _row_slabs(x_ref.shape[0], slab)


def _gated_residual_rows(x_ref, y_ref, wgate, gpost_ref, o_ref):
    g = gpost_ref[...]
    rows = x_ref.shape[0]
    slab = _tile(rows, ROW_SLAB, 2 * SUBLANES)
    for i in range(rows // slab):
        sl = slice(i * slab, (i + 1) * slab)
        o_ref[sl, :] = x_ref[sl, :] + wgate * _rms(y_ref[sl, :], g)


def _mod_spec(base, rows_per_group, tm):
    assert rows_per_group % tm == 0
    return lambda i, *_: (base + (i * tm) // rows_per_group, 0, 0)


def _wkv_kernel(r_ref, v_ref, kk_ref, lw_ref, kd_ref, a_ref, s0_ref, y_ref, sfin_ref, s_scr,
                *, chunk, pairs, nb):
    z = pl.program_id(0)
    c = pl.program_id(3)
    C = chunk
    side_by_side = lambda ref: jnp.concatenate([ref[b] for b in range(nb)], axis=1)

    @pl.when(c == 0)
    def _():
        for b in range(nb):
            s_scr[b * pairs:(b + 1) * pairs] = s0_ref[b]

    row = lax.broadcasted_iota(jnp.int32, (C, 2 * C), 0)
    pcol = lax.broadcasted_iota(jnp.int32, (C, 2 * C), 1)
    col = pcol % C
    d = (row - col) * (1 - 2 * z)
    strict = d > 0
    incl = d >= 0
    half_masks = (pcol < C, pcol >= C)
    lane = lax.broadcasted_iota(jnp.int32, (1, LANES), 1)
    head_masks = (lane < HEAD, lane >= HEAD)
    srow = lax.broadcasted_iota(jnp.int32, (LANES, LANES), 0)
    scol = lax.broadcasted_iota(jnp.int32, (LANES, LANES), 1)
    blockdiag = (srow < HEAD) == (scol < HEAD)
    eye = (row == col).astype(BF16)
    levels = []
    s = 1
    while s < C:
        levels.append((row // (2 * s) == col // (2 * s)) & (row // s != col // s))
        s *= 2

    def blockdiag2(m):
        zero = jnp.zeros_like(m)
        return jnp.concatenate([jnp.where(half_masks[0], m, zero), jnp.where(half_masks[1], m, zero)], axis=0)

    def by_head(x):
        zero = jnp.zeros_like(x)
        return jnp.concatenate([jnp.where(head_masks[0], x, zero), jnp.where(head_masks[1], x, zero)], axis=0)

    lw = side_by_side(lw_ref)
    kk = side_by_side(kk_ref)
    kd = side_by_side(kd_ref)
    L = _dot_exact_lhs01(incl[:, :C].astype(BF16), lw)
    l_end = jnp.sum(lw, axis=0, keepdims=True)
    lm = 0.5 * l_end
    e_m = jnp.exp(-lm)
    e_p = jnp.exp(lm)
    e_sh = jnp.exp(lm - L)
    at_t = -kk * jnp.exp(L - lw)
    rt_t = side_by_side(r_ref) * jnp.exp(L)
    sr_all = jnp.concatenate([at_t, rt_t], axis=0).astype(BF16)
    at_sh = at_t * e_m
    rt_sh = rt_t * e_m
    bt = (kk * side_by_side(a_ref)) * e_sh
    kt = kd * e_sh
    bk_all = jnp.concatenate([bt, kt], axis=0).astype(BF16)
    bkh_all = jnp.concatenate([bt * e_p, kt * e_p], axis=0).astype(BF16)
    v_all = side_by_side(v_ref).astype(BF16)
    s_decay = jnp.exp(l_end)

    prs = range(nb * pairs)
    lanes = [slice(p * LANES, (p + 1) * LANES) for p in prs]
    sr = [_dot_nt(sr_all[:, lanes[p]], s_scr[p].astype(BF16)) for p in prs]
    v_h = [by_head(v_all[:, lanes[p]]) for p in prs]
    quads = [_dot_nt(jnp.concatenate([at_sh[:, lanes[p]], rt_sh[:, lanes[p]]], axis=0).astype(BF16),
                     jnp.concatenate([by_head(bk_all[:C, lanes[p]]), by_head(bk_all[C:, lanes[p]])], axis=0))
             for p in prs]
    a_ab = [jnp.where(strict, q[:C, :2 * C], 0.0).astype(BF16) for q in quads]
    a_ak = [jnp.where(strict, q[:C, 2 * C:], 0.0).astype(BF16) for q in quads]
    r_bk = [jnp.concatenate([jnp.where(incl, q[C:, :2 * C], 0.0), jnp.where(incl, q[C:, 2 * C:], 0.0)],
                            axis=1).astype(BF16) for q in quads]
    t = [eye + jnp.where(levels[0], a, jnp.zeros_like(a)) for a in a_ab]
    for lv in levels[1:]:
        x = [jnp.where(lv, a, jnp.zeros_like(a)) for a in a_ab]
        m1 = [_dot(xi, blockdiag2(ti)).astype(BF16) for xi, ti in zip(x, t)]
        m2 = [_dot(ti, blockdiag2(mi)).astype(BF16) for ti, mi in zip(t, m1)]
        t = [ti + mi for ti, mi in zip(t, m2)]
    rhs = [(sr[p][:C] + _dot(a_ak[p], v_h[p])).astype(BF16) for p in prs]
    u = [_dot(t[p], by_head(rhs[p])).astype(BF16) for p in prs]
    y = [sr[p][C:] + _dot(r_bk[p], jnp.concatenate([by_head(u[p]), v_h[p]], axis=0)) for p in prs]
    for p in prs:
        y_ref[p // pairs, :, lanes[p % pairs]] = y[p]
        upd = _dot_tn(jnp.concatenate([u[p], v_all[:, lanes[p]]], axis=0), bkh_all[:, lanes[p]])
        s_scr[p] = s_scr[p] * s_decay[:, lanes[p]] + jnp.where(blockdiag, upd, 0.0)

    @pl.when(c == pl.num_programs(3) - 1)
    def _():
        for b in range(nb):
            sfin_ref[b] = s_scr[b * pairs:(b + 1) * pairs]


def _wkv_scan(r, v, kk, lw, kd, a, s0, *, chunk=64, pairs=16, nb=2):
    B, T, D = r.shape
    pairs = min(pairs, D // LANES)
    nb = nb if B % nb == 0 else 1
    lw_lanes = pairs * LANES
    assert T % chunk == 0 and D % lw_lanes == 0
    nc = T // chunk
    ng = D // lw_lanes

    def cidx(z, c):
        return c + z * (nc - 1 - 2 * c)

    tok_spec = pl.BlockSpec((nb, chunk, lw_lanes), lambda z, b, g, c: (b, cidx(z, c), g))
    dir_spec = pl.BlockSpec((None, nb, chunk, lw_lanes), lambda z, b, g, c: (z, b, cidx(z, c), g))
    st_spec = pl.BlockSpec((None, nb, pairs, LANES, LANES), lambda z, b, g, c: (z, b, g, 0, 0))
    y, s_fin = pl.pallas_call(
        functools.partial(_wkv_kernel, chunk=chunk, pairs=pairs, nb=nb),
        grid=(2, B // nb, ng, nc),
        in_specs=[tok_spec, tok_spec, tok_spec, dir_spec, dir_spec, dir_spec, st_spec],
        out_specs=[dir_spec, st_spec],
        out_shape=[jax.ShapeDtypeStruct((2, B, T, D), F32),
                   jax.ShapeDtypeStruct(s0.shape, F32)],
        scratch_shapes=[pltpu.VMEM((nb * pairs, LANES, LANES), F32)],
        compiler_params=_cparams(("arbitrary", "arbitrary", "arbitrary", "arbitrary")),
        name="wkv_scan",
    )(r, v, kk, lw, kd, a, s0)
    return y, s_fin


def _rwkv_prep_kernel(*refs, seq_len, tm, has_vres):
    (x_ref, xp_ref, xn_ref, mod_ref, gpre_ref, mix_ref, wr_ref, wk_ref, wv_ref,
     w1_ref, w2_ref, w0_ref, a1_ref, a2_ref, a0_ref, g1_ref, g2_ref, kk_ref, ka_ref) = refs[:19]
    pos = 19
    if has_vres:
        v1_ref, v2_ref, v0_ref, vf_ref = refs[pos:pos + 4]
        pos += 4
    r_ref, v_ref, kkn_ref, g_ref, lw_ref, kd_ref, a_ref = refs[pos:pos + 7]
    pos += 7
    xmix_scr, tw_scr, ta_scr, tg_scr = refs[pos:pos + 4]
    tv_scr = refs[pos + 4] if has_vres else None
    i = pl.program_id(0)
    n = pl.program_id(1)

    @pl.when(n == 0)
    def _():
        h = _modulated(x_ref[...], mod_ref, gpre_ref, 1)
        t0 = i * tm
        hp = _modulated(xp_ref[...], mod_ref, gpre_ref, 1)[SUBLANES - 1:SUBLANES, :]
        hn = _modulated(xn_ref[...], mod_ref, gpre_ref, 1)[0:1, :]
        hp = jnp.where(t0 % seq_len == 0, 0.0, hp)
        hn = jnp.where((t0 + tm) % seq_len == 0, 0.0, hn)
        rowid = lax.broadcasted_iota(jnp.int32, (tm, 1), 0)
        prev = jnp.where(rowid == 0, hp, pltpu.roll(h, 1, 0))
        nxt = jnp.where(rowid == tm - 1, hn, pltpu.roll(h, tm - 1, 0))
        xx = 0.5 * (prev + nxt) - h
        mixed = lambda m: (h + xx * mix_ref[m:m + 1, :]).astype(BF16)
        xmix_scr[0] = mixed(0)
        xmix_scr[1] = mixed(2)
        xv = mixed(3)
        xmix_scr[2] = xv
        tw_scr[...] = jnp.tanh(_dot(mixed(1), w1_ref[...])).astype(BF16)
        ta_scr[...] = _dot(mixed(4), a1_ref[...]).astype(BF16)
        tg_scr[...] = jax.nn.sigmoid(_dot(mixed(5), g1_ref[...])).astype(BF16)
        if has_vres:
            tv_scr[...] = _dot(xv, v1_ref[...]).astype(BF16)

    r_ref[...] = _dot(xmix_scr[0], wr_ref[...])
    k = _dot(xmix_scr[1], wk_ref[...])
    v = _dot(xmix_scr[2], wv_ref[...])
    if has_vres:
        v = v + (vf_ref[...] - v) * jax.nn.sigmoid(v0_ref[...] + _dot(tv_scr[...], v2_ref[...]))
    v_ref[...] = v
    g_ref[...] = _dot(tg_scr[...], g2_ref[...])
    kk = k * kk_ref[...]
    ssq = _dot_exact_rhs01(kk * kk, _head_ones(kk.shape[1]))
    kkn_ref[...] = kk * lax.rsqrt(jnp.maximum(ssq, L2_EPS_SQ))
    lp = w2_ref.shape[1]
    la = a2_ref.shape[1]
    for zz in range(2):
        lora_w = _dot(tw_scr[:, zz * lp:(zz + 1) * lp], w2_ref[zz])
        w_log = -_softplus(-(w0_ref[zz:zz + 1, :] + lora_w)) - 0.5
        lw_ref[zz] = -jnp.exp(w_log)
        a = jax.nn.sigmoid(a0_ref[zz:zz + 1, :] + _dot(ta_scr[:, zz * la:(zz + 1) * la], a2_ref[zz]))
        a_ref[zz] = a
        kd_ref[zz] = k * (1 + (a - 1) * ka_ref[...])


def _rwkv_prep(x, mod, grp, g_pre, p, v_first, seq_len, *, tm=512, tn=256):
    M, D = x.shape
    tm = _tile(seq_len, tm, SUBLANES)
    tn = _tile(D, tn, LANES)
    base, rpg = grp
    wtile = pl.BlockSpec((D, tn), lambda i, n: (0, n))
    has_vres = p["v_res"] is not None
    nb = M // SUBLANES
    row = lambda i, n: (i, 0)
    col = lambda i, n: (0, n)
    full = lambda i, n: (0, 0)
    col3 = lambda i, n: (0, 0, n)
    tile = lambda i, n: (i, n)
    tile3 = lambda i, n: (0, i, n)
    lw1, la1, lg1 = p["w1"].shape[1], p["a1"].shape[1], p["g1"].shape[1]
    in_specs = [
        pl.BlockSpec((tm, D), row),
        pl.BlockSpec((SUBLANES, D), lambda i, n: (jnp.maximum(i * (tm // SUBLANES) - 1, 0), 0)),
        pl.BlockSpec((SUBLANES, D), lambda i, n: (jnp.minimum((i + 1) * (tm // SUBLANES), nb - 1), 0)),
        pl.BlockSpec((None, N_MOD, D), _mod_spec(base, rpg, tm)),
        pl.BlockSpec((1, D), full),
        pl.BlockSpec((6, D), full),
        wtile, wtile, wtile,
        pl.BlockSpec((D, lw1), full), pl.BlockSpec((2, lw1 // 2, tn), col3), pl.BlockSpec((2, tn), col),
        pl.BlockSpec((D, la1), full), pl.BlockSpec((2, la1 // 2, tn), col3), pl.BlockSpec((2, tn), col),
        pl.BlockSpec((D, lg1), full), pl.BlockSpec((lg1, tn), col),
        pl.BlockSpec((1, tn), col), pl.BlockSpec((1, tn), col),
    ]
    args = [x, x, x, mod, g_pre.reshape(1, D), p["mix"], p["w_r"], p["w_k"], p["w_v"],
            p["w1"], p["w2"], p["w0"], p["a1"], p["a2"], p["a0"], p["g1"], p["g2"][0],
            p["k_k"].reshape(1, D), p["k_a"].reshape(1, D)]
    scratch = [pltpu.VMEM((3, tm, D), BF16), pltpu.VMEM((tm, lw1), BF16),
               pltpu.VMEM((tm, la1), BF16), pltpu.VMEM((tm, lg1), BF16)]
    if has_vres:
        v0, v1, v2 = p["v_res"]
        lv1 = v1.shape[1]
        in_specs += [pl.BlockSpec((D, lv1), full), pl.BlockSpec((lv1, tn), col),
                     pl.BlockSpec((1, tn), col), pl.BlockSpec((tm, tn), tile)]
        args += [v1, v2[0], v0.reshape(1, D), v_first]
        scratch.append(pltpu.VMEM((tm, lv1), BF16))
    one = jax.ShapeDtypeStruct((M, D), F32)
    two = jax.ShapeDtypeStruct((2, M, D), F32)
    r, v, kk, g, lw, kd, a = pl.pallas_call(
        functools.partial(_rwkv_prep_kernel, seq_len=seq_len, tm=tm, has_vres=has_vres),
        grid=(M // tm, D // tn),
        in_specs=in_specs,
        out_specs=[pl.BlockSpec((tm, tn), tile)] * 4 + [pl.BlockSpec((2, tm, tn), tile3)] * 3,
        out_shape=[one] * 4 + [two] * 3,
        scratch_shapes=scratch,
        compiler_params=_cparams(("arbitrary", "arbitrary")),
        name="rwkv_prep",
    )(*args)
    return dict(r=r, v=v, kk=kk, g=g, lw=lw, kd=kd, a=a)


def _rwkv_out_kernel(y_ref, r_ref, kd_ref, v_ref, g_ref, rk_ref, gnw_ref, gnb_ref, w_ref, x_ref, mod_ref,
                     gpost_ref, o_ref, pre_scr, *, slab):
    D = x_ref.shape[1]
    ones = _head_ones(slab)
    for j in range(D // slab):
        sl = slice(j * slab, (j + 1) * slab)
        wkv = y_ref[0, :, sl] + y_ref[1, :, sl]
        mu = _dot_exact_rhs01(wkv, ones) * (1.0 / HEAD)
        cen = wkv - mu
        var = _dot_exact_rhs01(cen * cen, ones) * (1.0 / HEAD)
        o = cen * lax.rsqrt(var + RWKV_GN_EPS) * gnw_ref[:, sl] + gnb_ref[:, sl]
        coef = _dot_exact_rhs01(r_ref[:, sl] * (kd_ref[0, :, sl] + kd_ref[1, :, sl]) * rk_ref[:, sl], ones)
        pre_scr[:, sl] = ((o + coef * v_ref[:, sl]) * g_ref[:, sl]).astype(BF16)
    y = _dot(pre_scr[...], w_ref[...])
    o_ref[...] = x_ref[...] + mod_ref[5:6, :] * _rms(y, gpost_ref[...])


def _rwkv_out(y, q, p, w_o, x, mod, grp, g_post, *, tm=256, slab=256):
    M, D = x.shape
    tm = _tile(M, tm, SUBLANES)
    slab = _tile(D, slab, LANES)
    base, rpg = grp
    row = lambda i: (i, 0)
    row3 = lambda i: (0, i, 0)
    full = lambda i: (0, 0)
    return pl.pallas_call(
        functools.partial(_rwkv_out_kernel, slab=slab),
        grid=(M // tm,),
        in_specs=[pl.BlockSpec((2, tm, D), row3), pl.BlockSpec((tm, D), row), pl.BlockSpec((2, tm, D), row3),
                  pl.BlockSpec((tm, D), row), pl.BlockSpec((tm, D), row),
                  pl.BlockSpec((1, D), full), pl.BlockSpec((1, D), full), pl.BlockSpec((1, D), full),
                  pl.BlockSpec((D, D), full, pipeline_mode=pl.Buffered(1)), pl.BlockSpec((tm, D), row),
                  pl.BlockSpec((None, N_MOD, D), _mod_spec(base, rpg, tm)), pl.BlockSpec((1, D), full)],
        out_specs=pl.BlockSpec((tm, D), row),
        out_shape=jax.ShapeDtypeStruct((M, D), F32),
        scratch_shapes=[pltpu.VMEM((tm, D), BF16)],
        compiler_params=_cparams(("arbitrary",)),
        name="rwkv_out",
    )(y, q["r"], q["kd"], q["v"], q["g"], p["r_k"].reshape(1, D), p["gn_w"].reshape(1, D),
      p["gn_b"].reshape(1, D), w_o, x, mod, g_post.reshape(1, D))


def _adaln_kernel(c_ref, w_ref, b_ref, o_ref):
    cv = c_ref[...]
    cv = cv * jax.nn.sigmoid(cv)
    o_ref[...] = _dot(cv.astype(BF16), w_ref[...].astype(BF16)) + b_ref[...]


def _adaln(cvec, ada_w, ada_b, *, bn=1024):
    depth, D, N = ada_w.shape
    bn = _tile(N, bn, LANES)
    return pl.pallas_call(
        _adaln_kernel,
        grid=(depth, N // bn),
        in_specs=[pl.BlockSpec((8, D), lambda i, n: (0, 0)),
                  pl.BlockSpec((None, D, bn), lambda i, n: (i, 0, n)),
                  pl.BlockSpec((None, 1, bn), lambda i, n: (i, 0, n))],
        out_specs=pl.BlockSpec((None, 8, bn), lambda i, n: (i, 0, n)),
        out_shape=jax.ShapeDtypeStruct((depth, 8, N), F32),
        compiler_params=_cparams(("arbitrary", "arbitrary")),
        name="adaln",
    )(cvec, ada_w, ada_b.reshape(depth, 1, N))


def _ffn_kernel(x_ref, mod_ref, gpre_ref, gpost_ref, wg_ref, wu_ref, wd_ref, o_ref, h_scr, *, slot):
    f = pl.program_id(1)

    @pl.when(f == 0)
    def _():
        _modulated_rows(x_ref, mod_ref, gpre_ref, slot, h_scr)
        o_ref[...] = jnp.zeros_like(o_ref)

    h = h_scr[...]
    g = _dot(h, wg_ref[...])
    u = _dot(h, wu_ref[...])
    a = (g * jax.nn.sigmoid(g)) * u
    o_ref[...] += _dot(a.astype(BF16), wd_ref[...])

    @pl.when(f == pl.num_programs(1) - 1)
    def _():
        gate = mod_ref[3 * slot + 2:3 * slot + 3, :]
        _gated_residual_rows(x_ref, o_ref, MACARON_WEIGHT * gate, gpost_ref, o_ref)


def _ffn(x, mod, grp, slot, g_pre, g_post, wg, wu, wd, layer, which, *, tm=1024, tf=512):
    M, D = x.shape
    F = wg.shape[-1]
    tm = _tile(min(M, grp[1]), tm, SUBLANES)
    tf = _tile(F, tf, LANES)
    base, rpg = grp
    return pl.pallas_call(
        functools.partial(_ffn_kernel, slot=slot),
        grid=(M // tm, F // tf),
        in_specs=[pl.BlockSpec((tm, D), lambda i, f: (i, 0)),
                  pl.BlockSpec((None, N_MOD, D), _mod_spec(base, rpg, tm)),
                  pl.BlockSpec((1, D), lambda i, f: (0, 0)),
                  pl.BlockSpec((1, D), lambda i, f: (0, 0)),
                  pl.BlockSpec((None, None, D, tf), lambda i, f: (layer, which, 0, f)),
                  pl.BlockSpec((None, None, D, tf), lambda i, f: (layer, which, 0, f)),
                  pl.BlockSpec((None, None, tf, D), lambda i, f: (layer, which, f, 0))],
        out_specs=pl.BlockSpec((tm, D), lambda i, f: (i, 0)),
        out_shape=jax.ShapeDtypeStruct((M, D), F32),
        scratch_shapes=[pltpu.VMEM((tm, D), BF16)],
        compiler_params=_cparams(("arbitrary", "arbitrary")),
        name="ffn",
    )(x, mod, g_pre.reshape(1, D), g_post.reshape(1, D), wg, wu, wd)


def _normmod_mm_kernel(x_ref, mod_ref, gpre_ref, w_ref, o_ref, h_scr, *, slot):
    @pl.when(pl.program_id(1) == 0)
    def _():
        _modulated_rows(x_ref, mod_ref, gpre_ref, slot, h_scr)

    o_ref[...] = _dot(h_scr[...], w_ref[...]).astype(o_ref.dtype)


def _normmod_mm(x, mod, grp, slot, g_pre, w, out_dtype, *, tm=1024, tn=512):
    M, D = x.shape
    N = w.shape[1]
    tm = _tile(min(M, grp[1]), tm, SUBLANES)
    tn = _tile(N, tn, LANES)
    base, rpg = grp
    return pl.pallas_call(
        functools.partial(_normmod_mm_kernel, slot=slot),
        grid=(M // tm, N // tn),
        in_specs=[pl.BlockSpec((tm, D), lambda i, n: (i, 0)),
                  pl.BlockSpec((None, N_MOD, D), _mod_spec(base, rpg, tm)),
                  pl.BlockSpec((1, D), lambda i, n: (0, 0)),
                  pl.BlockSpec((D, tn), lambda i, n: (0, n))],
        out_specs=pl.BlockSpec((tm, tn), lambda i, n: (i, n)),
        out_shape=jax.ShapeDtypeStruct((M, N), out_dtype),
        scratch_shapes=[pltpu.VMEM((tm, D), BF16)],
        compiler_params=_cparams(("arbitrary", "arbitrary")),
        name="normmod_mm",
    )(x, mod, g_pre.reshape(1, D), w)


def _outproj_kernel(a_ref, w_ref, x_ref, mod_ref, gpost_ref, o_ref, *, slot):
    y = _dot(a_ref[...], w_ref[...])
    gate = mod_ref[3 * slot + 2:3 * slot + 3, :]
    o_ref[...] = x_ref[...] + gate * _rms(y, gpost_ref[...])


def _outproj(a, w, x, mod, grp, slot, g_post, *, tm=512):
    M, D = x.shape
    K = a.shape[1]
    tm = _tile(M, tm, SUBLANES)
    base, rpg = grp
    return pl.pallas_call(
        functools.partial(_outproj_kernel, slot=slot),
        grid=(M // tm,),
        in_specs=[pl.BlockSpec((tm, K), lambda i: (i, 0)),
                  pl.BlockSpec((K, D), lambda i: (0, 0), pipeline_mode=pl.Buffered(1)),
                  pl.BlockSpec((tm, D), lambda i: (i, 0)),
                  pl.BlockSpec((None, N_MOD, D), _mod_spec(base, rpg, tm)),
                  pl.BlockSpec((1, D), lambda i: (0, 0))],
        out_specs=pl.BlockSpec((tm, D), lambda i: (i, 0)),
        out_shape=jax.ShapeDtypeStruct((M, D), F32),
        compiler_params=_cparams(("arbitrary",)),
        name="outproj",
    )(a, w, x, mod, g_post.reshape(1, D))


def _nat_kernel(q_ref, k_ref, v_ref, kc_ref, vc_ref, bias_ref, o_ref, *, rows, group):
    scale = HEAD ** -0.5
    n_lat = WIN_H * GRID_W
    lane = lax.broadcasted_iota(jnp.int32, (1, LANES), 1)
    head_masks = (lane < HEAD, lane >= HEAD)
    qcol = lax.broadcasted_iota(jnp.int32, (2 * GRID_W, n_lat), 0) % GRID_W
    kcol = lax.broadcasted_iota(jnp.int32, (2 * GRID_W, n_lat), 1) % GRID_W
    cstart = jnp.clip(qcol - WIN_W // 2, 0, GRID_W - WIN_W)
    col_ok = (kcol >= cstart) & (kcol < cstart + WIN_W)
    kc = kc_ref[...]
    vc = vc_ref[...]
    zero = jnp.zeros((), BF16)

    def step(i, carry):
        chains = []
        for rr in range(group):
            r = i * group + rr
            r0 = jnp.clip(r - WIN_H // 2, 0, rows - WIN_H)
            q = q_ref[pl.ds(pl.multiple_of(r * GRID_W, GRID_W), GRID_W), :] * scale
            q2 = jnp.concatenate([jnp.where(head_masks[0], q, zero), jnp.where(head_masks[1], q, zero)], axis=0)
            kw = k_ref[pl.ds(pl.multiple_of(r0 * GRID_W, GRID_W), n_lat), :]
            vw = v_ref[pl.ds(pl.multiple_of(r0 * GRID_W, GRID_W), n_lat), :]
            chains.append((r, r0, q2, kw, vw))
        s = [_dot_nt(q2, kw) for (_, _, q2, kw, _) in chains]
        sc = [_dot_nt(q2, kc) for (_, _, q2, _, _) in chains]
        s = [jnp.where(col_ok, si + jnp.concatenate(
                [jnp.concatenate([bias_ref[h, 2 * jj - (r - r0) + WIN_H - 1] for jj in range(WIN_H // 2)], axis=1)
                 for h in range(2)], axis=0), MASK_VALUE)
             for si, (r, r0, _, _, _) in zip(s, chains)]
        m = [jnp.maximum(jnp.max(si, axis=-1, keepdims=True), jnp.max(ci, axis=-1, keepdims=True))
             for si, ci in zip(s, sc)]
        e = [jnp.exp(si - mi) for si, mi in zip(s, m)]
        ec = [jnp.exp(ci - mi) for ci, mi in zip(sc, m)]
        den = [jnp.sum(ei, axis=-1, keepdims=True) + jnp.sum(ci, axis=-1, keepdims=True) for ei, ci in zip(e, ec)]
        o = [(_dot(ei.astype(BF16), vw) + _dot(ci.astype(BF16), vc)) / di
             for ei, ci, di, (_, _, _, _, vw) in zip(e, ec, den, chains)]
        for rr in range(group):
            r = i * group + rr
            o_ref[pl.ds(pl.multiple_of(r * GRID_W, GRID_W), GRID_W), :] = jnp.where(
                head_masks[0], o[rr][:GRID_W], o[rr][GRID_W:]).astype(o_ref.dtype)
        return carry

    lax.fori_loop(0, rows // group, step, 0)


def _nat_bias_table(rpb):
    edge = GRID_W - WIN_W
    padded = jnp.pad(rpb, ((0, 0), (0, 0), (edge, edge)), mode="edge")
    rows = jnp.stack([padded[:, :, GRID_W - 1 - q:2 * GRID_W - 1 - q] for q in range(GRID_W)], axis=2)
    return jnp.concatenate([rows[:, :-1], rows[:, 1:]], axis=-1).astype(F32)


def _nat_attention(qkv, qkv_c, bias, *, group=4):
    B, T, D3 = qkv.shape
    D = D3 // 3
    C = qkv_c.shape[1]
    nd = D // LANES
    rows = T // GRID_W
    assert rows >= WIN_H and rows % group == 0
    return pl.pallas_call(
        functools.partial(_nat_kernel, rows=rows, group=group),
        grid=(B, nd),
        in_specs=[pl.BlockSpec((None, T, LANES), lambda b, p: (b, 0, p)),
                  pl.BlockSpec((None, T, LANES), lambda b, p: (b, 0, nd + p)),
                  pl.BlockSpec((None, T, LANES), lambda b, p: (b, 0, 2 * nd + p)),
                  pl.BlockSpec((None, C, LANES), lambda b, p: (b, 0, nd + p)),
                  pl.BlockSpec((None, C, LANES), lambda b, p: (b, 0, 2 * nd + p)),
                  pl.BlockSpec((2, 2 * WIN_H - 2, GRID_W, 2 * GRID_W), lambda b, p: (p, 0, 0, 0))],
        out_specs=pl.BlockSpec((None, T, LANES), lambda b, p: (b, 0, p)),
        out_shape=jax.ShapeDtypeStruct((B, T, D), BF16),
        compiler_params=_cparams(("arbitrary", "arbitrary")),
        name="nat_attention",
    )(qkv, qkv, qkv, qkv_c, qkv_c, bias)


def _ctx_attn_kernel(q_ref, k_ref, v_ref, o_ref):
    scale = HEAD ** -0.5
    lane = lax.broadcasted_iota(jnp.int32, (1, LANES), 1)
    head_masks = (lane < HEAD, lane >= HEAD)
    q = q_ref[...] * scale
    k = k_ref[...]
    v = v_ref[...]
    zero = jnp.zeros((), BF16)
    outs = []
    for h in range(2):
        s = _dot_nt(jnp.where(head_masks[h], q, zero), k)
        e = jnp.exp(s - jnp.max(s, axis=-1, keepdims=True))
        outs.append(_dot(e.astype(BF16), v) / jnp.sum(e, axis=-1, keepdims=True))
    o_ref[...] = jnp.where(head_masks[0], outs[0], outs[1]).astype(o_ref.dtype)


def _ctx_attention(qkv_c):
    B, C, D3 = qkv_c.shape
    D = D3 // 3
    nd = D // LANES
    return pl.pallas_call(
        _ctx_attn_kernel,
        grid=(B, nd),
        in_specs=[pl.BlockSpec((None, C, LANES), lambda b, p: (b, 0, p)),
                  pl.BlockSpec((None, C, LANES), lambda b, p: (b, 0, nd + p)),
                  pl.BlockSpec((None, C, LANES), lambda b, p: (b, 0, 2 * nd + p))],
        out_specs=pl.BlockSpec((None, C, LANES), lambda b, p: (b, 0, p)),
        out_shape=jax.ShapeDtypeStruct((B, C, D), BF16),
        compiler_params=_cparams(("arbitrary", "arbitrary")),
        name="ctx_attention",
    )(qkv_c, qkv_c, qkv_c)


def _pad_to(w, axis, mult=LANES):
    n = w.shape[axis]
    pad = (-n) % mult
    if pad == 0:
        return w
    widths = [(0, 0)] * w.ndim
    widths[axis] = (0, pad)
    return jnp.pad(w, widths)


def _lora_in(w):
    w = _pad_to(w, 2)
    return jnp.concatenate(list(w), axis=1).astype(BF16)


def _lora_out(w):
    return _pad_to(w, 1).astype(BF16)


def _scan_inputs(q, B, L, D):
    three = lambda t: t.reshape(B, L, D)
    four = lambda t: t.reshape(2, B, L, D)
    return three(q["r"]), three(q["v"]), three(q["kk"]), four(q["lw"]), four(q["kd"]), four(q["a"])


def kernel(x, c, ctx, c_ctx, ada_w, ada_b, norm_pre, norm_post, ffn_w_gate, ffn_w_up, ffn_w_down, rwkv_mix, rwkv_w_r, rwkv_w_k, rwkv_w_v, rwkv_w_o, rwkv_w0, rwkv_w1, rwkv_w2, rwkv_a0, rwkv_a1, rwkv_a2, rwkv_v0, rwkv_v1, rwkv_v2, rwkv_k_k, rwkv_k_a, rwkv_r_k, rwkv_g1, rwkv_g2, rwkv_gn_w, rwkv_gn_b, nat_w_qkv, nat_w_o, nat_rpb):
    B, T, D = x.shape
    C = ctx.shape[1]
    depth = ada_w.shape[0]
    assert B + 1 <= 8 and D % (2 * LANES) == 0
    cvec = jnp.zeros((8, D), F32).at[:B].set(c).at[B].set(c_ctx)
    mods = _adaln(cvec, ada_w, ada_b).reshape(depth, 8, N_MOD, D)
    wg, wu, wd = (w.astype(BF16) for w in (ffn_w_gate, ffn_w_up, ffn_w_down))
    grp_l, grp_c = (0, T), (B, B * C)
    xl = x.reshape(B * T, D)
    xc = ctx.reshape(B * C, D)
    vf_l = vf_c = None
    for i in range(depth):
        last = i == depth - 1
        j = i // 2
        mod = mods[i]
        ffn = functools.partial(_ffn, mod=mod, wg=wg, wu=wu, wd=wd, layer=i)
        xl = ffn(xl, grp=grp_l, slot=0, g_pre=norm_pre[i, 0], g_post=norm_post[i, 0], which=0)
        xc = ffn(xc, grp=grp_c, slot=0, g_pre=norm_pre[i, 0], g_post=norm_post[i, 0], which=0)
        if i % 2 == 0:
            p = dict(
                mix=rwkv_mix[j], w_r=rwkv_w_r[j].astype(BF16), w_k=rwkv_w_k[j].astype(BF16),
                w_v=rwkv_w_v[j].astype(BF16), w0=rwkv_w0[j], w1=_lora_in(rwkv_w1[j]), w2=_lora_out(rwkv_w2[j]),
                a0=rwkv_a0[j], a1=_lora_in(rwkv_a1[j]), a2=_lora_out(rwkv_a2[j]),
                k_k=rwkv_k_k[j], k_a=rwkv_k_a[j], r_k=rwkv_r_k[j].reshape(D),
                g1=_lora_in(rwkv_g1[j][None]), g2=_lora_out(rwkv_g2[j][None]),
                gn_w=rwkv_gn_w[j], gn_b=rwkv_gn_b[j],
                v_res=None if j == 0 else (rwkv_v0[j - 1], _lora_in(rwkv_v1[j - 1][None]), _lora_out(rwkv_v2[j - 1][None])))
            q_c = _rwkv_prep(xc, mod, grp_c, norm_pre[i, 1], p, vf_c, C)
            q_l = _rwkv_prep(xl, mod, grp_l, norm_pre[i, 1], p, vf_l, T)
            if j == 0:
                vf_l, vf_c = q_l["v"], q_c["v"]
            s0 = jnp.zeros((2, B, D // LANES, LANES, LANES), F32)
            y_c, s_c = _wkv_scan(*_scan_inputs(q_c, B, C, D), s0)
            y_l, _ = _wkv_scan(*_scan_inputs(q_l, B, T, D), s_c)
            w_o = rwkv_w_o[j].astype(BF16)
            xl = _rwkv_out(y_l.reshape(2, B * T, D), q_l, p, w_o, xl, mod, grp_l, norm_post[i, 1])
            if not last:
                xc = _rwkv_out(y_c.reshape(2, B * C, D), q_c, p, w_o, xc, mod, grp_c, norm_post[i, 1])
        else:
            w_qkv = nat_w_qkv[j].astype(BF16)
            w_o = nat_w_o[j].astype(BF16)
            qkv_l = _normmod_mm(xl, mod, grp_l, 1, norm_pre[i, 1], w_qkv, BF16).reshape(B, T, 3 * D)
            qkv_c = _normmod_mm(xc, mod, grp_c, 1, norm_pre[i, 1], w_qkv, BF16).reshape(B, C, 3 * D)
            a_l = _nat_attention(qkv_l, qkv_c, _nat_bias_table(nat_rpb[j])).reshape(B * T, D)
            xl = _outproj(a_l, w_o, xl, mod, grp_l, 1, norm_post[i, 1])
            if not last:
                a_c = _ctx_attention(qkv_c).reshape(B * C, D)
                xc = _outproj(a_c, w_o, xc, mod, grp_c, 1, norm_post[i, 1])
        xl = ffn(xl, grp=grp_l, slot=2, g_pre=norm_pre[i, 2], g_post=norm_post[i, 2], which=1)
        if not last:
            xc = ffn(xc, grp=grp_c, slot=2, g_pre=norm_pre[i, 2], g_post=norm_post[i, 2], which=1)
    return xl.reshape(B, T, D)
```

```python
import functools

import jax
import jax.numpy as jnp
from jax import lax
from jax.experimental import pallas as pl
from jax.experimental.pallas import tpu as pltpu

F32 = jnp.float32
BF16 = jnp.bfloat16

LANES = 128
SUBLANES = 8
ROW_SLAB = 32
HEAD = 64
N_MOD = 9
MACARON_WEIGHT = 0.5
RMS_EPS = 1e-6
RWKV_GN_EPS = 64e-5
L2_EPS_SQ = 1e-24
GRID_W = 64
WIN_H = 8
WIN_W = 16
MASK_VALUE = -1e30
VMEM_LIMIT = 56 * 1024 * 1024


def _cparams(sem):
    return pltpu.CompilerParams(dimension_semantics=sem, vmem_limit_bytes=VMEM_LIMIT)


def _dot(a, b):
    return jnp.dot(a, b, preferred_element_type=F32)


def _dot_nt(a, b):
    return lax.dot_general(a, b, (((1,), (1,)), ((), ())), preferred_element_type=F32)


def _dot_tn(a, b):
    return lax.dot_general(a, b, (((0,), (0,)), ((), ())), preferred_element_type=F32)


def _split3(x):
    hi = x.astype(BF16)
    r1 = x - hi.astype(F32)
    mid = r1.astype(BF16)
    lo = (r1 - mid.astype(F32)).astype(BF16)
    return hi, mid, lo


def _dot_exact_lhs01(m01, x):
    hi, mid, lo = _split3(x)
    return _dot(m01, hi) + _dot(m01, mid) + _dot(m01, lo)


def _dot_exact_rhs01(x, m01):
    n = x.shape[0]
    parts = _dot(jnp.concatenate(_split3(x), axis=0), m01)
    return parts[:n] + parts[n:2 * n] + parts[2 * n:]


def _head_ones(n):
    r = lax.broadcasted_iota(jnp.int32, (n, n), 0) // HEAD
    c = lax.broadcasted_iota(jnp.int32, (n, n), 1) // HEAD
    return (r == c).astype(BF16)


def _softplus(x):
    return jnp.maximum(x, 0.0) + jnp.log1p(jnp.exp(-jnp.abs(x)))


def _rms(x, g):
    return x * lax.rsqrt(jnp.mean(x * x, axis=-1, keepdims=True) + RMS_EPS) * g


def _modulated(x, mod_ref, gpre_ref, slot):
    shift = mod_ref[3 * slot:3 * slot + 1, :]
    scale = mod_ref[3 * slot + 1:3 * slot + 2, :]
    return _rms(x, gpre_ref[...]) * (1 + scale) + shift


def _tile(n, want, unit):
    if n <= want:
        return n
    t = (want // unit) * unit
    while n % t:
        t -= unit
    return t


def _row_slabs(rows, fn):
    slab = _tile(rows, ROW_SLAB, 2 * SUBLANES)

    def body(i, carry):
        fn(pl.ds(pl.multiple_of(i * slab, slab), slab))
        return carry

    lax.fori_loop(0, rows // slab, body, 0, unroll=min(8, rows // slab))


def _modulated_rows(x_ref, mod_ref, gpre_ref, slot, dst_ref):
    shift = mod_ref[3 * slot:3 * slot + 1, :]
    scale1 = 1 + mod_ref[3 * slot + 1:3 * slot + 2, :]
    g = gpre_ref[...]

    def slab(sl):
        dst_ref[sl, :] = (_rms(x_ref[sl, :], g) * scale1 + shift).astype(dst_ref.dtype)

    _row_slabs(x_ref.shape[0], slab)


def _gated_residual_rows(x_ref, y_ref, wgate, gpost_ref, o_ref):
    g = gpost_ref[...]
    rows = x_ref.shape[0]
    slab = _tile(rows, ROW_SLAB, 2 * SUBLANES)
    for i in range(rows // slab):
        sl = slice(i * slab, (i + 1) * slab)
        o_ref[sl, :] = x_ref[sl, :] + wgate * _rms(y_ref[sl, :], g)


def _mod_spec(base, rows_per_group, tm):
    assert rows_per_group % tm == 0
    return lambda i, *_: (base + (i * tm) // rows_per_group, 0, 0)


def _wkv_kernel(r_ref, v_ref, kk_ref, lw_ref, kd_ref, a_ref, s0_ref, y_ref, sfin_ref, s_scr,
                *, chunk, pairs, nb):
    z = pl.program_id(0)
    c = pl.program_id(3)
    C = chunk
    side_by_side = lambda ref: jnp.concatenate([ref[b] for b in range(nb)], axis=1)

    @pl.when(c == 0)
    def _():
        for b in range(nb):
            s_scr[b * pairs:(b + 1) * pairs] = s0_ref[b]

    row = lax.broadcasted_iota(jnp.int32, (C, 2 * C), 0)
    pcol = lax.broadcasted_iota(jnp.int32, (C, 2 * C), 1)
    col = pcol % C
    d = (row - col) * (1 - 2 * z)
    strict = d > 0
    incl = d >= 0
    half_masks = (pcol < C, pcol >= C)
    lane = lax.broadcasted_iota(jnp.int32, (1, LANES), 1)
    head_masks = (lane < HEAD, lane >= HEAD)
    srow = lax.broadcasted_iota(jnp.int32, (LANES, LANES), 0)
    scol = lax.broadcasted_iota(jnp.int32, (LANES, LANES), 1)
    blockdiag = (srow < HEAD) == (scol < HEAD)
    eye = (row == col).astype(BF16)
    levels = []
    s = 1
    while s < C:
        levels.append((row // (2 * s) == col // (2 * s)) & (row // s != col // s))
        s *= 2

    def blockdiag2(m):
        zero = jnp.zeros_like(m)
        return jnp.concatenate([jnp.where(half_masks[0], m, zero), jnp.where(half_masks[1], m, zero)], axis=0)

    def by_head(x):
        zero = jnp.zeros_like(x)
        return jnp.concatenate([jnp.where(head_masks[0], x, zero), jnp.where(head_masks[1], x, zero)], axis=0)

    lw = side_by_side(lw_ref)
    kk = side_by_side(kk_ref)
    kd = side_by_side(kd_ref)
    L = _dot_exact_lhs01(incl[:, :C].astype(BF16), lw)
    l_end = jnp.sum(lw, axis=0, keepdims=True)
    lm = 0.5 * l_end
    e_m = jnp.exp(-lm)
    e_p = jnp.exp(lm)
    e_sh = jnp.exp(lm - L)
    at_t = -kk * jnp.exp(L - lw)
    rt_t = side_by_side(r_ref) * jnp.exp(L)
    sr_all = jnp.concatenate([at_t, rt_t], axis=0).astype(BF16)
    at_sh = at_t * e_m
    rt_sh = rt_t * e_m
    bt = (kk * side_by_side(a_ref)) * e_sh
    kt = kd * e_sh
    bk_all = jnp.concatenate([bt, kt], axis=0).astype(BF16)
    bkh_all = jnp.concatenate([bt * e_p, kt * e_p], axis=0).astype(BF16)
    v_all = side_by_side(v_ref).astype(BF16)
    s_decay = jnp.exp(l_end)

    prs = range(nb * pairs)
    lanes = [slice(p * LANES, (p + 1) * LANES) for p in prs]
    sr = [_dot_nt(sr_all[:, lanes[p]], s_scr[p].astype(BF16)) for p in prs]
    v_h = [by_head(v_all[:, lanes[p]]) for p in prs]
    quads = [_dot_nt(jnp.concatenate([at_sh[:, lanes[p]], rt_sh[:, lanes[p]]], axis=0).astype(BF16),
                     jnp.concatenate([by_head(bk_all[:C, lanes[p]]), by_head(bk_all[C:, lanes[p]])], axis=0))
             for p in prs]
    a_ab = [jnp.where(strict, q[:C, :2 * C], 0.0).astype(BF16) for q in quads]
    a_ak = [jnp.where(strict, q[:C, 2 * C:], 0.0).astype(BF16) for q in quads]
    r_bk = [jnp.concatenate([jnp.where(incl, q[C:, :2 * C], 0.0), jnp.where(incl, q[C:, 2 * C:], 0.0)],
                            axis=1).astype(BF16) for q in quads]
    t = [eye + jnp.where(levels[0], a, jnp.zeros_like(a)) for a in a_ab]
    for lv in levels[1:]:
        x = [jnp.where(lv, a, jnp.zeros_like(a)) for a in a_ab]
        m1 = [_dot(xi, blockdiag2(ti)).astype(BF16) for xi, ti in zip(x, t)]
        m2 = [_dot(ti, blockdiag2(mi)).astype(BF16) for ti, mi in zip(t, m1)]
        t = [ti + mi for ti, mi in zip(t, m2)]
    rhs = [(sr[p][:C] + _dot(a_ak[p], v_h[p])).astype(BF16) for p in prs]
    u = [_dot(t[p], by_head(rhs[p])).astype(BF16) for p in prs]
    y = [sr[p][C:] + _dot(r_bk[p], jnp.concatenate([by_head(u[p]), v_h[p]], axis=0)) for p in prs]
    for p in prs:
        y_ref[p // pairs, :, lanes[p % pairs]] = y[p]
        upd = _dot_tn(jnp.concatenate([u[p], v_all[:, lanes[p]]], axis=0), bkh_all[:, lanes[p]])
        s_scr[p] = s_scr[p] * s_decay[:, lanes[p]] + jnp.where(blockdiag, upd, 0.0)

    @pl.when(c == pl.num_programs(3) - 1)
    def _():
        for b in range(nb):
            sfin_ref[b] = s_scr[b * pairs:(b + 1) * pairs]


def _wkv_scan(r, v, kk, lw, kd, a, s0, *, chunk=64, pairs=16, nb=2):
    B, T, D = r.shape
    pairs = min(pairs, D // LANES)
    nb = nb if B % nb == 0 else 1
    lw_lanes = pairs * LANES
    assert T % chunk == 0 and D % lw_lanes == 0
    nc = T // chunk
    ng = D // lw_lanes

    def cidx(z, c):
        return c + z * (nc - 1 - 2 * c)

    tok_spec = pl.BlockSpec((nb, chunk, lw_lanes), lambda z, b, g, c: (b, cidx(z, c), g))
    dir_spec = pl.BlockSpec((None, nb, chunk, lw_lanes), lambda z, b, g, c: (z, b, cidx(z, c), g))
    st_spec = pl.BlockSpec((None, nb, pairs, LANES, LANES), lambda z, b, g, c: (z, b, g, 0, 0))
    y, s_fin = pl.pallas_call(
        functools.partial(_wkv_kernel, chunk=chunk, pairs=pairs, nb=nb),
        grid=(2, B // nb, ng, nc),
        in_specs=[tok_spec, tok_spec, tok_spec, dir_spec, dir_spec, dir_spec, st_spec],
        out_specs=[dir_spec, st_spec],
        out_shape=[jax.ShapeDtypeStruct((2, B, T, D), F32),
                   jax.ShapeDtypeStruct(s0.shape, F32)],
        scratch_shapes=[pltpu.VMEM((nb * pairs, LANES, LANES), F32)],
        compiler_params=_cparams(("arbitrary", "arbitrary", "arbitrary", "arbitrary")),
        name="wkv_scan",
    )(r, v, kk, lw, kd, a, s0)
    return y, s_fin


def _rwkv_prep_kernel(*refs, seq_len, tm, has_vres):
    (x_ref, xp_ref, xn_ref, mod_ref, gpre_ref, mix_ref, wr_ref, wk_ref, wv_ref,
     w1_ref, w2_ref, w0_ref, a1_ref, a2_ref, a0_ref, g1_ref, g2_ref, kk_ref, ka_ref) = refs[:19]
    pos = 19
    if has_vres:
        v1_ref, v2_ref, v0_ref, vf_ref = refs[pos:pos + 4]
        pos += 4
    r_ref, v_ref, kkn_ref, g_ref, lw_ref, kd_ref, a_ref = refs[pos:pos + 7]
    pos += 7
    xmix_scr, tw_scr, ta_scr, tg_scr = refs[pos:pos + 4]
    tv_scr = refs[pos + 4] if has_vres else None
    i = pl.program_id(0)
    n = pl.program_id(1)

    @pl.when(n == 0)
    def _():
        h = _modulated(x_ref[...], mod_ref, gpre_ref, 1)
        t0 = i * tm
        hp = _modulated(xp_ref[...], mod_ref, gpre_ref, 1)[SUBLANES - 1:SUBLANES, :]
        hn = _modulated(xn_ref[...], mod_ref, gpre_ref, 1)[0:1, :]
        hp = jnp.where(t0 % seq_len == 0, 0.0, hp)
        hn = jnp.where((t0 + tm) % seq_len == 0, 0.0, hn)
        rowid = lax.broadcasted_iota(jnp.int32, (tm, 1), 0)
        prev = jnp.where(rowid == 0, hp, pltpu.roll(h, 1, 0))
        nxt = jnp.where(rowid == tm - 1, hn, pltpu.roll(h, tm - 1, 0))
        xx = 0.5 * (prev + nxt) - h
        mixed = lambda m: (h + xx * mix_ref[m:m + 1, :]).astype(BF16)
        xmix_scr[0] = mixed(0)
        xmix_scr[1] = mixed(2)
        xv = mixed(3)
        xmix_scr[2] = xv
        tw_scr[...] = jnp.tanh(_dot(mixed(1), w1_ref[...])).astype(BF16)
        ta_scr[...] = _dot(mixed(4), a1_ref[...]).astype(BF16)
        tg_scr[...] = jax.nn.sigmoid(_dot(mixed(5), g1_ref[...])).astype(BF16)
        if has_vres:
            tv_scr[...] = _dot(xv, v1_ref[...]).astype(BF16)

    r_ref[...] = _dot(xmix_scr[0], wr_ref[...])
    k = _dot(xmix_scr[1], wk_ref[...])
    v = _dot(xmix_scr[2], wv_ref[...])
    if has_vres:
        v = v + (vf_ref[...] - v) * jax.nn.sigmoid(v0_ref[...] + _dot(tv_scr[...], v2_ref[...]))
    v_ref[...] = v
    g_ref[...] = _dot(tg_scr[...], g2_ref[...])
    kk = k * kk_ref[...]
    ssq = _dot_exact_rhs01(kk * kk, _head_ones(kk.shape[1]))
    kkn_ref[...] = kk * lax.rsqrt(jnp.maximum(ssq, L2_EPS_SQ))
    lp = w2_ref.shape[1]
    la = a2_ref.shape[1]
    for zz in range(2):
        lora_w = _dot(tw_scr[:, zz * lp:(zz + 1) * lp], w2_ref[zz])
        w_log = -_softplus(-(w0_ref[zz:zz + 1, :] + lora_w)) - 0.5
        lw_ref[zz] = -jnp.exp(w_log)
        a = jax.nn.sigmoid(a0_ref[zz:zz + 1, :] + _dot(ta_scr[:, zz * la:(zz + 1) * la], a2_ref[zz]))
        a_ref[zz] = a
        kd_ref[zz] = k * (1 + (a - 1) * ka_ref[...])


def _rwkv_prep(x, mod, grp, g_pre, p, v_first, seq_len, *, tm=512, tn=256):
    M, D = x.shape
    tm = _tile(seq_len, tm, SUBLANES)
    tn = _tile(D, tn, LANES)
    base, rpg = grp
    wtile = pl.BlockSpec((D, tn), lambda i, n: (0, n))
    has_vres = p["v_res"] is not None
    nb = M // SUBLANES
    row = lambda i, n: (i, 0)
    col = lambda i, n: (0, n)
    full = lambda i, n: (0, 0)
    col3 = lambda i, n: (0, 0, n)
    tile = lambda i, n: (i, n)
    tile3 = lambda i, n: (0, i, n)
    lw1, la1, lg1 = p["w1"].shape[1], p["a1"].shape[1], p["g1"].shape[1]
    in_specs = [
        pl.BlockSpec((tm, D), row),
        pl.BlockSpec((SUBLANES, D), lambda i, n: (jnp.maximum(i * (tm // SUBLANES) - 1, 0), 0)),
        pl.BlockSpec((SUBLANES, D), lambda i, n: (jnp.minimum((i + 1) * (tm // SUBLANES), nb - 1), 0)),
        pl.BlockSpec((None, N_MOD, D), _mod_spec(base, rpg, tm)),
        pl.BlockSpec((1, D), full),
        pl.BlockSpec((6, D), full),
        wtile, wtile, wtile,
        pl.BlockSpec((D, lw1), full), pl.BlockSpec((2, lw1 // 2, tn), col3), pl.BlockSpec((2, tn), col),
        pl.BlockSpec((D, la1), full), pl.BlockSpec((2, la1 // 2, tn), col3), pl.BlockSpec((2, tn), col),
        pl.BlockSpec((D, lg1), full), pl.BlockSpec((lg1, tn), col),
        pl.BlockSpec((1, tn), col), pl.BlockSpec((1, tn), col),
    ]
    args = [x, x, x, mod, g_pre.reshape(1, D), p["mix"], p["w_r"], p["w_k"], p["w_v"],
            p["w1"], p["w2"], p["w0"], p["a1"], p["a2"], p["a0"], p["g1"], p["g2"][0],
            p["k_k"].reshape(1, D), p["k_a"].reshape(1, D)]
    scratch = [pltpu.VMEM((3, tm, D), BF16), pltpu.VMEM((tm, lw1), BF16),
               pltpu.VMEM((tm, la1), BF16), pltpu.VMEM((tm, lg1), BF16)]
    if has_vres:
        v0, v1, v2 = p["v_res"]
        lv1 = v1.shape[1]
        in_specs += [pl.BlockSpec((D, lv1), full), pl.BlockSpec((lv1, tn), col),
                     pl.BlockSpec((1, tn), col), pl.BlockSpec((tm, tn), tile)]
        args += [v1, v2[0], v0.reshape(1, D), v_first]
        scratch.append(pltpu.VMEM((tm, lv1), BF16))
    one = jax.ShapeDtypeStruct((M, D), F32)
    two = jax.ShapeDtypeStruct((2, M, D), F32)
    r, v, kk, g, lw, kd, a = pl.pallas_call(
        functools.partial(_rwkv_prep_kernel, seq_len=seq_len, tm=tm, has_vres=has_vres),
        grid=(M // tm, D // tn),
        in_specs=in_specs,
        out_specs=[pl.BlockSpec((tm, tn), tile)] * 4 + [pl.BlockSpec((2, tm, tn), tile3)] * 3,
        out_shape=[one] * 4 + [two] * 3,
        scratch_shapes=scratch,
        compiler_params=_cparams(("arbitrary", "arbitrary")),
        name="rwkv_prep",
    )(*args)
    return dict(r=r, v=v, kk=kk, g=g, lw=lw, kd=kd, a=a)


def _rwkv_out_kernel(y_ref, r_ref, kd_ref, v_ref, g_ref, rk_ref, gnw_ref, gnb_ref, w_ref, x_ref, mod_ref,
                     gpost_ref, o_ref, pre_scr, *, slab):
    D = x_ref.shape[1]
    ones = _head_ones(slab)
    for j in range(D // slab):
        sl = slice(j * slab, (j + 1) * slab)
        wkv = y_ref[0, :, sl] + y_ref[1, :, sl]
        mu = _dot_exact_rhs01(wkv, ones) * (1.0 / HEAD)
        cen = wkv - mu
        var = _dot_exact_rhs01(cen * cen, ones) * (1.0 / HEAD)
        o = cen * lax.rsqrt(var + RWKV_GN_EPS) * gnw_ref[:, sl] + gnb_ref[:, sl]
        coef = _dot_exact_rhs01(r_ref[:, sl] * (kd_ref[0, :, sl] + kd_ref[1, :, sl]) * rk_ref[:, sl], ones)
        pre_scr[:, sl] = ((o + coef * v_ref[:, sl]) * g_ref[:, sl]).astype(BF16)
    y = _dot(pre_scr[...], w_ref[...])
    o_ref[...] = x_ref[...] + mod_ref[5:6, :] * _rms(y, gpost_ref[...])


def _rwkv_out(y, q, p, w_o, x, mod, grp, g_post, *, tm=256, slab=256):
    M, D = x.shape
    tm = _tile(M, tm, SUBLANES)
    slab = _tile(D, slab, LANES)
    base, rpg = grp
    row = lambda i: (i, 0)
    row3 = lambda i: (0, i, 0)
    full = lambda i: (0, 0)
    return pl.pallas_call(
        functools.partial(_rwkv_out_kernel, slab=slab),
        grid=(M // tm,),
        in_specs=[pl.BlockSpec((2, tm, D), row3), pl.BlockSpec((tm, D), row), pl.BlockSpec((2, tm, D), row3),
                  pl.BlockSpec((tm, D), row), pl.BlockSpec((tm, D), row),
                  pl.BlockSpec((1, D), full), pl.BlockSpec((1, D), full), pl.BlockSpec((1, D), full),
                  pl.BlockSpec((D, D), full, pipeline_mode=pl.Buffered(1)), pl.BlockSpec((tm, D), row),
                  pl.BlockSpec((None, N_MOD, D), _mod_spec(base, rpg, tm)), pl.BlockSpec((1, D), full)],
        out_specs=pl.BlockSpec((tm, D), row),
        out_shape=jax.ShapeDtypeStruct((M, D), F32),
        scratch_shapes=[pltpu.VMEM((tm, D), BF16)],
        compiler_params=_cparams(("arbitrary",)),
        name="rwkv_out",
    )(y, q["r"], q["kd"], q["v"], q["g"], p["r_k"].reshape(1, D), p["gn_w"].reshape(1, D),
      p["gn_b"].reshape(1, D), w_o, x, mod, g_post.reshape(1, D))


def _adaln_kernel(c_ref, w_ref, b_ref, o_ref):
    cv = c_ref[...]
    cv = cv * jax.nn.sigmoid(cv)
    o_ref[...] = _dot(cv.astype(BF16), w_ref[...].astype(BF16)) + b_ref[...]


def _adaln(cvec, ada_w, ada_b, *, bn=2048):
    depth, D, N = ada_w.shape
    bn = _tile(N, bn, LANES)
    return pl.pallas_call(
        _adaln_kernel,
        grid=(depth, N // bn),
        in_specs=[pl.BlockSpec((8, D), lambda i, n: (0, 0)),
                  pl.BlockSpec((None, D, bn), lambda i, n: (i, 0, n)),
                  pl.BlockSpec((None, 1, bn), lambda i, n: (i, 0, n))],
        out_specs=pl.BlockSpec((None, 8, bn), lambda i, n: (i, 0, n)),
        out_shape=jax.ShapeDtypeStruct((depth, 8, N), F32),
        compiler_params=_cparams(("arbitrary", "arbitrary")),
        name="adaln",
    )(cvec, ada_w, ada_b.reshape(depth, 1, N))


def _ffn_kernel(x_ref, mod_ref, gpre_ref, gpost_ref, wg_ref, wu_ref, wd_ref, o_ref, *rest, slot, emit_bf16):
    h_scr = rest[-1]
    f = pl.program_id(1)

    @pl.when(f == 0)
    def _():
        _modulated_rows(x_ref, mod_ref, gpre_ref, slot, h_scr)
        o_ref[...] = jnp.zeros_like(o_ref)

    wg, wu, wd = wg_ref[...], wu_ref[...], wd_ref[...]
    if emit_bf16:
        wg, wu, wd = wg.astype(BF16), wu.astype(BF16), wd.astype(BF16)
        for ref, w in zip(rest[:3], (wg, wu, wd)):
            ref[...] = w
    h = h_scr[...]
    g = _dot(h, wg)
    u = _dot(h, wu)
    a = (g * jax.nn.sigmoid(g)) * u
    o_ref[...] += _dot(a.astype(BF16), wd)

    @pl.when(f == pl.num_programs(1) - 1)
    def _():
        gate = mod_ref[3 * slot + 2:3 * slot + 3, :]
        _gated_residual_rows(x_ref, o_ref, MACARON_WEIGHT * gate, gpost_ref, o_ref)


def _ffn(x, mod, grp, slot, g_pre, g_post, w_in, w_up, w_down, layer=None, which=None, *, tm=1024, tf=512):
    M, D = x.shape
    F = w_in.shape[-1]
    emit_bf16 = layer is not None
    tm = _tile(min(M, grp[1]), tm, SUBLANES)
    tf = _tile(F, tf // 2 if emit_bf16 else tf, LANES)
    base, rpg = grp
    if emit_bf16:
        assert M == tm
        up_spec = pl.BlockSpec((None, None, D, tf), lambda i, f: (layer, which, 0, f))
        down_spec = pl.BlockSpec((None, None, tf, D), lambda i, f: (layer, which, f, 0))
    else:
        up_spec = pl.BlockSpec((D, tf), lambda i, f: (0, f))
        down_spec = pl.BlockSpec((tf, D), lambda i, f: (f, 0))
    x_spec = pl.BlockSpec((tm, D), lambda i, f: (i, 0))
    out_specs = [x_spec]
    out_shape = [jax.ShapeDtypeStruct((M, D), F32)]
    if emit_bf16:
        x_spec = pl.BlockSpec((tm, D), lambda i, f: (i, 0), pipeline_mode=pl.Buffered(1))
        out_specs += [pl.BlockSpec((D, tf), lambda i, f: (0, f)), pl.BlockSpec((D, tf), lambda i, f: (0, f)),
                      pl.BlockSpec((tf, D), lambda i, f: (f, 0))]
        out_shape += [jax.ShapeDtypeStruct((D, F), BF16), jax.ShapeDtypeStruct((D, F), BF16),
                      jax.ShapeDtypeStruct((F, D), BF16)]
    out = pl.pallas_call(
        functools.partial(_ffn_kernel, slot=slot, emit_bf16=emit_bf16),
        grid=(M // tm, F // tf),
        in_specs=[x_spec,
                  pl.BlockSpec((None, N_MOD, D), _mod_spec(base, rpg, tm)),
                  pl.BlockSpec((1, D), lambda i, f: (0, 0)),
                  pl.BlockSpec((1, D), lambda i, f: (0, 0)),
                  up_spec, up_spec, down_spec],
        out_specs=out_specs,
        out_shape=out_shape,
        scratch_shapes=[pltpu.VMEM((tm, D), BF16)],
        compiler_params=_cparams(("arbitrary", "arbitrary")),
        name="ffn_cast" if emit_bf16 else "ffn",
    )(x, mod, g_pre.reshape(1, D), g_post.reshape(1, D), w_in, w_up, w_down)
    return (out[0], tuple(out[1:])) if emit_bf16 else out[0]


def _normmod_mm_kernel(x_ref, mod_ref, gpre_ref, w_ref, o_ref, h_scr, *, slot):
    @pl.when(pl.program_id(1) == 0)
    def _():
        _modulated_rows(x_ref, mod_ref, gpre_ref, slot, h_scr)

    o_ref[...] = _dot(h_scr[...], w_ref[...]).astype(o_ref.dtype)


def _normmod_mm(x, mod, grp, slot, g_pre, w, out_dtype, *, tm=1024, tn=1024):
    M, D = x.shape
    N = w.shape[1]
    tm = _tile(min(M, grp[1]), tm, SUBLANES)
    tn = _tile(N, tn, LANES)
    base, rpg = grp
    return pl.pallas_call(
        functools.partial(_normmod_mm_kernel, slot=slot),
        grid=(M // tm, N // tn),
        in_specs=[pl.BlockSpec((tm, D), lambda i, n: (i, 0)),
                  pl.BlockSpec((None, N_MOD, D), _mod_spec(base, rpg, tm)),
                  pl.BlockSpec((1, D), lambda i, n: (0, 0)),
                  pl.BlockSpec((D, tn), lambda i, n: (0, n))],
        out_specs=pl.BlockSpec((tm, tn), lambda i, n: (i, n)),
        out_shape=jax.ShapeDtypeStruct((M, N), out_dtype),
        scratch_shapes=[pltpu.VMEM((tm, D), BF16)],
        compiler_params=_cparams(("arbitrary", "arbitrary")),
        name="normmod_mm",
    )(x, mod, g_pre.reshape(1, D), w)


def _outproj_kernel(a_ref, w_ref, x_ref, mod_ref, gpost_ref, o_ref, *, slot):
    y = _dot(a_ref[...], w_ref[...])
    gate = mod_ref[3 * slot + 2:3 * slot + 3, :]
    o_ref[...] = x_ref[...] + gate * _rms(y, gpost_ref[...])


def _outproj(a, w, x, mod, grp, slot, g_post, *, tm=512):
    M, D = x.shape
    K = a.shape[1]
    tm = _tile(M, tm, SUBLANES)
    base, rpg = grp
    return pl.pallas_call(
        functools.partial(_outproj_kernel, slot=slot),
        grid=(M // tm,),
        in_specs=[pl.BlockSpec((tm, K), lambda i: (i, 0)),
                  pl.BlockSpec((K, D), lambda i: (0, 0), pipeline_mode=pl.Buffered(1)),
                  pl.BlockSpec((tm, D), lambda i: (i, 0)),
                  pl.BlockSpec((None, N_MOD, D), _mod_spec(base, rpg, tm)),
                  pl.BlockSpec((1, D), lambda i: (0, 0))],
        out_specs=pl.BlockSpec((tm, D), lambda i: (i, 0)),
        out_shape=jax.ShapeDtypeStruct((M, D), F32),
        compiler_params=_cparams(("arbitrary",)),
        name="outproj",
    )(a, w, x, mod, g_post.reshape(1, D))


def _nat_kernel(q_ref, k_ref, v_ref, kc_ref, vc_ref, bias_ref, o_ref, *, rows, group):
    scale = HEAD ** -0.5
    n_lat = WIN_H * GRID_W
    lane = lax.broadcasted_iota(jnp.int32, (1, LANES), 1)
    head_masks = (lane < HEAD, lane >= HEAD)
    qcol = lax.broadcasted_iota(jnp.int32, (2 * GRID_W, n_lat), 0) % GRID_W
    kcol = lax.broadcasted_iota(jnp.int32, (2 * GRID_W, n_lat), 1) % GRID_W
    cstart = jnp.clip(qcol - WIN_W // 2, 0, GRID_W - WIN_W)
    col_ok = (kcol >= cstart) & (kcol < cstart + WIN_W)
    kc = kc_ref[...]
    vc = vc_ref[...]
    zero = jnp.zeros((), BF16)

    def step(i, carry):
        chains = []
        for rr in range(group):
            r = i * group + rr
            r0 = jnp.clip(r - WIN_H // 2, 0, rows - WIN_H)
            q = q_ref[pl.ds(pl.multiple_of(r * GRID_W, GRID_W), GRID_W), :] * scale
            q2 = jnp.concatenate([jnp.where(head_masks[0], q, zero), jnp.where(head_masks[1], q, zero)], axis=0)
            kw = k_ref[pl.ds(pl.multiple_of(r0 * GRID_W, GRID_W), n_lat), :]
            vw = v_ref[pl.ds(pl.multiple_of(r0 * GRID_W, GRID_W), n_lat), :]
            chains.append((r, r0, q2, kw, vw))
        s = [_dot_nt(q2, kw) for (_, _, q2, kw, _) in chains]
        sc = [_dot_nt(q2, kc) for (_, _, q2, _, _) in chains]
        s = [jnp.where(col_ok, si + jnp.concatenate(
                [jnp.concatenate([bias_ref[h, 2 * jj - (r - r0) + WIN_H - 1] for jj in range(WIN_H // 2)], axis=1)
                 for h in range(2)], axis=0), MASK_VALUE)
             for si, (r, r0, _, _, _) in zip(s, chains)]
        m = [jnp.maximum(jnp.max(si, axis=-1, keepdims=True), jnp.max(ci, axis=-1, keepdims=True))
             for si, ci in zip(s, sc)]
        e = [jnp.exp(si - mi) for si, mi in zip(s, m)]
        ec = [jnp.exp(ci - mi) for ci, mi in zip(sc, m)]
        den = [jnp.sum(ei, axis=-1, keepdims=True) + jnp.sum(ci, axis=-1, keepdims=True) for ei, ci in zip(e, ec)]
        o = [(_dot(ei.astype(BF16), vw) + _dot(ci.astype(BF16), vc)) / di
             for ei, ci, di, (_, _, _, _, vw) in zip(e, ec, den, chains)]
        for rr in range(group):
            r = i * group + rr
            o_ref[pl.ds(pl.multiple_of(r * GRID_W, GRID_W), GRID_W), :] = jnp.where(
                head_masks[0], o[rr][:GRID_W], o[rr][GRID_W:]).astype(o_ref.dtype)
        return carry

    lax.fori_loop(0, rows // group, step, 0)


def _nat_bias_table(rpb):
    edge = GRID_W - WIN_W
    padded = jnp.pad(rpb, ((0, 0), (0, 0), (edge, edge)), mode="edge")
    rows = jnp.stack([padded[:, :, GRID_W - 1 - q:2 * GRID_W - 1 - q] for q in range(GRID_W)], axis=2)
    return jnp.concatenate([rows[:, :-1], rows[:, 1:]], axis=-1).astype(F32)


def _nat_attention(qkv, qkv_c, bias, *, group=4):
    B, T, D3 = qkv.shape
    D = D3 // 3
    C = qkv_c.shape[1]
    nd = D // LANES
    rows = T // GRID_W
    assert rows >= WIN_H and rows % group == 0
    return pl.pallas_call(
        functools.partial(_nat_kernel, rows=rows, group=group),
        grid=(B, nd),
        in_specs=[pl.BlockSpec((None, T, LANES), lambda b, p: (b, 0, p)),
                  pl.BlockSpec((None, T, LANES), lambda b, p: (b, 0, nd + p)),
                  pl.BlockSpec((None, T, LANES), lambda b, p: (b, 0, 2 * nd + p)),
                  pl.BlockSpec((None, C, LANES), lambda b, p: (b, 0, nd + p)),
                  pl.BlockSpec((None, C, LANES), lambda b, p: (b, 0, 2 * nd + p)),
                  pl.BlockSpec((2, 2 * WIN_H - 2, GRID_W, 2 * GRID_W), lambda b, p: (p, 0, 0, 0))],
        out_specs=pl.BlockSpec((None, T, LANES), lambda b, p: (b, 0, p)),
        out_shape=jax.ShapeDtypeStruct((B, T, D), BF16),
        compiler_params=_cparams(("arbitrary", "arbitrary")),
        name="nat_attention",
    )(qkv, qkv, qkv, qkv_c, qkv_c, bias)


def _ctx_attn_kernel(q_ref, k_ref, v_ref, o_ref):
    scale = HEAD ** -0.5
    lane = lax.broadcasted_iota(jnp.int32, (1, LANES), 1)
    head_masks = (lane < HEAD, lane >= HEAD)
    q = q_ref[...] * scale
    k = k_ref[...]
    v = v_ref[...]
    zero = jnp.zeros((), BF16)
    outs = []
    for h in range(2):
        s = _dot_nt(jnp.where(head_masks[h], q, zero), k)
        e = jnp.exp(s - jnp.max(s, axis=-1, keepdims=True))
        outs.append(_dot(e.astype(BF16), v) / jnp.sum(e, axis=-1, keepdims=True))
    o_ref[...] = jnp.where(head_masks[0], outs[0], outs[1]).astype(o_ref.dtype)


def _ctx_attention(qkv_c):
    B, C, D3 = qkv_c.shape
    D = D3 // 3
    nd = D // LANES
    return pl.pallas_call(
        _ctx_attn_kernel,
        grid=(B, nd),
        in_specs=[pl.BlockSpec((None, C, LANES), lambda b, p: (b, 0, p)),
                  pl.BlockSpec((None, C, LANES), lambda b, p: (b, 0, nd + p)),
                  pl.BlockSpec((None, C, LANES), lambda b, p: (b, 0, 2 * nd + p))],
        out_specs=pl.BlockSpec((None, C, LANES), lambda b, p: (b, 0, p)),
        out_shape=jax.ShapeDtypeStruct((B, C, D), BF16),
        compiler_params=_cparams(("arbitrary", "arbitrary")),
        name="ctx_attention",
    )(qkv_c, qkv_c, qkv_c)


def _pad_to(w, axis, mult=LANES):
    n = w.shape[axis]
    pad = (-n) % mult
    if pad == 0:
        return w
    widths = [(0, 0)] * w.ndim
    widths[axis] = (0, pad)
    return jnp.pad(w, widths)


def _lora_in(w):
    w = _pad_to(w, 2)
    return jnp.concatenate(list(w), axis=1).astype(BF16)


def _lora_out(w):
    return _pad_to(w, 1).astype(BF16)


def _scan_inputs(q, B, L, D):
    three = lambda t: t.reshape(B, L, D)
    four = lambda t: t.reshape(2, B, L, D)
    return three(q["r"]), three(q["v"]), three(q["kk"]), four(q["lw"]), four(q["kd"]), four(q["a"])


def kernel(x, c, ctx, c_ctx, ada_w, ada_b, norm_pre, norm_post, ffn_w_gate, ffn_w_up, ffn_w_down, rwkv_mix, rwkv_w_r, rwkv_w_k, rwkv_w_v, rwkv_w_o, rwkv_w0, rwkv_w1, rwkv_w2, rwkv_a0, rwkv_a1, rwkv_a2, rwkv_v0, rwkv_v1, rwkv_v2, rwkv_k_k, rwkv_k_a, rwkv_r_k, rwkv_g1, rwkv_g2, rwkv_gn_w, rwkv_gn_b, nat_w_qkv, nat_w_o, nat_rpb):
    B, T, D = x.shape
    C = ctx.shape[1]
    depth = ada_w.shape[0]
    assert B + 1 <= 8 and D % (2 * LANES) == 0
    cvec = jnp.zeros((8, D), F32).at[:B].set(c).at[B].set(c_ctx)
    mods = _adaln(cvec, ada_w, ada_b).reshape(depth, 8, N_MOD, D)
    ffn_w = (ffn_w_gate, ffn_w_up, ffn_w_down)
    grp_l, grp_c = (0, T), (B, B * C)
    xl = x.reshape(B * T, D)
    xc = ctx.reshape(B * C, D)
    vf_l = vf_c = None

    def ffn_both(xl, xc, mod, i, slot, which, with_ctx):
        args = (slot, norm_pre[i, slot], norm_post[i, slot])
        if with_ctx and B * C <= 1024:
            xc, w = _ffn(xc, mod, grp_c, *args, *ffn_w, layer=i, which=which)
        else:
            w = tuple(t[i, which].astype(BF16) for t in ffn_w)
            if with_ctx:
                xc = _ffn(xc, mod, grp_c, *args, *w)
        return _ffn(xl, mod, grp_l, *args, *w), xc

    for i in range(depth):
        last = i == depth - 1
        j = i // 2
        mod = mods[i]
        xl, xc = ffn_both(xl, xc, mod, i, 0, 0, True)
        if i % 2 == 0:
            p = dict(
                mix=rwkv_mix[j], w_r=rwkv_w_r[j].astype(BF16), w_k=rwkv_w_k[j].astype(BF16),
                w_v=rwkv_w_v[j].astype(BF16), w0=rwkv_w0[j], w1=_lora_in(rwkv_w1[j]), w2=_lora_out(rwkv_w2[j]),
                a0=rwkv_a0[j], a1=_lora_in(rwkv_a1[j]), a2=_lora_out(rwkv_a2[j]),
                k_k=rwkv_k_k[j], k_a=rwkv_k_a[j], r_k=rwkv_r_k[j].reshape(D),
                g1=_lora_in(rwkv_g1[j][None]), g2=_lora_out(rwkv_g2[j][None]),
                gn_w=rwkv_gn_w[j], gn_b=rwkv_gn_b[j],
                v_res=None if j == 0 else (rwkv_v0[j - 1], _lora_in(rwkv_v1[j - 1][None]), _lora_out(rwkv_v2[j - 1][None])))
            q_c = _rwkv_prep(xc, mod, grp_c, norm_pre[i, 1], p, vf_c, C)
            q_l = _rwkv_prep(xl, mod, grp_l, norm_pre[i, 1], p, vf_l, T)
            if j == 0:
                vf_l, vf_c = q_l["v"], q_c["v"]
            s0 = jnp.zeros((2, B, D // LANES, LANES, LANES), F32)
            y_c, s_c = _wkv_scan(*_scan_inputs(q_c, B, C, D), s0)
            y_l, _ = _wkv_scan(*_scan_inputs(q_l, B, T, D), s_c)
            w_o = rwkv_w_o[j].astype(BF16)
            xl = _rwkv_out(y_l.reshape(2, B * T, D), q_l, p, w_o, xl, mod, grp_l, norm_post[i, 1])
            if not last:
                xc = _rwkv_out(y_c.reshape(2, B * C, D), q_c, p, w_o, xc, mod, grp_c, norm_post[i, 1])
        else:
            w_qkv = nat_w_qkv[j].astype(BF16)
            w_o = nat_w_o[j].astype(BF16)
            qkv_l = _normmod_mm(xl, mod, grp_l, 1, norm_pre[i, 1], w_qkv, BF16).reshape(B, T, 3 * D)
            qkv_c = _normmod_mm(xc, mod, grp_c, 1, norm_pre[i, 1], w_qkv, BF16).reshape(B, C, 3 * D)
            a_l = _nat_attention(qkv_l, qkv_c, _nat_bias_table(nat_rpb[j])).reshape(B * T, D)
            xl = _outproj(a_l, w_o, xl, mod, grp_l, 1, norm_post[i, 1])
            if not last:
                a_c = _ctx_attention(qkv_c).reshape(B * C, D)
                xc = _outproj(a_c, w_o, xc, mod, grp_c, 1, norm_post[i, 1])
        xl, xc = ffn_both(xl, xc, mod, i, 2, 1, not last)
    return xl.reshape(B, T, D)
```

```python
import functools
import math

import jax
import jax.numpy as jnp
from jax import lax
from jax.experimental import pallas as pl
from jax.experimental.pallas import tpu as pltpu

F32 = jnp.float32
BF16 = jnp.bfloat16

LANES = 128
SUBLANES = 8
ROW_SLAB = 32
HEAD = 64
N_MOD = 9
MACARON_WEIGHT = 0.5
RMS_EPS = 1e-6
RWKV_GN_EPS = 64e-5
L2_EPS_SQ = 1e-24
GRID_W = 64
WIN_H = 8
WIN_W = 16
MASK_VALUE = -1e30
EXP_NEG_HALF = math.exp(-0.5)
VMEM_LIMIT = 56 * 1024 * 1024


def _cparams(sem):
    return pltpu.CompilerParams(dimension_semantics=sem, vmem_limit_bytes=VMEM_LIMIT)


def _dot(a, b):
    return jnp.dot(a, b, preferred_element_type=F32)


def _dot_nt(a, b):
    return lax.dot_general(a, b, (((1,), (1,)), ((), ())), preferred_element_type=F32)


def _dot_tn(a, b):
    return lax.dot_general(a, b, (((0,), (0,)), ((), ())), preferred_element_type=F32)


def _split3(x):
    hi = x.astype(BF16)
    r1 = x - hi.astype(F32)
    mid = r1.astype(BF16)
    lo = (r1 - mid.astype(F32)).astype(BF16)
    return hi, mid, lo


def _dot_exact_lhs01(m01, x):
    hi, mid, lo = _split3(x)
    return _dot(m01, hi) + _dot(m01, mid) + _dot(m01, lo)


def _dot_exact_rhs01(x, m01):
    n = x.shape[0]
    parts = _dot(jnp.concatenate(_split3(x), axis=0), m01)
    return parts[:n] + parts[n:2 * n] + parts[2 * n:]


def _head_ones(n):
    r = lax.broadcasted_iota(jnp.int32, (n, n), 0) // HEAD
    c = lax.broadcasted_iota(jnp.int32, (n, n), 1) // HEAD
    return (r == c).astype(BF16)


def _rms(x, g):
    return x * lax.rsqrt(jnp.mean(x * x, axis=-1, keepdims=True) + RMS_EPS) * g


def _modulated(x, mod_ref, gpre_ref, slot):
    shift = mod_ref[3 * slot:3 * slot + 1, :]
    scale = mod_ref[3 * slot + 1:3 * slot + 2, :]
    return _rms(x, gpre_ref[...]) * (1 + scale) + shift


def _tile(n, want, unit):
    if n <= want:
        return n
    t = (want // unit) * unit
    while n % t:
        t -= unit
    return t


def _row_slabs(rows, fn):
    slab = _tile(rows, ROW_SLAB, 2 * SUBLANES)

    def body(i, carry):
        fn(pl.ds(pl.multiple_of(i * slab, slab), slab))
        return carry

    lax.fori_loop(0, rows // slab, body, 0, unroll=min(8, rows // slab))


def _modulated_rows(x_ref, mod_ref, gpre_ref, slot, dst_ref):
    shift = mod_ref[3 * slot:3 * slot + 1, :]
    scale1 = 1 + mod_ref[3 * slot + 1:3 * slot + 2, :]
    g = gpre_ref[...]

    def slab(sl):
        dst_ref[sl, :] = (_rms(x_ref[sl, :], g) * scale1 + shift).astype(dst_ref.dtype)

    _row_slabs(x_ref.shape[0], slab)


def _gated_residual_rows(x_ref, y_ref, wgate, gpost_ref, o_ref):
    g = gpost_ref[...]
    rows = x_ref.shape[0]
    slab = _tile(rows, ROW_SLAB, 2 * SUBLANES)
    for i in range(rows // slab):
        sl = slice(i * slab, (i + 1) * slab)
        o_ref[sl, :] = x_ref[sl, :] + wgate * _rms(y_ref[sl, :], g)


def _mod_spec(base, rows_per_group, tm):
    assert rows_per_group % tm == 0
    return lambda i, *_: (base + (i * tm) // rows_per_group, 0, 0)


def _wkv_kernel(r_ref, v_ref, kk_ref, lw_ref, kd_ref, a_ref, s0_ref, y_ref, sfin_ref, s_scr,
                *, chunk, pairs, nb):
    z = pl.program_id(0)
    c = pl.program_id(3)
    C = chunk
    side_by_side = lambda ref: jnp.concatenate([ref[b] for b in range(nb)], axis=1)

    @pl.when(c == 0)
    def _():
        for b in range(nb):
            s_scr[b * pairs:(b + 1) * pairs] = s0_ref[b]

    row = lax.broadcasted_iota(jnp.int32, (C, 2 * C), 0)
    pcol = lax.broadcasted_iota(jnp.int32, (C, 2 * C), 1)
    col = pcol % C
    d = (row - col) * (1 - 2 * z)
    strict = d > 0
    incl = d >= 0
    half_masks = (pcol < C, pcol >= C)
    lane = lax.broadcasted_iota(jnp.int32, (1, LANES), 1)
    head_masks = (lane < HEAD, lane >= HEAD)
    srow = lax.broadcasted_iota(jnp.int32, (LANES, LANES), 0)
    scol = lax.broadcasted_iota(jnp.int32, (LANES, LANES), 1)
    blockdiag = (srow < HEAD) == (scol < HEAD)
    eye = (row == col).astype(BF16)
    levels = []
    s = 1
    while s < C:
        levels.append((row // (2 * s) == col // (2 * s)) & (row // s != col // s))
        s *= 2

    def blockdiag2(m):
        zero = jnp.zeros_like(m)
        return jnp.concatenate([jnp.where(half_masks[0], m, zero), jnp.where(half_masks[1], m, zero)], axis=0)

    def by_head(x):
        zero = jnp.zeros_like(x)
        return jnp.concatenate([jnp.where(head_masks[0], x, zero), jnp.where(head_masks[1], x, zero)], axis=0)

    lw = side_by_side(lw_ref)
    kk = side_by_side(kk_ref)
    kd = side_by_side(kd_ref)
    L = _dot_exact_lhs01(incl[:, :C].astype(BF16), lw)
    l_end = jnp.sum(lw, axis=0, keepdims=True)
    lm = 0.5 * l_end
    e_m = jnp.exp(-lm)
    e_p = jnp.exp(lm)
    e_sh = jnp.exp(lm - L)
    at_t = -kk * jnp.exp(L - lw)
    rt_t = side_by_side(r_ref) * jnp.exp(L)
    sr_all = jnp.concatenate([at_t, rt_t], axis=0).astype(BF16)
    at_sh = at_t * e_m
    rt_sh = rt_t * e_m
    bt = (kk * side_by_side(a_ref)) * e_sh
    kt = kd * e_sh
    bk_all = jnp.concatenate([bt, kt], axis=0).astype(BF16)
    bkh_all = jnp.concatenate([bt * e_p, kt * e_p], axis=0).astype(BF16)
    v_all = side_by_side(v_ref).astype(BF16)
    s_decay = jnp.exp(l_end)

    prs = range(nb * pairs)
    lanes = [slice(p * LANES, (p + 1) * LANES) for p in prs]
    sr = [_dot_nt(sr_all[:, lanes[p]], s_scr[p].astype(BF16)) for p in prs]
    v_h = [by_head(v_all[:, lanes[p]]) for p in prs]
    quads = [_dot_nt(jnp.concatenate([at_sh[:, lanes[p]], rt_sh[:, lanes[p]]], axis=0).astype(BF16),
                     jnp.concatenate([by_head(bk_all[:C, lanes[p]]), by_head(bk_all[C:, lanes[p]])], axis=0))
             for p in prs]
    a_ab = [jnp.where(strict, q[:C, :2 * C], 0.0).astype(BF16) for q in quads]
    a_ak = [jnp.where(strict, q[:C, 2 * C:], 0.0).astype(BF16) for q in quads]
    r_bk = [jnp.concatenate([jnp.where(incl, q[C:, :2 * C], 0.0), jnp.where(incl, q[C:, 2 * C:], 0.0)],
                            axis=1).astype(BF16) for q in quads]
    t = [eye + jnp.where(levels[0], a, jnp.zeros_like(a)) for a in a_ab]
    for lv in levels[1:]:
        x = [jnp.where(lv, a, jnp.zeros_like(a)) for a in a_ab]
        m1 = [_dot(xi, blockdiag2(ti)).astype(BF16) for xi, ti in zip(x, t)]
        m2 = [_dot(ti, blockdiag2(mi)).astype(BF16) for ti, mi in zip(t, m1)]
        t = [ti + mi for ti, mi in zip(t, m2)]
    rhs = [(sr[p][:C] + _dot(a_ak[p], v_h[p])).astype(BF16) for p in prs]
    u = [_dot(t[p], by_head(rhs[p])).astype(BF16) for p in prs]
    y = [sr[p][C:] + _dot(r_bk[p], jnp.concatenate([by_head(u[p]), v_h[p]], axis=0)) for p in prs]
    for p in prs:
        y_ref[p // pairs, :, lanes[p % pairs]] = y[p]
        upd = _dot_tn(jnp.concatenate([u[p], v_all[:, lanes[p]]], axis=0), bkh_all[:, lanes[p]])
        s_scr[p] = s_scr[p] * s_decay[:, lanes[p]] + jnp.where(blockdiag, upd, 0.0)

    @pl.when(c == pl.num_programs(3) - 1)
    def _():
        for b in range(nb):
            sfin_ref[b] = s_scr[b * pairs:(b + 1) * pairs]


def _wkv_scan(r, v, kk, lw, kd, a, s0, *, chunk=64, pairs=16, nb=2):
    B, T, D = r.shape
    pairs = min(pairs, D // LANES)
    nb = nb if B % nb == 0 else 1
    lw_lanes = pairs * LANES
    assert T % chunk == 0 and D % lw_lanes == 0
    nc = T // chunk
    ng = D // lw_lanes

    def cidx(z, c):
        return c + z * (nc - 1 - 2 * c)

    tok_spec = pl.BlockSpec((nb, chunk, lw_lanes), lambda z, b, g, c: (b, cidx(z, c), g))
    dir_spec = pl.BlockSpec((None, nb, chunk, lw_lanes), lambda z, b, g, c: (z, b, cidx(z, c), g))
    st_spec = pl.BlockSpec((None, nb, pairs, LANES, LANES), lambda z, b, g, c: (z, b, g, 0, 0))
    y, s_fin = pl.pallas_call(
        functools.partial(_wkv_kernel, chunk=chunk, pairs=pairs, nb=nb),
        grid=(2, B // nb, ng, nc),
        in_specs=[tok_spec, tok_spec, tok_spec, dir_spec, dir_spec, dir_spec, st_spec],
        out_specs=[dir_spec, st_spec],
        out_shape=[jax.ShapeDtypeStruct((2, B, T, D), F32),
                   jax.ShapeDtypeStruct(s0.shape, F32)],
        scratch_shapes=[pltpu.VMEM((nb * pairs, LANES, LANES), F32)],
        compiler_params=_cparams(("arbitrary", "arbitrary", "arbitrary", "arbitrary")),
        name="wkv_scan",
    )(r, v, kk, lw, kd, a, s0)
    return y, s_fin


def _rwkv_prep_kernel(*refs, seq_len, tm, has_vres):
    (x_ref, xp_ref, xn_ref, mod_ref, gpre_ref, mix_ref, wr_ref, wk_ref, wv_ref,
     w1_ref, w2_ref, w0_ref, a1_ref, a2_ref, a0_ref, g1_ref, g2_ref, kk_ref, ka_ref) = refs[:19]
    pos = 19
    if has_vres:
        v1_ref, v2_ref, v0_ref, vf_ref = refs[pos:pos + 4]
        pos += 4
    r_ref, v_ref, kkn_ref, g_ref, lw_ref, kd_ref, a_ref = refs[pos:pos + 7]
    pos += 7
    xmix_scr, tw_scr, ta_scr, tg_scr = refs[pos:pos + 4]
    tv_scr = refs[pos + 4] if has_vres else None
    i = pl.program_id(0)
    n = pl.program_id(1)

    @pl.when(n == 0)
    def _():
        h = _modulated(x_ref[...], mod_ref, gpre_ref, 1)
        t0 = i * tm
        hp = _modulated(xp_ref[...], mod_ref, gpre_ref, 1)[SUBLANES - 1:SUBLANES, :]
        hn = _modulated(xn_ref[...], mod_ref, gpre_ref, 1)[0:1, :]
        hp = jnp.where(t0 % seq_len == 0, 0.0, hp)
        hn = jnp.where((t0 + tm) % seq_len == 0, 0.0, hn)
        rowid = lax.broadcasted_iota(jnp.int32, (tm, 1), 0)
        prev = jnp.where(rowid == 0, hp, pltpu.roll(h, 1, 0))
        nxt = jnp.where(rowid == tm - 1, hn, pltpu.roll(h, tm - 1, 0))
        xx = 0.5 * (prev + nxt) - h
        mixed = lambda m: (h + xx * mix_ref[m:m + 1, :]).astype(BF16)
        xmix_scr[0] = mixed(0)
        xmix_scr[1] = mixed(2)
        xv = mixed(3)
        xmix_scr[2] = xv
        tw_scr[...] = jnp.tanh(_dot(mixed(1), w1_ref[...])).astype(BF16)
        ta_scr[...] = _dot(mixed(4), a1_ref[...]).astype(BF16)
        tg_scr[...] = jax.nn.sigmoid(_dot(mixed(5), g1_ref[...])).astype(BF16)
        if has_vres:
            tv_scr[...] = _dot(xv, v1_ref[...]).astype(BF16)

    lp = w2_ref.shape[1]
    la = a2_ref.shape[1]
    a_dirs = []
    for zz in range(2):
        lora_w = _dot(tw_scr[:, zz * lp:(zz + 1) * lp], w2_ref[zz])
        lw_ref[zz] = -(EXP_NEG_HALF * jax.nn.sigmoid(w0_ref[zz:zz + 1, :] + lora_w))
        a = jax.nn.sigmoid(a0_ref[zz:zz + 1, :] + _dot(ta_scr[:, zz * la:(zz + 1) * la], a2_ref[zz]))
        a_ref[zz] = a
        a_dirs.append(1 + (a - 1) * ka_ref[...])
    g_ref[...] = _dot(tg_scr[...], g2_ref[...])
    v = _dot(xmix_scr[2], wv_ref[...])
    if has_vres:
        v = v + (vf_ref[...] - v) * jax.nn.sigmoid(v0_ref[...] + _dot(tv_scr[...], v2_ref[...]))
    v_ref[...] = v
    k = _dot(xmix_scr[1], wk_ref[...])
    for zz in range(2):
        kd_ref[zz] = k * a_dirs[zz]
    kk = k * kk_ref[...]
    ssq = _dot_exact_rhs01(kk * kk, _head_ones(kk.shape[1]))
    kkn_ref[...] = kk * lax.rsqrt(jnp.maximum(ssq, L2_EPS_SQ))
    r_ref[...] = _dot(xmix_scr[0], wr_ref[...])


def _rwkv_prep(x, mod, grp, g_pre, p, v_first, seq_len, *, tm=512, tn=256):
    M, D = x.shape
    tm = _tile(seq_len, tm, SUBLANES)
    tn = _tile(D, tn, LANES)
    base, rpg = grp
    wtile = pl.BlockSpec((D, tn), lambda i, n: (0, n))
    has_vres = p["v_res"] is not None
    nb = M // SUBLANES
    row = lambda i, n: (i, 0)
    col = lambda i, n: (0, n)
    full = lambda i, n: (0, 0)
    col3 = lambda i, n: (0, 0, n)
    tile = lambda i, n: (i, n)
    tile3 = lambda i, n: (0, i, n)
    lw1, la1, lg1 = p["w1"].shape[1], p["a1"].shape[1], p["g1"].shape[1]
    in_specs = [
        pl.BlockSpec((tm, D), row),
        pl.BlockSpec((SUBLANES, D), lambda i, n: (jnp.maximum(i * (tm // SUBLANES) - 1, 0), 0)),
        pl.BlockSpec((SUBLANES, D), lambda i, n: (jnp.minimum((i + 1) * (tm // SUBLANES), nb - 1), 0)),
        pl.BlockSpec((None, N_MOD, D), _mod_spec(base, rpg, tm)),
        pl.BlockSpec((1, D), full),
        pl.BlockSpec((6, D), full),
        wtile, wtile, wtile,
        pl.BlockSpec((D, lw1), full), pl.BlockSpec((2, lw1 // 2, tn), col3), pl.BlockSpec((2, tn), col),
        pl.BlockSpec((D, la1), full), pl.BlockSpec((2, la1 // 2, tn), col3), pl.BlockSpec((2, tn), col),
        pl.BlockSpec((D, lg1), full), pl.BlockSpec((lg1, tn), col),
        pl.BlockSpec((1, tn), col), pl.BlockSpec((1, tn), col),
    ]
    args = [x, x, x, mod, g_pre.reshape(1, D), p["mix"], p["w_r"], p["w_k"], p["w_v"],
            p["w1"], p["w2"], p["w0"], p["a1"], p["a2"], p["a0"], p["g1"], p["g2"][0],
            p["k_k"].reshape(1, D), p["k_a"].reshape(1, D)]
    scratch = [pltpu.VMEM((3, tm, D), BF16), pltpu.VMEM((tm, lw1), BF16),
               pltpu.VMEM((tm, la1), BF16), pltpu.VMEM((tm, lg1), BF16)]
    if has_vres:
        v0, v1, v2 = p["v_res"]
        lv1 = v1.shape[1]
        in_specs += [pl.BlockSpec((D, lv1), full), pl.BlockSpec((lv1, tn), col),
                     pl.BlockSpec((1, tn), col), pl.BlockSpec((tm, tn), tile)]
        args += [v1, v2[0], v0.reshape(1, D), v_first]
        scratch.append(pltpu.VMEM((tm, lv1), BF16))
    one = jax.ShapeDtypeStruct((M, D), F32)
    two = jax.ShapeDtypeStruct((2, M, D), F32)
    r, v, kk, g, lw, kd, a = pl.pallas_call(
        functools.partial(_rwkv_prep_kernel, seq_len=seq_len, tm=tm, has_vres=has_vres),
        grid=(M // tm, D // tn),
        in_specs=in_specs,
        out_specs=[pl.BlockSpec((tm, tn), tile)] * 4 + [pl.BlockSpec((2, tm, tn), tile3)] * 3,
        out_shape=[one] * 4 + [two] * 3,
        scratch_shapes=scratch,
        compiler_params=_cparams(("arbitrary", "arbitrary")),
        name="rwkv_prep",
    )(*args)
    return dict(r=r, v=v, kk=kk, g=g, lw=lw, kd=kd, a=a)


def _rwkv_out_kernel(y_ref, r_ref, kd_ref, v_ref, g_ref, rk_ref, gnw_ref, gnb_ref, w_ref, x_ref, mod_ref,
                     gpost_ref, o_ref, pre_scr, *, slab):
    D = x_ref.shape[1]
    ones = _head_ones(slab)
    for j in range(D // slab):
        sl = slice(j * slab, (j + 1) * slab)
        wkv = y_ref[0, :, sl] + y_ref[1, :, sl]
        mu = _dot_exact_rhs01(wkv, ones) * (1.0 / HEAD)
        cen = wkv - mu
        var = _dot_exact_rhs01(cen * cen, ones) * (1.0 / HEAD)
        o = cen * lax.rsqrt(var + RWKV_GN_EPS) * gnw_ref[:, sl] + gnb_ref[:, sl]
        coef = _dot_exact_rhs01(r_ref[:, sl] * (kd_ref[0, :, sl] + kd_ref[1, :, sl]) * rk_ref[:, sl], ones)
        pre_scr[:, sl] = ((o + coef * v_ref[:, sl]) * g_ref[:, sl]).astype(BF16)
    y = _dot(pre_scr[...], w_ref[...])
    o_ref[...] = x_ref[...] + mod_ref[5:6, :] * _rms(y, gpost_ref[...])


def _rwkv_out(y, q, p, w_o, x, mod, grp, g_post, *, tm=256, slab=256):
    M, D = x.shape
    tm = _tile(M, tm, SUBLANES)
    slab = _tile(D, slab, LANES)
    base, rpg = grp
    row = lambda i: (i, 0)
    row3 = lambda i: (0, i, 0)
    full = lambda i: (0, 0)
    return pl.pallas_call(
        functools.partial(_rwkv_out_kernel, slab=slab),
        grid=(M // tm,),
        in_specs=[pl.BlockSpec((2, tm, D), row3), pl.BlockSpec((tm, D), row), pl.BlockSpec((2, tm, D), row3),
                  pl.BlockSpec((tm, D), row), pl.BlockSpec((tm, D), row),
                  pl.BlockSpec((1, D), full), pl.BlockSpec((1, D), full), pl.BlockSpec((1, D), full),
                  pl.BlockSpec((D, D), full, pipeline_mode=pl.Buffered(1)), pl.BlockSpec((tm, D), row),
                  pl.BlockSpec((None, N_MOD, D), _mod_spec(base, rpg, tm)), pl.BlockSpec((1, D), full)],
        out_specs=pl.BlockSpec((tm, D), row),
        out_shape=jax.ShapeDtypeStruct((M, D), F32),
        scratch_shapes=[pltpu.VMEM((tm, D), BF16)],
        compiler_params=_cparams(("arbitrary",)),
        name="rwkv_out",
    )(y, q["r"], q["kd"], q["v"], q["g"], p["r_k"].reshape(1, D), p["gn_w"].reshape(1, D),
      p["gn_b"].reshape(1, D), w_o, x, mod, g_post.reshape(1, D))


def _adaln_kernel(c_ref, w_ref, b_ref, o_ref):
    cv = c_ref[...]
    cv = cv * jax.nn.sigmoid(cv)
    o_ref[...] = _dot(cv.astype(BF16), w_ref[...].astype(BF16)) + b_ref[...]


def _adaln(cvec, ada_w, ada_b, *, bn=2048):
    depth, D, N = ada_w.shape
    bn = _tile(N, bn, LANES)
    return pl.pallas_call(
        _adaln_kernel,
        grid=(depth, N // bn),
        in_specs=[pl.BlockSpec((8, D), lambda i, n: (0, 0)),
                  pl.BlockSpec((None, D, bn), lambda i, n: (i, 0, n)),
                  pl.BlockSpec((None, 1, bn), lambda i, n: (i, 0, n))],
        out_specs=pl.BlockSpec((None, 8, bn), lambda i, n: (i, 0, n)),
        out_shape=jax.ShapeDtypeStruct((depth, 8, N), F32),
        compiler_params=_cparams(("arbitrary", "arbitrary")),
        name="adaln",
    )(cvec, ada_w, ada_b.reshape(depth, 1, N))


def _ffn_kernel(x_ref, mod_ref, gpre_ref, gpost_ref, wg_ref, wu_ref, wd_ref, o_ref, *rest, slot, emit_bf16):
    h_scr = rest[-1]
    f = pl.program_id(1)

    @pl.when(f == 0)
    def _():
        _modulated_rows(x_ref, mod_ref, gpre_ref, slot, h_scr)
        o_ref[...] = jnp.zeros_like(o_ref)

    wg, wu, wd = wg_ref[...], wu_ref[...], wd_ref[...]
    if emit_bf16:
        wg, wu, wd = wg.astype(BF16), wu.astype(BF16), wd.astype(BF16)
        for ref, w in zip(rest[:3], (wg, wu, wd)):
            ref[...] = w
    h = h_scr[...]
    g = _dot(h, wg)
    u = _dot(h, wu)
    a = (g * jax.nn.sigmoid(g)) * u
    o_ref[...] += _dot(a.astype(BF16), wd)

    @pl.when(f == pl.num_programs(1) - 1)
    def _():
        gate = mod_ref[3 * slot + 2:3 * slot + 3, :]
        _gated_residual_rows(x_ref, o_ref, MACARON_WEIGHT * gate, gpost_ref, o_ref)


def _ffn(x, mod, grp, slot, g_pre, g_post, w_in, w_up, w_down, layer=None, which=None, *, tm=1024, tf=512):
    M, D = x.shape
    F = w_in.shape[-1]
    emit_bf16 = layer is not None
    tm = _tile(min(M, grp[1]), tm, SUBLANES)
    tf = _tile(F, tf // 2 if emit_bf16 else tf, LANES)
    base, rpg = grp
    if emit_bf16:
        assert M == tm
        up_spec = pl.BlockSpec((None, None, D, tf), lambda i, f: (layer, which, 0, f))
        down_spec = pl.BlockSpec((None, None, tf, D), lambda i, f: (layer, which, f, 0))
    else:
        up_spec = pl.BlockSpec((D, tf), lambda i, f: (0, f))
        down_spec = pl.BlockSpec((tf, D), lambda i, f: (f, 0))
    x_spec = pl.BlockSpec((tm, D), lambda i, f: (i, 0))
    out_specs = [x_spec]
    out_shape = [jax.ShapeDtypeStruct((M, D), F32)]
    if emit_bf16:
        x_spec = pl.BlockSpec((tm, D), lambda i, f: (i, 0), pipeline_mode=pl.Buffered(1))
        out_specs += [pl.BlockSpec((D, tf), lambda i, f: (0, f)), pl.BlockSpec((D, tf), lambda i, f: (0, f)),
                      pl.BlockSpec((tf, D), lambda i, f: (f, 0))]
        out_shape += [jax.ShapeDtypeStruct((D, F), BF16), jax.ShapeDtypeStruct((D, F), BF16),
                      jax.ShapeDtypeStruct((F, D), BF16)]
    out = pl.pallas_call(
        functools.partial(_ffn_kernel, slot=slot, emit_bf16=emit_bf16),
        grid=(M // tm, F // tf),
        in_specs=[x_spec,
                  pl.BlockSpec((None, N_MOD, D), _mod_spec(base, rpg, tm)),
                  pl.BlockSpec((1, D), lambda i, f: (0, 0)),
                  pl.BlockSpec((1, D), lambda i, f: (0, 0)),
                  up_spec, up_spec, down_spec],
        out_specs=out_specs,
        out_shape=out_shape,
        scratch_shapes=[pltpu.VMEM((tm, D), BF16)],
        compiler_params=_cparams(("arbitrary", "arbitrary")),
        name="ffn_cast" if emit_bf16 else "ffn",
    )(x, mod, g_pre.reshape(1, D), g_post.reshape(1, D), w_in, w_up, w_down)
    return (out[0], tuple(out[1:])) if emit_bf16 else out[0]


def _normmod_mm_kernel(x_ref, mod_ref, gpre_ref, w_ref, o_ref, h_scr, *, slot):
    @pl.when(pl.program_id(1) == 0)
    def _():
        _modulated_rows(x_ref, mod_ref, gpre_ref, slot, h_scr)

    o_ref[...] = _dot(h_scr[...], w_ref[...]).astype(o_ref.dtype)


def _normmod_mm(x, mod, grp, slot, g_pre, w, out_dtype, *, tm=1024, tn=1024):
    M, D = x.shape
    N = w.shape[1]
    tm = _tile(min(M, grp[1]), tm, SUBLANES)
    tn = _tile(N, tn, LANES)
    base, rpg = grp
    return pl.pallas_call(
        functools.partial(_normmod_mm_kernel, slot=slot),
        grid=(M // tm, N // tn),
        in_specs=[pl.BlockSpec((tm, D), lambda i, n: (i, 0)),
                  pl.BlockSpec((None, N_MOD, D), _mod_spec(base, rpg, tm)),
                  pl.BlockSpec((1, D), lambda i, n: (0, 0)),
                  pl.BlockSpec((D, tn), lambda i, n: (0, n))],
        out_specs=pl.BlockSpec((tm, tn), lambda i, n: (i, n)),
        out_shape=jax.ShapeDtypeStruct((M, N), out_dtype),
        scratch_shapes=[pltpu.VMEM((tm, D), BF16)],
        compiler_params=_cparams(("arbitrary", "arbitrary")),
        name="normmod_mm",
    )(x, mod, g_pre.reshape(1, D), w)


def _outproj_kernel(a_ref, w_ref, x_ref, mod_ref, gpost_ref, o_ref, *, slot):
    y = _dot(a_ref[...], w_ref[...])
    gate = mod_ref[3 * slot + 2:3 * slot + 3, :]
    o_ref[...] = x_ref[...] + gate * _rms(y, gpost_ref[...])


def _outproj(a, w, x, mod, grp, slot, g_post, *, tm=512):
    M, D = x.shape
    K = a.shape[1]
    tm = _tile(M, tm, SUBLANES)
    base, rpg = grp
    return pl.pallas_call(
        functools.partial(_outproj_kernel, slot=slot),
        grid=(M // tm,),
        in_specs=[pl.BlockSpec((tm, K), lambda i: (i, 0)),
                  pl.BlockSpec((K, D), lambda i: (0, 0), pipeline_mode=pl.Buffered(1)),
                  pl.BlockSpec((tm, D), lambda i: (i, 0)),
                  pl.BlockSpec((None, N_MOD, D), _mod_spec(base, rpg, tm)),
                  pl.BlockSpec((1, D), lambda i: (0, 0))],
        out_specs=pl.BlockSpec((tm, D), lambda i: (i, 0)),
        out_shape=jax.ShapeDtypeStruct((M, D), F32),
        compiler_params=_cparams(("arbitrary",)),
        name="outproj",
    )(a, w, x, mod, g_post.reshape(1, D))


def _nat_kernel(q_ref, k_ref, v_ref, kc_ref, vc_ref, bias_ref, o_ref, *, rows, group):
    scale = HEAD ** -0.5
    n_lat = WIN_H * GRID_W
    lane = lax.broadcasted_iota(jnp.int32, (1, LANES), 1)
    head_masks = (lane < HEAD, lane >= HEAD)
    qcol = lax.broadcasted_iota(jnp.int32, (2 * GRID_W, n_lat), 0) % GRID_W
    kcol = lax.broadcasted_iota(jnp.int32, (2 * GRID_W, n_lat), 1) % GRID_W
    cstart = jnp.clip(qcol - WIN_W // 2, 0, GRID_W - WIN_W)
    col_ok = (kcol >= cstart) & (kcol < cstart + WIN_W)
    kc = kc_ref[...]
    vc = vc_ref[...]
    zero = jnp.zeros((), BF16)

    def step(i, carry):
        chains = []
        for rr in range(group):
            r = i * group + rr
            r0 = jnp.clip(r - WIN_H // 2, 0, rows - WIN_H)
            q = q_ref[pl.ds(pl.multiple_of(r * GRID_W, GRID_W), GRID_W), :] * scale
            q2 = jnp.concatenate([jnp.where(head_masks[0], q, zero), jnp.where(head_masks[1], q, zero)], axis=0)
            kw = k_ref[pl.ds(pl.multiple_of(r0 * GRID_W, GRID_W), n_lat), :]
            vw = v_ref[pl.ds(pl.multiple_of(r0 * GRID_W, GRID_W), n_lat), :]
            chains.append((r, r0, q2, kw, vw))
        s = [_dot_nt(q2, kw) for (_, _, q2, kw, _) in chains]
        sc = [_dot_nt(q2, kc) for (_, _, q2, _, _) in chains]
        s = [jnp.where(col_ok, si + jnp.concatenate(
                [jnp.concatenate([bias_ref[h, 2 * jj - (r - r0) + WIN_H - 1] for jj in range(WIN_H // 2)], axis=1)
                 for h in range(2)], axis=0), MASK_VALUE)
             for si, (r, r0, _, _, _) in zip(s, chains)]
        m = [jnp.maximum(jnp.max(si, axis=-1, keepdims=True), jnp.max(ci, axis=-1, keepdims=True))
             for si, ci in zip(s, sc)]
        e = [jnp.exp(si - mi) for si, mi in zip(s, m)]
        ec = [jnp.exp(ci - mi) for ci, mi in zip(sc, m)]
        den = [jnp.sum(ei, axis=-1, keepdims=True) + jnp.sum(ci, axis=-1, keepdims=True) for ei, ci in zip(e, ec)]
        o = [(_dot(ei.astype(BF16), vw) + _dot(ci.astype(BF16), vc)) / di
             for ei, ci, di, (_, _, _, _, vw) in zip(e, ec, den, chains)]
        for rr in range(group):
            r = i * group + rr
            o_ref[pl.ds(pl.multiple_of(r * GRID_W, GRID_W), GRID_W), :] = jnp.where(
                head_masks[0], o[rr][:GRID_W], o[rr][GRID_W:]).astype(o_ref.dtype)
        return carry

    lax.fori_loop(0, rows // group, step, 0)


def _nat_bias_table(rpb):
    edge = GRID_W - WIN_W
    padded = jnp.pad(rpb, ((0, 0), (0, 0), (edge, edge)), mode="edge")
    rows = jnp.stack([padded[:, :, GRID_W - 1 - q:2 * GRID_W - 1 - q] for q in range(GRID_W)], axis=2)
    return jnp.concatenate([rows[:, :-1], rows[:, 1:]], axis=-1).astype(F32)


def _nat_attention(qkv, qkv_c, bias, *, group=4):
    B, T, D3 = qkv.shape
    D = D3 // 3
    C = qkv_c.shape[1]
    nd = D // LANES
    rows = T // GRID_W
    assert rows >= WIN_H and rows % group == 0
    return pl.pallas_call(
        functools.partial(_nat_kernel, rows=rows, group=group),
        grid=(B, nd),
        in_specs=[pl.BlockSpec((None, T, LANES), lambda b, p: (b, 0, p)),
                  pl.BlockSpec((None, T, LANES), lambda b, p: (b, 0, nd + p)),
                  pl.BlockSpec((None, T, LANES), lambda b, p: (b, 0, 2 * nd + p)),
                  pl.BlockSpec((None, C, LANES), lambda b, p: (b, 0, nd + p)),
                  pl.BlockSpec((None, C, LANES), lambda b, p: (b, 0, 2 * nd + p)),
                  pl.BlockSpec((2, 2 * WIN_H - 2, GRID_W, 2 * GRID_W), lambda b, p: (p, 0, 0, 0))],
        out_specs=pl.BlockSpec((None, T, LANES), lambda b, p: (b, 0, p)),
        out_shape=jax.ShapeDtypeStruct((B, T, D), BF16),
        compiler_params=_cparams(("arbitrary", "arbitrary")),
        name="nat_attention",
    )(qkv, qkv, qkv, qkv_c, qkv_c, bias)


def _ctx_attn_kernel(q_ref, k_ref, v_ref, o_ref):
    scale = HEAD ** -0.5
    lane = lax.broadcasted_iota(jnp.int32, (1, LANES), 1)
    head_masks = (lane < HEAD, lane >= HEAD)
    q = q_ref[...] * scale
    k = k_ref[...]
    v = v_ref[...]
    zero = jnp.zeros((), BF16)
    outs = []
    for h in range(2):
        s = _dot_nt(jnp.where(head_masks[h], q, zero), k)
        e = jnp.exp(s - jnp.max(s, axis=-1, keepdims=True))
        outs.append(_dot(e.astype(BF16), v) / jnp.sum(e, axis=-1, keepdims=True))
    o_ref[...] = jnp.where(head_masks[0], outs[0], outs[1]).astype(o_ref.dtype)


def _ctx_attention(qkv_c):
    B, C, D3 = qkv_c.shape
    D = D3 // 3
    nd = D // LANES
    return pl.pallas_call(
        _ctx_attn_kernel,
        grid=(B, nd),
        in_specs=[pl.BlockSpec((None, C, LANES), lambda b, p: (b, 0, p)),
                  pl.BlockSpec((None, C, LANES), lambda b, p: (b, 0, nd + p)),
                  pl.BlockSpec((None, C, LANES), lambda b, p: (b, 0, 2 * nd + p))],
        out_specs=pl.BlockSpec((None, C, LANES), lambda b, p: (b, 0, p)),
        out_shape=jax.ShapeDtypeStruct((B, C, D), BF16),
        compiler_params=_cparams(("arbitrary", "arbitrary")),
        name="ctx_attention",
    )(qkv_c, qkv_c, qkv_c)


def _pad_to(w, axis, mult=LANES):
    n = w.shape[axis]
    pad = (-n) % mult
    if pad == 0:
        return w
    widths = [(0, 0)] * w.ndim
    widths[axis] = (0, pad)
    return jnp.pad(w, widths)


def _lora_in(w):
    w = _pad_to(w, 2)
    return jnp.concatenate(list(w), axis=1).astype(BF16)


def _lora_out(w):
    return _pad_to(w, 1).astype(BF16)


def _scan_inputs(q, B, L, D):
    three = lambda t: t.reshape(B, L, D)
    four = lambda t: t.reshape(2, B, L, D)
    return three(q["r"]), three(q["v"]), three(q["kk"]), four(q["lw"]), four(q["kd"]), four(q["a"])


def kernel(x, c, ctx, c_ctx, ada_w, ada_b, norm_pre, norm_post, ffn_w_gate, ffn_w_up, ffn_w_down, rwkv_mix, rwkv_w_r, rwkv_w_k, rwkv_w_v, rwkv_w_o, rwkv_w0, rwkv_w1, rwkv_w2, rwkv_a0, rwkv_a1, rwkv_a2, rwkv_v0, rwkv_v1, rwkv_v2, rwkv_k_k, rwkv_k_a, rwkv_r_k, rwkv_g1, rwkv_g2, rwkv_gn_w, rwkv_gn_b, nat_w_qkv, nat_w_o, nat_rpb):
    B, T, D = x.shape
    C = ctx.shape[1]
    depth = ada_w.shape[0]
    assert B + 1 <= 8 and D % (2 * LANES) == 0
    cvec = jnp.zeros((8, D), F32).at[:B].set(c).at[B].set(c_ctx)
    mods = _adaln(cvec, ada_w, ada_b).reshape(depth, 8, N_MOD, D)
    ffn_w = (ffn_w_gate, ffn_w_up, ffn_w_down)
    grp_l, grp_c = (0, T), (B, B * C)
    xl = x.reshape(B * T, D)
    xc = ctx.reshape(B * C, D)
    vf_l = vf_c = None

    def ffn_both(xl, xc, mod, i, slot, which, with_ctx):
        args = (slot, norm_pre[i, slot], norm_post[i, slot])
        if with_ctx and B * C <= 1024:
            xc, w = _ffn(xc, mod, grp_c, *args, *ffn_w, layer=i, which=which)
        else:
            w = tuple(t[i, which].astype(BF16) for t in ffn_w)
            if with_ctx:
                xc = _ffn(xc, mod, grp_c, *args, *w)
        return _ffn(xl, mod, grp_l, *args, *w), xc

    for i in range(depth):
        last = i == depth - 1
        j = i // 2
        mod = mods[i]
        xl, xc = ffn_both(xl, xc, mod, i, 0, 0, True)
        if i % 2 == 0:
            p = dict(
                mix=rwkv_mix[j], w_r=rwkv_w_r[j].astype(BF16), w_k=rwkv_w_k[j].astype(BF16),
                w_v=rwkv_w_v[j].astype(BF16), w0=rwkv_w0[j], w1=_lora_in(rwkv_w1[j]), w2=_lora_out(rwkv_w2[j]),
                a0=rwkv_a0[j], a1=_lora_in(rwkv_a1[j]), a2=_lora_out(rwkv_a2[j]),
                k_k=rwkv_k_k[j], k_a=rwkv_k_a[j], r_k=rwkv_r_k[j].reshape(D),
                g1=_lora_in(rwkv_g1[j][None]), g2=_lora_out(rwkv_g2[j][None]),
                gn_w=rwkv_gn_w[j], gn_b=rwkv_gn_b[j],
                v_res=None if j == 0 else (rwkv_v0[j - 1], _lora_in(rwkv_v1[j - 1][None]), _lora_out(rwkv_v2[j - 1][None])))
            q_c = _rwkv_prep(xc, mod, grp_c, norm_pre[i, 1], p, vf_c, C)
            q_l = _rwkv_prep(xl, mod, grp_l, norm_pre[i, 1], p, vf_l, T)
            if j == 0:
                vf_l, vf_c = q_l["v"], q_c["v"]
            s0 = jnp.zeros((2, B, D // LANES, LANES, LANES), F32)
            y_c, s_c = _wkv_scan(*_scan_inputs(q_c, B, C, D), s0)
            y_l, _ = _wkv_scan(*_scan_inputs(q_l, B, T, D), s_c)
            w_o = rwkv_w_o[j].astype(BF16)
            xl = _rwkv_out(y_l.reshape(2, B * T, D), q_l, p, w_o, xl, mod, grp_l, norm_post[i, 1])
            if not last:
                xc = _rwkv_out(y_c.reshape(2, B * C, D), q_c, p, w_o, xc, mod, grp_c, norm_post[i, 1])
        else:
            w_qkv = nat_w_qkv[j].astype(BF16)
            w_o = nat_w_o[j].astype(BF16)
            qkv_l = _normmod_mm(xl, mod, grp_l, 1, norm_pre[i, 1], w_qkv, BF16).reshape(B, T, 3 * D)
            qkv_c = _normmod_mm(xc, mod, grp_c, 1, norm_pre[i, 1], w_qkv, BF16).reshape(B, C, 3 * D)
            a_l = _nat_attention(qkv_l, qkv_c, _nat_bias_table(nat_rpb[j])).reshape(B * T, D)
            xl = _outproj(a_l, w_o, xl, mod, grp_l, 1, norm_post[i, 1])
            if not last:
                a_c = _ctx_attention(qkv_c).reshape(B * C, D)
                xc = _outproj(a_c, w_o, xc, mod, grp_c, 1, norm_post[i, 1])
        xl, xc = ffn_both(xl, xc, mod, i, 2, 1, not last)
    return xl.reshape(B, T, D)
```

```python
import functools
import math

import jax
import jax.numpy as jnp
from jax import lax
from jax.experimental import pallas as pl
from jax.experimental.pallas import tpu as pltpu

F32 = jnp.float32
BF16 = jnp.bfloat16

LANES = 128
SUBLANES = 8
ROW_SLAB = 32
HEAD = 64
N_MOD = 9
MACARON_WEIGHT = 0.5
RMS_EPS = 1e-6
RWKV_GN_EPS = 64e-5
L2_EPS_SQ = 1e-24
GRID_W = 64
WIN_H = 8
WIN_W = 16
MASK_VALUE = -1e30
EXP_NEG_HALF = math.exp(-0.5)
VMEM_LIMIT = 56 * 1024 * 1024


def _cparams(sem):
    return pltpu.CompilerParams(dimension_semantics=sem, vmem_limit_bytes=VMEM_LIMIT)


def _dot(a, b):
    return jnp.dot(a, b, preferred_element_type=F32)


def _dot_nt(a, b):
    return lax.dot_general(a, b, (((1,), (1,)), ((), ())), preferred_element_type=F32)


def _dot_tn(a, b):
    return lax.dot_general(a, b, (((0,), (0,)), ((), ())), preferred_element_type=F32)


def _split3(x):
    hi = x.astype(BF16)
    r1 = x - hi.astype(F32)
    mid = r1.astype(BF16)
    lo = (r1 - mid.astype(F32)).astype(BF16)
    return hi, mid, lo


def _dot_exact_rhs01(x, m01):
    n = x.shape[0]
    parts = _dot(jnp.concatenate(_split3(x), axis=0), m01)
    return parts[:n] + parts[n:2 * n] + parts[2 * n:]


def _head_ones(n):
    r = lax.broadcasted_iota(jnp.int32, (n, n), 0) // HEAD
    c = lax.broadcasted_iota(jnp.int32, (n, n), 1) // HEAD
    return (r == c).astype(BF16)


def _rms(x, g):
    return x * lax.rsqrt(jnp.mean(x * x, axis=-1, keepdims=True) + RMS_EPS) * g


def _modulated(x, mod_ref, gpre_ref, slot):
    shift = mod_ref[3 * slot:3 * slot + 1, :]
    scale = mod_ref[3 * slot + 1:3 * slot + 2, :]
    return _rms(x, gpre_ref[...]) * (1 + scale) + shift


def _tile(n, want, unit):
    if n <= want:
        return n
    t = (want // unit) * unit
    while n % t:
        t -= unit
    return t


def _row_slabs(rows, fn):
    slab = _tile(rows, ROW_SLAB, 2 * SUBLANES)

    def body(i, carry):
        fn(pl.ds(pl.multiple_of(i * slab, slab), slab))
        return carry

    lax.fori_loop(0, rows // slab, body, 0, unroll=min(8, rows // slab))


def _modulated_rows(x_ref, mod_ref, gpre_ref, slot, dst_ref):
    shift = mod_ref[3 * slot:3 * slot + 1, :]
    scale1 = 1 + mod_ref[3 * slot + 1:3 * slot + 2, :]
    g = gpre_ref[...]

    def slab(sl):
        dst_ref[sl, :] = (_rms(x_ref[sl, :], g) * scale1 + shift).astype(dst_ref.dtype)

    _row_slabs(x_ref.shape[0], slab)


def _gated_residual_rows(x_ref, y_ref, wgate, gpost_ref, o_ref):
    g = gpost_ref[...]
    rows = x_ref.shape[0]
    slab = _tile(rows, ROW_SLAB, 2 * SUBLANES)
    for i in range(rows // slab):
        sl = slice(i * slab, (i + 1) * slab)
        o_ref[sl, :] = x_ref[sl, :] + wgate * _rms(y_ref[sl, :], g)


def _mod_spec(base, rows_per_group, tm):
    assert rows_per_group % tm == 0
    return lambda i, *_: (base + (i * tm) // rows_per_group, 0, 0)


def _wkv_kernel(r_ref, v_ref, kk_ref, lw_ref, kd_ref, a_ref, s0_ref, y_ref, sfin_ref, s_scr,
                *, chunk, pairs, nb):
    z = pl.program_id(0)
    c = pl.program_id(3)
    C = chunk
    side_by_side = lambda ref: jnp.concatenate([ref[b] for b in range(nb)], axis=1)

    @pl.when(c == 0)
    def _():
        for b in range(nb):
            s_scr[b * pairs:(b + 1) * pairs] = s0_ref[b]

    row = lax.broadcasted_iota(jnp.int32, (C, 2 * C), 0)
    pcol = lax.broadcasted_iota(jnp.int32, (C, 2 * C), 1)
    col = pcol % C
    d = (row - col) * (1 - 2 * z)
    strict = d > 0
    incl = d >= 0
    half_masks = (pcol < C, pcol >= C)
    lane = lax.broadcasted_iota(jnp.int32, (1, LANES), 1)
    head_masks = (lane < HEAD, lane >= HEAD)
    srow = lax.broadcasted_iota(jnp.int32, (LANES, LANES), 0)
    scol = lax.broadcasted_iota(jnp.int32, (LANES, LANES), 1)
    blockdiag = (srow < HEAD) == (scol < HEAD)
    eye = (row == col).astype(BF16)
    levels = []
    s = 1
    while s < C:
        levels.append((row // (2 * s) == col // (2 * s)) & (row // s != col // s))
        s *= 2

    def blockdiag2(m):
        zero = jnp.zeros_like(m)
        return jnp.concatenate([jnp.where(half_masks[0], m, zero), jnp.where(half_masks[1], m, zero)], axis=0)

    def by_head(x):
        zero = jnp.zeros_like(x)
        return jnp.concatenate([jnp.where(head_masks[0], x, zero), jnp.where(head_masks[1], x, zero)], axis=0)

    lw = side_by_side(lw_ref)
    kk = side_by_side(kk_ref)
    kd = side_by_side(kd_ref)
    L = lw
    rowid = lax.broadcasted_iota(jnp.int32, (C, 1), 0)
    s = 1
    while s < C:
        L = L + jnp.where(rowid >= s, pltpu.roll(L, s, 0), 0.0)
        s *= 2
    l_end = L[C - 1:C, :]
    L = jnp.where(z == 0, L, l_end - L + lw)
    lm = 0.5 * l_end
    e_m = jnp.exp(-lm)
    e_p = jnp.exp(lm)
    e_sh = jnp.exp(lm - L)
    at_t = -kk * jnp.exp(L - lw)
    rt_t = side_by_side(r_ref) * jnp.exp(L)
    sr_all = jnp.concatenate([at_t, rt_t], axis=0).astype(BF16)
    at_sh = at_t * e_m
    rt_sh = rt_t * e_m
    bt = (kk * side_by_side(a_ref)) * e_sh
    kt = kd * e_sh
    bk_all = jnp.concatenate([bt, kt], axis=0).astype(BF16)
    bkh_all = jnp.concatenate([bt * e_p, kt * e_p], axis=0).astype(BF16)
    v_all = side_by_side(v_ref).astype(BF16)
    s_decay = jnp.exp(l_end)

    prs = range(nb * pairs)
    lanes = [slice(p * LANES, (p + 1) * LANES) for p in prs]
    sr = [_dot_nt(sr_all[:, lanes[p]], s_scr[p].astype(BF16)) for p in prs]
    v_h = [by_head(v_all[:, lanes[p]]) for p in prs]
    quads = [_dot_nt(jnp.concatenate([at_sh[:, lanes[p]], rt_sh[:, lanes[p]]], axis=0).astype(BF16),
                     jnp.concatenate([by_head(bk_all[:C, lanes[p]]), by_head(bk_all[C:, lanes[p]])], axis=0))
             for p in prs]
    a_ab = [jnp.where(strict, q[:C, :2 * C], 0.0).astype(BF16) for q in quads]
    a_ak = [jnp.where(strict, q[:C, 2 * C:], 0.0).astype(BF16) for q in quads]
    r_bk = [jnp.concatenate([jnp.where(incl, q[C:, :2 * C], 0.0), jnp.where(incl, q[C:, 2 * C:], 0.0)],
                            axis=1).astype(BF16) for q in quads]
    t = [eye + jnp.where(levels[0], a, jnp.zeros_like(a)) for a in a_ab]
    for lv in levels[1:]:
        x = [jnp.where(lv, a, jnp.zeros_like(a)) for a in a_ab]
        m1 = [_dot(xi, blockdiag2(ti)).astype(BF16) for xi, ti in zip(x, t)]
        m2 = [_dot(ti, blockdiag2(mi)).astype(BF16) for ti, mi in zip(t, m1)]
        t = [ti + mi for ti, mi in zip(t, m2)]
    rhs = [(sr[p][:C] + _dot(a_ak[p], v_h[p])).astype(BF16) for p in prs]
    u = [_dot(t[p], by_head(rhs[p])).astype(BF16) for p in prs]
    y = [sr[p][C:] + _dot(r_bk[p], jnp.concatenate([by_head(u[p]), v_h[p]], axis=0)) for p in prs]
    for p in prs:
        y_ref[p // pairs, :, lanes[p % pairs]] = y[p]
        upd = _dot_tn(jnp.concatenate([u[p], v_all[:, lanes[p]]], axis=0), bkh_all[:, lanes[p]])
        s_scr[p] = s_scr[p] * s_decay[:, lanes[p]] + jnp.where(blockdiag, upd, 0.0)

    @pl.when(c == pl.num_programs(3) - 1)
    def _():
        for b in range(nb):
            sfin_ref[b] = s_scr[b * pairs:(b + 1) * pairs]


def _wkv_scan(r, v, kk, lw, kd, a, s0, *, chunk=64, pairs=16, nb=2):
    B, T, D = r.shape
    pairs = min(pairs, D // LANES)
    nb = nb if B % nb == 0 else 1
    lw_lanes = pairs * LANES
    assert T % chunk == 0 and D % lw_lanes == 0
    nc = T // chunk
    ng = D // lw_lanes

    def cidx(z, c):
        return c + z * (nc - 1 - 2 * c)

    tok_spec = pl.BlockSpec((nb, chunk, lw_lanes), lambda z, b, g, c: (b, cidx(z, c), g))
    dir_spec = pl.BlockSpec((None, nb, chunk, lw_lanes), lambda z, b, g, c: (z, b, cidx(z, c), g))
    st_spec = pl.BlockSpec((None, nb, pairs, LANES, LANES), lambda z, b, g, c: (z, b, g, 0, 0))
    y, s_fin = pl.pallas_call(
        functools.partial(_wkv_kernel, chunk=chunk, pairs=pairs, nb=nb),
        grid=(2, B // nb, ng, nc),
        in_specs=[tok_spec, tok_spec, tok_spec, dir_spec, dir_spec, dir_spec, st_spec],
        out_specs=[dir_spec, st_spec],
        out_shape=[jax.ShapeDtypeStruct((2, B, T, D), F32),
                   jax.ShapeDtypeStruct(s0.shape, F32)],
        scratch_shapes=[pltpu.VMEM((nb * pairs, LANES, LANES), F32)],
        compiler_params=_cparams(("arbitrary", "arbitrary", "arbitrary", "arbitrary")),
        name="wkv_scan",
    )(r, v, kk, lw, kd, a, s0)
    return y, s_fin


def _rwkv_prep_kernel(*refs, seq_len, tm, has_vres):
    (x_ref, xp_ref, xn_ref, mod_ref, gpre_ref, mix_ref, wr_ref, wk_ref, wv_ref,
     w1_ref, w2_ref, w0_ref, a1_ref, a2_ref, a0_ref, g1_ref, g2_ref, kk_ref, ka_ref) = refs[:19]
    pos = 19
    if has_vres:
        v1_ref, v2_ref, v0_ref, vf_ref = refs[pos:pos + 4]
        pos += 4
    r_ref, v_ref, kkn_ref, g_ref, lw_ref, kd_ref, a_ref = refs[pos:pos + 7]
    pos += 7
    xmix_scr, tw_scr, ta_scr, tg_scr = refs[pos:pos + 4]
    tv_scr = refs[pos + 4] if has_vres else None
    i = pl.program_id(0)
    n = pl.program_id(1)

    @pl.when(n == 0)
    def _():
        h = _modulated(x_ref[...], mod_ref, gpre_ref, 1)
        t0 = i * tm
        hp = _modulated(xp_ref[...], mod_ref, gpre_ref, 1)[SUBLANES - 1:SUBLANES, :]
        hn = _modulated(xn_ref[...], mod_ref, gpre_ref, 1)[0:1, :]
        hp = jnp.where(t0 % seq_len == 0, 0.0, hp)
        hn = jnp.where((t0 + tm) % seq_len == 0, 0.0, hn)
        rowid = lax.broadcasted_iota(jnp.int32, (tm, 1), 0)
        prev = jnp.where(rowid == 0, hp, pltpu.roll(h, 1, 0))
        nxt = jnp.where(rowid == tm - 1, hn, pltpu.roll(h, tm - 1, 0))
        xx = 0.5 * (prev + nxt) - h
        mixed = lambda m: (h + xx * mix_ref[m:m + 1, :]).astype(BF16)
        xmix_scr[0] = mixed(0)
        xmix_scr[1] = mixed(2)
        xv = mixed(3)
        xmix_scr[2] = xv
        tw_scr[...] = jnp.tanh(_dot(mixed(1), w1_ref[...])).astype(BF16)
        ta_scr[...] = _dot(mixed(4), a1_ref[...]).astype(BF16)
        tg_scr[...] = jax.nn.sigmoid(_dot(mixed(5), g1_ref[...])).astype(BF16)
        if has_vres:
            tv_scr[...] = _dot(xv, v1_ref[...]).astype(BF16)

    lp = w2_ref.shape[1]
    la = a2_ref.shape[1]
    a_dirs = []
    for zz in range(2):
        lora_w = _dot(tw_scr[:, zz * lp:(zz + 1) * lp], w2_ref[zz])
        lw_ref[zz] = -(EXP_NEG_HALF * jax.nn.sigmoid(w0_ref[zz:zz + 1, :] + lora_w))
        a = jax.nn.sigmoid(a0_ref[zz:zz + 1, :] + _dot(ta_scr[:, zz * la:(zz + 1) * la], a2_ref[zz]))
        a_ref[zz] = a
        a_dirs.append(1 + (a - 1) * ka_ref[...])
    g_ref[...] = _dot(tg_scr[...], g2_ref[...])
    v = _dot(xmix_scr[2], wv_ref[...])
    if has_vres:
        v = v + (vf_ref[...] - v) * jax.nn.sigmoid(v0_ref[...] + _dot(tv_scr[...], v2_ref[...]))
    v_ref[...] = v
    k = _dot(xmix_scr[1], wk_ref[...])
    for zz in range(2):
        kd_ref[zz] = k * a_dirs[zz]
    kk = k * kk_ref[...]
    ssq = _dot_exact_rhs01(kk * kk, _head_ones(kk.shape[1]))
    kkn_ref[...] = kk * lax.rsqrt(jnp.maximum(ssq, L2_EPS_SQ))
    r_ref[...] = _dot(xmix_scr[0], wr_ref[...])


def _rwkv_prep(x, mod, grp, g_pre, p, v_first, seq_len, *, tm=512, tn=256):
    M, D = x.shape
    tm = _tile(seq_len, tm, SUBLANES)
    tn = _tile(D, tn, LANES)
    base, rpg = grp
    wtile = pl.BlockSpec((D, tn), lambda i, n: (0, n))
    has_vres = p["v_res"] is not None
    nb = M // SUBLANES
    row = lambda i, n: (i, 0)
    col = lambda i, n: (0, n)
    full = lambda i, n: (0, 0)
    col3 = lambda i, n: (0, 0, n)
    tile = lambda i, n: (i, n)
    tile3 = lambda i, n: (0, i, n)
    lw1, la1, lg1 = p["w1"].shape[1], p["a1"].shape[1], p["g1"].shape[1]
    in_specs = [
        pl.BlockSpec((tm, D), row),
        pl.BlockSpec((SUBLANES, D), lambda i, n: (jnp.maximum(i * (tm // SUBLANES) - 1, 0), 0)),
        pl.BlockSpec((SUBLANES, D), lambda i, n: (jnp.minimum((i + 1) * (tm // SUBLANES), nb - 1), 0)),
        pl.BlockSpec((None, N_MOD, D), _mod_spec(base, rpg, tm)),
        pl.BlockSpec((1, D), full),
        pl.BlockSpec((6, D), full),
        wtile, wtile, wtile,
        pl.BlockSpec((D, lw1), full), pl.BlockSpec((2, lw1 // 2, tn), col3), pl.BlockSpec((2, tn), col),
        pl.BlockSpec((D, la1), full), pl.BlockSpec((2, la1 // 2, tn), col3), pl.BlockSpec((2, tn), col),
        pl.BlockSpec((D, lg1), full), pl.BlockSpec((lg1, tn), col),
        pl.BlockSpec((1, tn), col), pl.BlockSpec((1, tn), col),
    ]
    args = [x, x, x, mod, g_pre.reshape(1, D), p["mix"], p["w_r"], p["w_k"], p["w_v"],
            p["w1"], p["w2"], p["w0"], p["a1"], p["a2"], p["a0"], p["g1"], p["g2"][0],
            p["k_k"].reshape(1, D), p["k_a"].reshape(1, D)]
    scratch = [pltpu.VMEM((3, tm, D), BF16), pltpu.VMEM((tm, lw1), BF16),
               pltpu.VMEM((tm, la1), BF16), pltpu.VMEM((tm, lg1), BF16)]
    if has_vres:
        v0, v1, v2 = p["v_res"]
        lv1 = v1.shape[1]
        in_specs += [pl.BlockSpec((D, lv1), full), pl.BlockSpec((lv1, tn), col),
                     pl.BlockSpec((1, tn), col), pl.BlockSpec((tm, tn), tile)]
        args += [v1, v2[0], v0.reshape(1, D), v_first]
        scratch.append(pltpu.VMEM((tm, lv1), BF16))
    one = jax.ShapeDtypeStruct((M, D), F32)
    two = jax.ShapeDtypeStruct((2, M, D), F32)
    r, v, kk, g, lw, kd, a = pl.pallas_call(
        functools.partial(_rwkv_prep_kernel, seq_len=seq_len, tm=tm, has_vres=has_vres),
        grid=(M // tm, D // tn),
        in_specs=in_specs,
        out_specs=[pl.BlockSpec((tm, tn), tile)] * 4 + [pl.BlockSpec((2, tm, tn), tile3)] * 3,
        out_shape=[one] * 4 + [two] * 3,
        scratch_shapes=scratch,
        compiler_params=_cparams(("arbitrary", "arbitrary")),
        name="rwkv_prep",
    )(*args)
    return dict(r=r, v=v, kk=kk, g=g, lw=lw, kd=kd, a=a)


def _rwkv_out_kernel(y_ref, r_ref, kd_ref, v_ref, g_ref, rk_ref, gnw_ref, gnb_ref, w_ref, x_ref, mod_ref,
                     gpost_ref, o_ref, pre_scr, *, slab):
    D = x_ref.shape[1]
    ones = _head_ones(slab)
    for j in range(D // slab):
        sl = slice(j * slab, (j + 1) * slab)
        wkv = y_ref[0, :, sl] + y_ref[1, :, sl]
        mu = _dot_exact_rhs01(wkv, ones) * (1.0 / HEAD)
        cen = wkv - mu
        var = _dot_exact_rhs01(cen * cen, ones) * (1.0 / HEAD)
        o = cen * lax.rsqrt(var + RWKV_GN_EPS) * gnw_ref[:, sl] + gnb_ref[:, sl]
        coef = _dot_exact_rhs01(r_ref[:, sl] * (kd_ref[0, :, sl] + kd_ref[1, :, sl]) * rk_ref[:, sl], ones)
        pre_scr[:, sl] = ((o + coef * v_ref[:, sl]) * g_ref[:, sl]).astype(BF16)
    y = _dot(pre_scr[...], w_ref[...])
    o_ref[...] = x_ref[...] + mod_ref[5:6, :] * _rms(y, gpost_ref[...])


def _rwkv_out(y, q, p, w_o, x, mod, grp, g_post, *, tm=256, slab=256):
    M, D = x.shape
    tm = _tile(M, tm, SUBLANES)
    slab = _tile(D, slab, LANES)
    base, rpg = grp
    row = lambda i: (i, 0)
    row3 = lambda i: (0, i, 0)
    full = lambda i: (0, 0)
    return pl.pallas_call(
        functools.partial(_rwkv_out_kernel, slab=slab),
        grid=(M // tm,),
        in_specs=[pl.BlockSpec((2, tm, D), row3), pl.BlockSpec((tm, D), row), pl.BlockSpec((2, tm, D), row3),
                  pl.BlockSpec((tm, D), row), pl.BlockSpec((tm, D), row),
                  pl.BlockSpec((1, D), full), pl.BlockSpec((1, D), full), pl.BlockSpec((1, D), full),
                  pl.BlockSpec((D, D), full, pipeline_mode=pl.Buffered(1)), pl.BlockSpec((tm, D), row),
                  pl.BlockSpec((None, N_MOD, D), _mod_spec(base, rpg, tm)), pl.BlockSpec((1, D), full)],
        out_specs=pl.BlockSpec((tm, D), row),
        out_shape=jax.ShapeDtypeStruct((M, D), F32),
        scratch_shapes=[pltpu.VMEM((tm, D), BF16)],
        compiler_params=_cparams(("arbitrary",)),
        name="rwkv_out",
    )(y, q["r"], q["kd"], q["v"], q["g"], p["r_k"].reshape(1, D), p["gn_w"].reshape(1, D),
      p["gn_b"].reshape(1, D), w_o, x, mod, g_post.reshape(1, D))


def _adaln_kernel(c_ref, w_ref, b_ref, o_ref):
    cv = c_ref[...]
    cv = cv * jax.nn.sigmoid(cv)
    o_ref[...] = _dot(cv.astype(BF16), w_ref[...].astype(BF16)) + b_ref[...]


def _adaln(cvec, ada_w, ada_b, *, bn=2048):
    depth, D, N = ada_w.shape
    bn = _tile(N, bn, LANES)
    return pl.pallas_call(
        _adaln_kernel,
        grid=(depth, N // bn),
        in_specs=[pl.BlockSpec((8, D), lambda i, n: (0, 0)),
                  pl.BlockSpec((None, D, bn), lambda i, n: (i, 0, n)),
                  pl.BlockSpec((None, 1, bn), lambda i, n: (i, 0, n))],
        out_specs=pl.BlockSpec((None, 8, bn), lambda i, n: (i, 0, n)),
        out_shape=jax.ShapeDtypeStruct((depth, 8, N), F32),
        compiler_params=_cparams(("arbitrary", "arbitrary")),
        name="adaln",
    )(cvec, ada_w, ada_b.reshape(depth, 1, N))


def _ffn_kernel(x_ref, mod_ref, gpre_ref, gpost_ref, wg_ref, wu_ref, wd_ref, o_ref, *rest, slot, emit_bf16):
    h_scr = rest[-1]
    f = pl.program_id(1)

    @pl.when(f == 0)
    def _():
        _modulated_rows(x_ref, mod_ref, gpre_ref, slot, h_scr)
        o_ref[...] = jnp.zeros_like(o_ref)

    wg, wu, wd = wg_ref[...], wu_ref[...], wd_ref[...]
    if emit_bf16:
        wg, wu, wd = wg.astype(BF16), wu.astype(BF16), wd.astype(BF16)
        for ref, w in zip(rest[:3], (wg, wu, wd)):
            ref[...] = w
    h = h_scr[...]
    g = _dot(h, wg)
    u = _dot(h, wu)
    a = (g * jax.nn.sigmoid(g)) * u
    o_ref[...] += _dot(a.astype(BF16), wd)

    @pl.when(f == pl.num_programs(1) - 1)
    def _():
        gate = mod_ref[3 * slot + 2:3 * slot + 3, :]
        _gated_residual_rows(x_ref, o_ref, MACARON_WEIGHT * gate, gpost_ref, o_ref)


def _ffn(x, mod, grp, slot, g_pre, g_post, w_in, w_up, w_down, layer=None, which=None, *, tm=1024, tf=512):
    M, D = x.shape
    F = w_in.shape[-1]
    emit_bf16 = layer is not None
    tm = _tile(min(M, grp[1]), tm, SUBLANES)
    tf = _tile(F, tf // 2 if emit_bf16 else tf, LANES)
    base, rpg = grp
    if emit_bf16:
        assert M == tm
        up_spec = pl.BlockSpec((None, None, D, tf), lambda i, f: (layer, which, 0, f))
        down_spec = pl.BlockSpec((None, None, tf, D), lambda i, f: (layer, which, f, 0))
    else:
        up_spec = pl.BlockSpec((D, tf), lambda i, f: (0, f))
        down_spec = pl.BlockSpec((tf, D), lambda i, f: (f, 0))
    x_spec = pl.BlockSpec((tm, D), lambda i, f: (i, 0))
    out_specs = [x_spec]
    out_shape = [jax.ShapeDtypeStruct((M, D), F32)]
    if emit_bf16:
        x_spec = pl.BlockSpec((tm, D), lambda i, f: (i, 0), pipeline_mode=pl.Buffered(1))
        out_specs += [pl.BlockSpec((D, tf), lambda i, f: (0, f)), pl.BlockSpec((D, tf), lambda i, f: (0, f)),
                      pl.BlockSpec((tf, D), lambda i, f: (f, 0))]
        out_shape += [jax.ShapeDtypeStruct((D, F), BF16), jax.ShapeDtypeStruct((D, F), BF16),
                      jax.ShapeDtypeStruct((F, D), BF16)]
    out = pl.pallas_call(
        functools.partial(_ffn_kernel, slot=slot, emit_bf16=emit_bf16),
        grid=(M // tm, F // tf),
        in_specs=[x_spec,
                  pl.BlockSpec((None, N_MOD, D), _mod_spec(base, rpg, tm)),
                  pl.BlockSpec((1, D), lambda i, f: (0, 0)),
                  pl.BlockSpec((1, D), lambda i, f: (0, 0)),
                  up_spec, up_spec, down_spec],
        out_specs=out_specs,
        out_shape=out_shape,
        scratch_shapes=[pltpu.VMEM((tm, D), BF16)],
        compiler_params=_cparams(("arbitrary", "arbitrary")),
        name="ffn_cast" if emit_bf16 else "ffn",
    )(x, mod, g_pre.reshape(1, D), g_post.reshape(1, D), w_in, w_up, w_down)
    return (out[0], tuple(out[1:])) if emit_bf16 else out[0]


def _normmod_mm_kernel(x_ref, mod_ref, gpre_ref, w_ref, o_ref, h_scr, *, slot):
    @pl.when(pl.program_id(1) == 0)
    def _():
        _modulated_rows(x_ref, mod_ref, gpre_ref, slot, h_scr)

    o_ref[...] = _dot(h_scr[...], w_ref[...]).astype(o_ref.dtype)


def _normmod_mm(x, mod, grp, slot, g_pre, w, out_dtype, *, tm=1024, tn=1024):
    M, D = x.shape
    N = w.shape[1]
    tm = _tile(min(M, grp[1]), tm, SUBLANES)
    tn = _tile(N, tn, LANES)
    base, rpg = grp
    return pl.pallas_call(
        functools.partial(_normmod_mm_kernel, slot=slot),
        grid=(M // tm, N // tn),
        in_specs=[pl.BlockSpec((tm, D), lambda i, n: (i, 0)),
                  pl.BlockSpec((None, N_MOD, D), _mod_spec(base, rpg, tm)),
                  pl.BlockSpec((1, D), lambda i, n: (0, 0)),
                  pl.BlockSpec((D, tn), lambda i, n: (0, n))],
        out_specs=pl.BlockSpec((tm, tn), lambda i, n: (i, n)),
        out_shape=jax.ShapeDtypeStruct((M, N), out_dtype),
        scratch_shapes=[pltpu.VMEM((tm, D), BF16)],
        compiler_params=_cparams(("arbitrary", "arbitrary")),
        name="normmod_mm",
    )(x, mod, g_pre.reshape(1, D), w)


def _outproj_kernel(a_ref, w_ref, x_ref, mod_ref, gpost_ref, o_ref, *, slot):
    y = _dot(a_ref[...], w_ref[...])
    gate = mod_ref[3 * slot + 2:3 * slot + 3, :]
    o_ref[...] = x_ref[...] + gate * _rms(y, gpost_ref[...])


def _outproj(a, w, x, mod, grp, slot, g_post, *, tm=512):
    M, D = x.shape
    K = a.shape[1]
    tm = _tile(M, tm, SUBLANES)
    base, rpg = grp
    return pl.pallas_call(
        functools.partial(_outproj_kernel, slot=slot),
        grid=(M // tm,),
        in_specs=[pl.BlockSpec((tm, K), lambda i: (i, 0)),
                  pl.BlockSpec((K, D), lambda i: (0, 0), pipeline_mode=pl.Buffered(1)),
                  pl.BlockSpec((tm, D), lambda i: (i, 0)),
                  pl.BlockSpec((None, N_MOD, D), _mod_spec(base, rpg, tm)),
                  pl.BlockSpec((1, D), lambda i: (0, 0))],
        out_specs=pl.BlockSpec((tm, D), lambda i: (i, 0)),
        out_shape=jax.ShapeDtypeStruct((M, D), F32),
        compiler_params=_cparams(("arbitrary",)),
        name="outproj",
    )(a, w, x, mod, g_post.reshape(1, D))


def _nat_kernel(q_ref, k_ref, v_ref, kc_ref, vc_ref, bias_ref, o_ref, *, rows, group):
    scale = HEAD ** -0.5
    n_lat = WIN_H * GRID_W
    lane = lax.broadcasted_iota(jnp.int32, (1, LANES), 1)
    head_masks = (lane < HEAD, lane >= HEAD)
    qcol = lax.broadcasted_iota(jnp.int32, (2 * GRID_W, n_lat), 0) % GRID_W
    kcol = lax.broadcasted_iota(jnp.int32, (2 * GRID_W, n_lat), 1) % GRID_W
    cstart = jnp.clip(qcol - WIN_W // 2, 0, GRID_W - WIN_W)
    col_ok = (kcol >= cstart) & (kcol < cstart + WIN_W)
    kc = kc_ref[...]
    vc = vc_ref[...]
    zero = jnp.zeros((), BF16)

    def step(i, carry):
        chains = []
        for rr in range(group):
            r = i * group + rr
            r0 = jnp.clip(r - WIN_H // 2, 0, rows - WIN_H)
            q = q_ref[pl.ds(pl.multiple_of(r * GRID_W, GRID_W), GRID_W), :] * scale
            q2 = jnp.concatenate([jnp.where(head_masks[0], q, zero), jnp.where(head_masks[1], q, zero)], axis=0)
            kw = k_ref[pl.ds(pl.multiple_of(r0 * GRID_W, GRID_W), n_lat), :]
            vw = v_ref[pl.ds(pl.multiple_of(r0 * GRID_W, GRID_W), n_lat), :]
            chains.append((r, r0, q2, kw, vw))
        s = [_dot_nt(q2, kw) for (_, _, q2, kw, _) in chains]
        sc = [_dot_nt(q2, kc) for (_, _, q2, _, _) in chains]
        s = [jnp.where(col_ok, si + jnp.concatenate(
                [jnp.concatenate([bias_ref[h, 2 * jj - (r - r0) + WIN_H - 1] for jj in range(WIN_H // 2)], axis=1)
                 for h in range(2)], axis=0), MASK_VALUE)
             for si, (r, r0, _, _, _) in zip(s, chains)]
        m = [jnp.maximum(jnp.max(si, axis=-1, keepdims=True), jnp.max(ci, axis=-1, keepdims=True))
             for si, ci in zip(s, sc)]
        e = [jnp.exp(si - mi) for si, mi in zip(s, m)]
        ec = [jnp.exp(ci - mi) for ci, mi in zip(sc, m)]
        den = [jnp.sum(ei, axis=-1, keepdims=True) + jnp.sum(ci, axis=-1, keepdims=True) for ei, ci in zip(e, ec)]
        o = [(_dot(ei.astype(BF16), vw) + _dot(ci.astype(BF16), vc)) / di
             for ei, ci, di, (_, _, _, _, vw) in zip(e, ec, den, chains)]
        for rr in range(group):
            r = i * group + rr
            o_ref[pl.ds(pl.multiple_of(r * GRID_W, GRID_W), GRID_W), :] = jnp.where(
                head_masks[0], o[rr][:GRID_W], o[rr][GRID_W:]).astype(o_ref.dtype)
        return carry

    lax.fori_loop(0, rows // group, step, 0)


def _nat_bias_table(rpb):
    edge = GRID_W - WIN_W
    padded = jnp.pad(rpb, ((0, 0), (0, 0), (edge, edge)), mode="edge")
    rows = jnp.stack([padded[:, :, GRID_W - 1 - q:2 * GRID_W - 1 - q] for q in range(GRID_W)], axis=2)
    return jnp.concatenate([rows[:, :-1], rows[:, 1:]], axis=-1).astype(F32)


def _nat_attention(qkv, qkv_c, bias, *, group=4):
    B, T, D3 = qkv.shape
    D = D3 // 3
    C = qkv_c.shape[1]
    nd = D // LANES
    rows = T // GRID_W
    assert rows >= WIN_H and rows % group == 0
    return pl.pallas_call(
        functools.partial(_nat_kernel, rows=rows, group=group),
        grid=(B, nd),
        in_specs=[pl.BlockSpec((None, T, LANES), lambda b, p: (b, 0, p)),
                  pl.BlockSpec((None, T, LANES), lambda b, p: (b, 0, nd + p)),
                  pl.BlockSpec((None, T, LANES), lambda b, p: (b, 0, 2 * nd + p)),
                  pl.BlockSpec((None, C, LANES), lambda b, p: (b, 0, nd + p)),
                  pl.BlockSpec((None, C, LANES), lambda b, p: (b, 0, 2 * nd + p)),
                  pl.BlockSpec((2, 2 * WIN_H - 2, GRID_W, 2 * GRID_W), lambda b, p: (p, 0, 0, 0))],
        out_specs=pl.BlockSpec((None, T, LANES), lambda b, p: (b, 0, p)),
        out_shape=jax.ShapeDtypeStruct((B, T, D), BF16),
        compiler_params=_cparams(("arbitrary", "arbitrary")),
        name="nat_attention",
    )(qkv, qkv, qkv, qkv_c, qkv_c, bias)


def _ctx_attn_kernel(q_ref, k_ref, v_ref, o_ref):
    scale = HEAD ** -0.5
    lane = lax.broadcasted_iota(jnp.int32, (1, LANES), 1)
    head_masks = (lane < HEAD, lane >= HEAD)
    q = q_ref[...] * scale
    k = k_ref[...]
    v = v_ref[...]
    zero = jnp.zeros((), BF16)
    outs = []
    for h in range(2):
        s = _dot_nt(jnp.where(head_masks[h], q, zero), k)
        e = jnp.exp(s - jnp.max(s, axis=-1, keepdims=True))
        outs.append(_dot(e.astype(BF16), v) / jnp.sum(e, axis=-1, keepdims=True))
    o_ref[...] = jnp.where(head_masks[0], outs[0], outs[1]).astype(o_ref.dtype)


def _ctx_attention(qkv_c):
    B, C, D3 = qkv_c.shape
    D = D3 // 3
    nd = D // LANES
    return pl.pallas_call(
        _ctx_attn_kernel,
        grid=(B, nd),
        in_specs=[pl.BlockSpec((None, C, LANES), lambda b, p: (b, 0, p)),
                  pl.BlockSpec((None, C, LANES), lambda b, p: (b, 0, nd + p)),
                  pl.BlockSpec((None, C, LANES), lambda b, p: (b, 0, 2 * nd + p))],
        out_specs=pl.BlockSpec((None, C, LANES), lambda b, p: (b, 0, p)),
        out_shape=jax.ShapeDtypeStruct((B, C, D), BF16),
        compiler_params=_cparams(("arbitrary", "arbitrary")),
        name="ctx_attention",
    )(qkv_c, qkv_c, qkv_c)


def _pad_to(w, axis, mult=LANES):
    n = w.shape[axis]
    pad = (-n) % mult
    if pad == 0:
        return w
    widths = [(0, 0)] * w.ndim
    widths[axis] = (0, pad)
    return jnp.pad(w, widths)


def _lora_in(w):
    w = _pad_to(w, 2)
    return jnp.concatenate(list(w), axis=1).astype(BF16)


def _lora_out(w):
    return _pad_to(w, 1).astype(BF16)


def _scan_inputs(q, B, L, D):
    three = lambda t: t.reshape(B, L, D)
    four = lambda t: t.reshape(2, B, L, D)
    return three(q["r"]), three(q["v"]), three(q["kk"]), four(q["lw"]), four(q["kd"]), four(q["a"])


def kernel(x, c, ctx, c_ctx, ada_w, ada_b, norm_pre, norm_post, ffn_w_gate, ffn_w_up, ffn_w_down, rwkv_mix, rwkv_w_r, rwkv_w_k, rwkv_w_v, rwkv_w_o, rwkv_w0, rwkv_w1, rwkv_w2, rwkv_a0, rwkv_a1, rwkv_a2, rwkv_v0, rwkv_v1, rwkv_v2, rwkv_k_k, rwkv_k_a, rwkv_r_k, rwkv_g1, rwkv_g2, rwkv_gn_w, rwkv_gn_b, nat_w_qkv, nat_w_o, nat_rpb):
    B, T, D = x.shape
    C = ctx.shape[1]
    depth = ada_w.shape[0]
    assert B + 1 <= 8 and D % (2 * LANES) == 0
    cvec = jnp.zeros((8, D), F32).at[:B].set(c).at[B].set(c_ctx)
    mods = _adaln(cvec, ada_w, ada_b).reshape(depth, 8, N_MOD, D)
    ffn_w = (ffn_w_gate, ffn_w_up, ffn_w_down)
    grp_l, grp_c = (0, T), (B, B * C)
    xl = x.reshape(B * T, D)
    xc = ctx.reshape(B * C, D)
    vf_l = vf_c = None

    def ffn_both(xl, xc, mod, i, slot, which, with_ctx):
        args = (slot, norm_pre[i, slot], norm_post[i, slot])
        if with_ctx and B * C <= 1024:
            xc, w = _ffn(xc, mod, grp_c, *args, *ffn_w, layer=i, which=which)
        else:
            w = tuple(t[i, which].astype(BF16) for t in ffn_w)
            if with_ctx:
                xc = _ffn(xc, mod, grp_c, *args, *w)
        return _ffn(xl, mod, grp_l, *args, *w), xc

    for i in range(depth):
        last = i == depth - 1
        j = i // 2
        mod = mods[i]
        xl, xc = ffn_both(xl, xc, mod, i, 0, 0, True)
        if i % 2 == 0:
            p = dict(
                mix=rwkv_mix[j], w_r=rwkv_w_r[j].astype(BF16), w_k=rwkv_w_k[j].astype(BF16),
                w_v=rwkv_w_v[j].astype(BF16), w0=rwkv_w0[j], w1=_lora_in(rwkv_w1[j]), w2=_lora_out(rwkv_w2[j]),
                a0=rwkv_a0[j], a1=_lora_in(rwkv_a1[j]), a2=_lora_out(rwkv_a2[j]),
                k_k=rwkv_k_k[j], k_a=rwkv_k_a[j], r_k=rwkv_r_k[j].reshape(D),
                g1=_lora_in(rwkv_g1[j][None]), g2=_lora_out(rwkv_g2[j][None]),
                gn_w=rwkv_gn_w[j], gn_b=rwkv_gn_b[j],
                v_res=None if j == 0 else (rwkv_v0[j - 1], _lora_in(rwkv_v1[j - 1][None]), _lora_out(rwkv_v2[j - 1][None])))
            q_c = _rwkv_prep(xc, mod, grp_c, norm_pre[i, 1], p, vf_c, C)
            q_l = _rwkv_prep(xl, mod, grp_l, norm_pre[i, 1], p, vf_l, T)
            if j == 0:
                vf_l, vf_c = q_l["v"], q_c["v"]
            s0 = jnp.zeros((2, B, D // LANES, LANES, LANES), F32)
            y_c, s_c = _wkv_scan(*_scan_inputs(q_c, B, C, D), s0)
            y_l, _ = _wkv_scan(*_scan_inputs(q_l, B, T, D), s_c)
            w_o = rwkv_w_o[j].astype(BF16)
            xl = _rwkv_out(y_l.reshape(2, B * T, D), q_l, p, w_o, xl, mod, grp_l, norm_post[i, 1])
            if not last:
                xc = _rwkv_out(y_c.reshape(2, B * C, D), q_c, p, w_o, xc, mod, grp_c, norm_post[i, 1])
        else:
            w_qkv = nat_w_qkv[j].astype(BF16)
            w_o = nat_w_o[j].astype(BF16)
            qkv_l = _normmod_mm(xl, mod, grp_l, 1, norm_pre[i, 1], w_qkv, BF16).reshape(B, T, 3 * D)
            qkv_c = _normmod_mm(xc, mod, grp_c, 1, norm_pre[i, 1], w_qkv, BF16).reshape(B, C, 3 * D)
            a_l = _nat_attention(qkv_l, qkv_c, _nat_bias_table(nat_rpb[j])).reshape(B * T, D)
            xl = _outproj(a_l, w_o, xl, mod, grp_l, 1, norm_post[i, 1])
            if not last:
                a_c = _ctx_attention(qkv_c).reshape(B * C, D)
                xc = _outproj(a_c, w_o, xc, mod, grp_c, 1, norm_post[i, 1])
        xl, xc = ffn_both(xl, xc, mod, i, 2, 1, not last)
    return xl.reshape(B, T, D)
```

```python
import functools
import math

import jax
import jax.numpy as jnp
from jax import lax
from jax.experimental import pallas as pl
from jax.experimental.pallas import tpu as pltpu

F32 = jnp.float32
BF16 = jnp.bfloat16

LANES = 128
SUBLANES = 8
ROW_SLAB = 32
HEAD = 64
N_MOD = 9
MACARON_WEIGHT = 0.5
RMS_EPS = 1e-6
RWKV_GN_EPS = 64e-5
L2_EPS_SQ = 1e-24
GRID_W = 64
WIN_H = 8
WIN_W = 16
MASK_VALUE = -1e30
EXP_NEG_HALF = math.exp(-0.5)
VMEM_LIMIT = 56 * 1024 * 1024


def _cparams(sem, vmem_limit=VMEM_LIMIT):
    return pltpu.CompilerParams(dimension_semantics=sem, vmem_limit_bytes=vmem_limit)


def _dot(a, b):
    return jnp.dot(a, b, preferred_element_type=F32)


def _dot_nt(a, b):
    return lax.dot_general(a, b, (((1,), (1,)), ((), ())), preferred_element_type=F32)


def _dot_tn(a, b):
    return lax.dot_general(a, b, (((0,), (0,)), ((), ())), preferred_element_type=F32)


def _split3(x):
    hi = x.astype(BF16)
    r1 = x - hi.astype(F32)
    mid = r1.astype(BF16)
    lo = (r1 - mid.astype(F32)).astype(BF16)
    return hi, mid, lo


def _dot_exact_rhs01(x, m01):
    n = x.shape[0]
    parts = _dot(jnp.concatenate(_split3(x), axis=0), m01)
    return parts[:n] + parts[n:2 * n] + parts[2 * n:]


def _head_ones(n):
    r = lax.broadcasted_iota(jnp.int32, (n, n), 0) // HEAD
    c = lax.broadcasted_iota(jnp.int32, (n, n), 1) // HEAD
    return (r == c).astype(BF16)


def _rms(x, g):
    return x * lax.rsqrt(jnp.mean(x * x, axis=-1, keepdims=True) + RMS_EPS) * g


def _modulated(x, mod_ref, gpre_ref, slot):
    shift = mod_ref[3 * slot:3 * slot + 1, :]
    scale = mod_ref[3 * slot + 1:3 * slot + 2, :]
    return _rms(x, gpre_ref[...]) * (1 + scale) + shift


def _tile(n, want, unit):
    if n <= want:
        return n
    t = (want // unit) * unit
    while n % t:
        t -= unit
    return t


def _row_slabs(rows, fn):
    slab = _tile(rows, ROW_SLAB, 2 * SUBLANES)

    def body(i, carry):
        fn(pl.ds(pl.multiple_of(i * slab, slab), slab))
        return carry

    lax.fori_loop(0, rows // slab, body, 0, unroll=min(8, rows // slab))


def _modulated_rows(x_ref, mod_ref, gpre_ref, slot, dst_ref):
    shift = mod_ref[3 * slot:3 * slot + 1, :]
    scale1 = 1 + mod_ref[3 * slot + 1:3 * slot + 2, :]
    g = gpre_ref[...]

    def slab(sl):
        dst_ref[sl, :] = (_rms(x_ref[sl, :], g) * scale1 + shift).astype(dst_ref.dtype)

    _row_slabs(x_ref.shape[0], slab)


def _gated_residual_rows(x_ref, y_ref, wgate, gpost_ref, o_ref):
    g = gpost_ref[...]
    rows = x_ref.shape[0]
    slab = _tile(rows, ROW_SLAB, 2 * SUBLANES)
    for i in range(rows // slab):
        sl = slice(i * slab, (i + 1) * slab)
        o_ref[sl, :] = x_ref[sl, :] + wgate * _rms(y_ref[sl, :], g)


def _mod_spec(base, rows_per_group, tm):
    assert rows_per_group % tm == 0
    return lambda i, *_: (base + (i * tm) // rows_per_group, 0, 0)


def _wkv_kernel(r_ref, v_ref, kk_ref, lw_ref, kd_ref, a_ref, s0_ref, y_ref, sfin_ref, s_scr,
                *, chunk, pairs, nb):
    z = pl.program_id(0)
    c = pl.program_id(3)
    C = chunk
    side_by_side = lambda ref: jnp.concatenate([ref[b] for b in range(nb)], axis=1)

    @pl.when(c == 0)
    def _():
        for b in range(nb):
            s_scr[b * pairs:(b + 1) * pairs] = s0_ref[b]

    row = lax.broadcasted_iota(jnp.int32, (C, 2 * C), 0)
    pcol = lax.broadcasted_iota(jnp.int32, (C, 2 * C), 1)
    col = pcol % C
    d = (row - col) * (1 - 2 * z)
    strict = d > 0
    incl = d >= 0
    half_masks = (pcol < C, pcol >= C)
    lane = lax.broadcasted_iota(jnp.int32, (1, LANES), 1)
    head_masks = (lane < HEAD, lane >= HEAD)
    srow = lax.broadcasted_iota(jnp.int32, (LANES, LANES), 0)
    scol = lax.broadcasted_iota(jnp.int32, (LANES, LANES), 1)
    blockdiag = (srow < HEAD) == (scol < HEAD)
    eye = (row == col).astype(BF16)
    levels = []
    s = 1
    while s < C:
        levels.append((row // (2 * s) == col // (2 * s)) & (row // s != col // s))
        s *= 2

    def blockdiag2(m):
        zero = jnp.zeros_like(m)
        return jnp.concatenate([jnp.where(half_masks[0], m, zero), jnp.where(half_masks[1], m, zero)], axis=0)

    def by_head(x):
        zero = jnp.zeros_like(x)
        return jnp.concatenate([jnp.where(head_masks[0], x, zero), jnp.where(head_masks[1], x, zero)], axis=0)

    lw = side_by_side(lw_ref)
    kk = side_by_side(kk_ref)
    kd = side_by_side(kd_ref)
    L = lw
    rowid = lax.broadcasted_iota(jnp.int32, (C, 1), 0)
    s = 1
    while s < C:
        L = L + jnp.where(rowid >= s, pltpu.roll(L, s, 0), 0.0)
        s *= 2
    l_end = L[C - 1:C, :]
    L = jnp.where(z == 0, L, l_end - L + lw)
    lm = 0.5 * l_end
    e_m = jnp.exp(-lm)
    e_p = jnp.exp(lm)
    e_sh = jnp.exp(lm - L)
    at_t = -kk * jnp.exp(L - lw)
    rt_t = side_by_side(r_ref) * jnp.exp(L)
    sr_all = jnp.concatenate([at_t, rt_t], axis=0).astype(BF16)
    at_sh = at_t * e_m
    rt_sh = rt_t * e_m
    bt = (kk * side_by_side(a_ref)) * e_sh
    kt = kd * e_sh
    bk_all = jnp.concatenate([bt, kt], axis=0).astype(BF16)
    bkh_all = jnp.concatenate([bt * e_p, kt * e_p], axis=0).astype(BF16)
    v_all = side_by_side(v_ref).astype(BF16)
    s_decay = jnp.exp(l_end)

    prs = range(nb * pairs)
    lanes = [slice(p * LANES, (p + 1) * LANES) for p in prs]
    sr = [_dot_nt(sr_all[:, lanes[p]], s_scr[p].astype(BF16)) for p in prs]
    v_h = [by_head(v_all[:, lanes[p]]) for p in prs]
    quads = [_dot_nt(jnp.concatenate([at_sh[:, lanes[p]], rt_sh[:, lanes[p]]], axis=0).astype(BF16),
                     jnp.concatenate([by_head(bk_all[:C, lanes[p]]), by_head(bk_all[C:, lanes[p]])], axis=0))
             for p in prs]
    a_ab = [jnp.where(strict, q[:C, :2 * C], 0.0).astype(BF16) for q in quads]
    a_ak = [jnp.where(strict, q[:C, 2 * C:], 0.0).astype(BF16) for q in quads]
    r_bk = [jnp.concatenate([jnp.where(incl, q[C:, :2 * C], 0.0), jnp.where(incl, q[C:, 2 * C:], 0.0)],
                            axis=1).astype(BF16) for q in quads]
    t = [eye + jnp.where(levels[0], a, jnp.zeros_like(a)) for a in a_ab]
    for lv in levels[1:]:
        x = [jnp.where(lv, a, jnp.zeros_like(a)) for a in a_ab]
        m1 = [_dot(xi, blockdiag2(ti)).astype(BF16) for xi, ti in zip(x, t)]
        m2 = [_dot(ti, blockdiag2(mi)).astype(BF16) for ti, mi in zip(t, m1)]
        t = [ti + mi for ti, mi in zip(t, m2)]
    rhs = [(sr[p][:C] + _dot(a_ak[p], v_h[p])).astype(BF16) for p in prs]
    u = [_dot(t[p], by_head(rhs[p])).astype(BF16) for p in prs]
    y = [sr[p][C:] + _dot(r_bk[p], jnp.concatenate([by_head(u[p]), v_h[p]], axis=0)) for p in prs]
    for p in prs:
        y_ref[p // pairs, :, lanes[p % pairs]] = y[p]
        upd = _dot_tn(jnp.concatenate([u[p], v_all[:, lanes[p]]], axis=0), bkh_all[:, lanes[p]])
        s_scr[p] = s_scr[p] * s_decay[:, lanes[p]] + jnp.where(blockdiag, upd, 0.0)

    @pl.when(c == pl.num_programs(3) - 1)
    def _():
        for b in range(nb):
            sfin_ref[b] = s_scr[b * pairs:(b + 1) * pairs]


def _wkv_scan(r, v, kk, lw, kd, a, s0, *, chunk=64, pairs=16, nb=2):
    B, T, D = r.shape
    pairs = min(pairs, D // LANES)
    nb = nb if B % nb == 0 else 1
    lw_lanes = pairs * LANES
    assert T % chunk == 0 and D % lw_lanes == 0
    nc = T // chunk
    ng = D // lw_lanes

    def cidx(z, c):
        return c + z * (nc - 1 - 2 * c)

    tok_spec = pl.BlockSpec((nb, chunk, lw_lanes), lambda z, b, g, c: (b, cidx(z, c), g))
    dir_spec = pl.BlockSpec((None, nb, chunk, lw_lanes), lambda z, b, g, c: (z, b, cidx(z, c), g))
    st_spec = pl.BlockSpec((None, nb, pairs, LANES, LANES), lambda z, b, g, c: (z, b, g, 0, 0))
    y, s_fin = pl.pallas_call(
        functools.partial(_wkv_kernel, chunk=chunk, pairs=pairs, nb=nb),
        grid=(2, B // nb, ng, nc),
        in_specs=[tok_spec, tok_spec, tok_spec, dir_spec, dir_spec, dir_spec, st_spec],
        out_specs=[dir_spec, st_spec],
        out_shape=[jax.ShapeDtypeStruct((2, B, T, D), F32),
                   jax.ShapeDtypeStruct(s0.shape, F32)],
        scratch_shapes=[pltpu.VMEM((nb * pairs, LANES, LANES), F32)],
        compiler_params=_cparams(("arbitrary", "arbitrary", "arbitrary", "arbitrary")),
        name="wkv_scan",
    )(r, v, kk, lw, kd, a, s0)
    return y, s_fin


def _rwkv_prep_kernel(*refs, seq_len, tm, has_vres):
    (x_ref, xp_ref, xn_ref, mod_ref, gpre_ref, mix_ref, wr_ref, wk_ref, wv_ref,
     w1_ref, w2_ref, w0_ref, a1_ref, a2_ref, a0_ref, g1_ref, g2_ref, kk_ref, ka_ref) = refs[:19]
    pos = 19
    if has_vres:
        v1_ref, v2_ref, v0_ref, vf_ref = refs[pos:pos + 4]
        pos += 4
    r_ref, v_ref, kkn_ref, g_ref, lw_ref, kd_ref, a_ref = refs[pos:pos + 7]
    pos += 7
    xmix_scr, tw_scr, ta_scr, tg_scr = refs[pos:pos + 4]
    tv_scr = refs[pos + 4] if has_vres else None
    i = pl.program_id(0)
    n = pl.program_id(1)

    @pl.when(n == 0)
    def _():
        h = _modulated(x_ref[...], mod_ref, gpre_ref, 1)
        t0 = i * tm
        hp = _modulated(xp_ref[...], mod_ref, gpre_ref, 1)[SUBLANES - 1:SUBLANES, :]
        hn = _modulated(xn_ref[...], mod_ref, gpre_ref, 1)[0:1, :]
        hp = jnp.where(t0 % seq_len == 0, 0.0, hp)
        hn = jnp.where((t0 + tm) % seq_len == 0, 0.0, hn)
        rowid = lax.broadcasted_iota(jnp.int32, (tm, 1), 0)
        prev = jnp.where(rowid == 0, hp, pltpu.roll(h, 1, 0))
        nxt = jnp.where(rowid == tm - 1, hn, pltpu.roll(h, tm - 1, 0))
        xx = 0.5 * (prev + nxt) - h
        mixed = lambda m: (h + xx * mix_ref[m:m + 1, :]).astype(BF16)
        xmix_scr[0] = mixed(0)
        xmix_scr[1] = mixed(2)
        xv = mixed(3)
        xmix_scr[2] = xv
        tw_scr[...] = jnp.tanh(_dot(mixed(1), w1_ref[...])).astype(BF16)
        ta_scr[...] = _dot(mixed(4), a1_ref[...]).astype(BF16)
        tg_scr[...] = jax.nn.sigmoid(_dot(mixed(5), g1_ref[...])).astype(BF16)
        if has_vres:
            tv_scr[...] = _dot(xv, v1_ref[...]).astype(BF16)

    lp = w2_ref.shape[1]
    la = a2_ref.shape[1]
    a_dirs = []
    for zz in range(2):
        lora_w = _dot(tw_scr[:, zz * lp:(zz + 1) * lp], w2_ref[zz])
        lw_ref[zz] = -(EXP_NEG_HALF * jax.nn.sigmoid(w0_ref[zz:zz + 1, :] + lora_w))
        a = jax.nn.sigmoid(a0_ref[zz:zz + 1, :] + _dot(ta_scr[:, zz * la:(zz + 1) * la], a2_ref[zz]))
        a_ref[zz] = a
        a_dirs.append(1 + (a - 1) * ka_ref[...])
    g_ref[...] = _dot(tg_scr[...], g2_ref[...])
    v = _dot(xmix_scr[2], wv_ref[n])
    if has_vres:
        v = v + (vf_ref[...] - v) * jax.nn.sigmoid(v0_ref[...] + _dot(tv_scr[...], v2_ref[...]))
    v_ref[...] = v
    k = _dot(xmix_scr[1], wk_ref[n])
    for zz in range(2):
        kd_ref[zz] = k * a_dirs[zz]
    kk = k * kk_ref[...]
    ssq = _dot_exact_rhs01(kk * kk, _head_ones(kk.shape[1]))
    kkn_ref[...] = kk * lax.rsqrt(jnp.maximum(ssq, L2_EPS_SQ))
    r_ref[...] = _dot(xmix_scr[0], wr_ref[n])


def _rwkv_prep(x, mod, grp, g_pre, p, v_first, seq_len, *, tm=512):
    M, D = x.shape
    tm = _tile(seq_len, tm, SUBLANES)
    tn = p["w_r"].shape[2]
    base, rpg = grp
    wtile = pl.BlockSpec((D // tn, D, tn), lambda i, n: (0, 0, 0), pipeline_mode=pl.Buffered(1))
    has_vres = p["v_res"] is not None
    nb = M // SUBLANES
    row = lambda i, n: (i, 0)
    col = lambda i, n: (0, n)
    full = lambda i, n: (0, 0)
    col3 = lambda i, n: (0, 0, n)
    tile = lambda i, n: (i, n)
    tile3 = lambda i, n: (0, i, n)
    lw1, la1, lg1 = p["w1"].shape[1], p["a1"].shape[1], p["g1"].shape[1]
    in_specs = [
        pl.BlockSpec((tm, D), row, pipeline_mode=pl.Buffered(1)),
        pl.BlockSpec((SUBLANES, D), lambda i, n: (jnp.maximum(i * (tm // SUBLANES) - 1, 0), 0)),
        pl.BlockSpec((SUBLANES, D), lambda i, n: (jnp.minimum((i + 1) * (tm // SUBLANES), nb - 1), 0)),
        pl.BlockSpec((None, N_MOD, D), _mod_spec(base, rpg, tm)),
        pl.BlockSpec((1, D), full),
        pl.BlockSpec((6, D), full),
        wtile, wtile, wtile,
        pl.BlockSpec((D, lw1), full), pl.BlockSpec((2, lw1 // 2, tn), col3), pl.BlockSpec((2, tn), col),
        pl.BlockSpec((D, la1), full), pl.BlockSpec((2, la1 // 2, tn), col3), pl.BlockSpec((2, tn), col),
        pl.BlockSpec((D, lg1), full), pl.BlockSpec((lg1, tn), col),
        pl.BlockSpec((1, tn), col), pl.BlockSpec((1, tn), col),
    ]
    args = [x, x, x, mod, g_pre.reshape(1, D), p["mix"], p["w_r"], p["w_k"], p["w_v"],
            p["w1"], p["w2"], p["w0"], p["a1"], p["a2"], p["a0"], p["g1"], p["g2"][0],
            p["k_k"].reshape(1, D), p["k_a"].reshape(1, D)]
    scratch = [pltpu.VMEM((3, tm, D), BF16), pltpu.VMEM((tm, lw1), BF16),
               pltpu.VMEM((tm, la1), BF16), pltpu.VMEM((tm, lg1), BF16)]
    if has_vres:
        v0, v1, v2 = p["v_res"]
        lv1 = v1.shape[1]
        in_specs += [pl.BlockSpec((D, lv1), full), pl.BlockSpec((lv1, tn), col),
                     pl.BlockSpec((1, tn), col), pl.BlockSpec((tm, tn), tile)]
        args += [v1, v2[0], v0.reshape(1, D), v_first]
        scratch.append(pltpu.VMEM((tm, lv1), BF16))
    one = jax.ShapeDtypeStruct((M, D), F32)
    two = jax.ShapeDtypeStruct((2, M, D), F32)
    r, v, kk, g, lw, kd, a = pl.pallas_call(
        functools.partial(_rwkv_prep_kernel, seq_len=seq_len, tm=tm, has_vres=has_vres),
        grid=(M // tm, D // tn),
        in_specs=in_specs,
        out_specs=[pl.BlockSpec((tm, tn), tile)] * 4 + [pl.BlockSpec((2, tm, tn), tile3)] * 3,
        out_shape=[one] * 4 + [two] * 3,
        scratch_shapes=scratch,
        compiler_params=_cparams(("arbitrary", "arbitrary"), 63 * 1024 * 1024),
        name="rwkv_prep",
    )(*args)
    return dict(r=r, v=v, kk=kk, g=g, lw=lw, kd=kd, a=a)


def _rwkv_out_kernel(y_ref, r_ref, kd_ref, v_ref, g_ref, rk_ref, gnw_ref, gnb_ref, w_ref, x_ref, mod_ref,
                     gpost_ref, o_ref, pre_scr, *, slab):
    D = x_ref.shape[1]
    ones = _head_ones(slab)
    for j in range(D // slab):
        sl = slice(j * slab, (j + 1) * slab)
        wkv = y_ref[0, :, sl] + y_ref[1, :, sl]
        mu = _dot_exact_rhs01(wkv, ones) * (1.0 / HEAD)
        cen = wkv - mu
        var = _dot_exact_rhs01(cen * cen, ones) * (1.0 / HEAD)
        o = cen * lax.rsqrt(var + RWKV_GN_EPS) * gnw_ref[:, sl] + gnb_ref[:, sl]
        coef = _dot_exact_rhs01(r_ref[:, sl] * (kd_ref[0, :, sl] + kd_ref[1, :, sl]) * rk_ref[:, sl], ones)
        pre_scr[:, sl] = ((o + coef * v_ref[:, sl]) * g_ref[:, sl]).astype(BF16)
    y = _dot(pre_scr[...], w_ref[...])
    o_ref[...] = x_ref[...] + mod_ref[5:6, :] * _rms(y, gpost_ref[...])


def _rwkv_out(y, q, p, w_o, x, mod, grp, g_post, *, tm=256, slab=256):
    M, D = x.shape
    tm = _tile(M, tm, SUBLANES)
    slab = _tile(D, slab, LANES)
    base, rpg = grp
    row = lambda i: (i, 0)
    row3 = lambda i: (0, i, 0)
    full = lambda i: (0, 0)
    return pl.pallas_call(
        functools.partial(_rwkv_out_kernel, slab=slab),
        grid=(M // tm,),
        in_specs=[pl.BlockSpec((2, tm, D), row3), pl.BlockSpec((tm, D), row), pl.BlockSpec((2, tm, D), row3),
                  pl.BlockSpec((tm, D), row), pl.BlockSpec((tm, D), row),
                  pl.BlockSpec((1, D), full), pl.BlockSpec((1, D), full), pl.BlockSpec((1, D), full),
                  pl.BlockSpec((D, D), full, pipeline_mode=pl.Buffered(1)), pl.BlockSpec((tm, D), row),
                  pl.BlockSpec((None, N_MOD, D), _mod_spec(base, rpg, tm)), pl.BlockSpec((1, D), full)],
        out_specs=pl.BlockSpec((tm, D), row),
        out_shape=jax.ShapeDtypeStruct((M, D), F32),
        scratch_shapes=[pltpu.VMEM((tm, D), BF16)],
        compiler_params=_cparams(("arbitrary",)),
        name="rwkv_out",
    )(y, q["r"], q["kd"], q["v"], q["g"], p["r_k"].reshape(1, D), p["gn_w"].reshape(1, D),
      p["gn_b"].reshape(1, D), w_o, x, mod, g_post.reshape(1, D))


def _adaln_kernel(c_ref, w_ref, b_ref, o_ref):
    cv = c_ref[...]
    cv = cv * jax.nn.sigmoid(cv)
    o_ref[...] = _dot(cv.astype(BF16), w_ref[...].astype(BF16)) + b_ref[...]


def _adaln(cvec, ada_w, ada_b, *, bn=2048):
    depth, D, N = ada_w.shape
    bn = _tile(N, bn, LANES)
    return pl.pallas_call(
        _adaln_kernel,
        grid=(depth, N // bn),
        in_specs=[pl.BlockSpec((8, D), lambda i, n: (0, 0)),
                  pl.BlockSpec((None, D, bn), lambda i, n: (i, 0, n)),
                  pl.BlockSpec((None, 1, bn), lambda i, n: (i, 0, n))],
        out_specs=pl.BlockSpec((None, 8, bn), lambda i, n: (i, 0, n)),
        out_shape=jax.ShapeDtypeStruct((depth, 8, N), F32),
        compiler_params=_cparams(("arbitrary", "arbitrary")),
        name="adaln",
    )(cvec, ada_w, ada_b.reshape(depth, 1, N))


def _ffn_kernel(x_ref, mod_ref, gpre_ref, gpost_ref, wg_ref, wu_ref, wd_ref, o_ref, *rest, slot, emit_bf16):
    h_scr = rest[-1]
    f = pl.program_id(1)

    @pl.when(f == 0)
    def _():
        _modulated_rows(x_ref, mod_ref, gpre_ref, slot, h_scr)
        o_ref[...] = jnp.zeros_like(o_ref)

    wg, wu, wd = wg_ref[...], wu_ref[...], wd_ref[...]
    if emit_bf16:
        wg, wu, wd = wg.astype(BF16), wu.astype(BF16), wd.astype(BF16)
        for ref, w in zip(rest[:3], (wg, wu, wd)):
            ref[...] = w
    h = h_scr[...]
    g = _dot(h, wg)
    u = _dot(h, wu)
    a = (g * jax.nn.sigmoid(g)) * u
    o_ref[...] += _dot(a.astype(BF16), wd)

    @pl.when(f == pl.num_programs(1) - 1)
    def _():
        gate = mod_ref[3 * slot + 2:3 * slot + 3, :]
        _gated_residual_rows(x_ref, o_ref, MACARON_WEIGHT * gate, gpost_ref, o_ref)


def _ffn(x, mod, grp, slot, g_pre, g_post, w_in, w_up, w_down, layer=None, which=None, *, tm=1024, tf=512):
    M, D = x.shape
    F = w_in.shape[-1]
    emit_bf16 = layer is not None
    tm = _tile(min(M, grp[1]), tm, SUBLANES)
    tf = _tile(F, tf // 2 if emit_bf16 else tf, LANES)
    base, rpg = grp
    if emit_bf16:
        assert M == tm
        up_spec = pl.BlockSpec((None, None, D, tf), lambda i, f: (layer, which, 0, f))
        down_spec = pl.BlockSpec((None, None, tf, D), lambda i, f: (layer, which, f, 0))
    else:
        up_spec = pl.BlockSpec((D, tf), lambda i, f: (0, f))
        down_spec = pl.BlockSpec((tf, D), lambda i, f: (f, 0))
    x_spec = pl.BlockSpec((tm, D), lambda i, f: (i, 0))
    out_specs = [x_spec]
    out_shape = [jax.ShapeDtypeStruct((M, D), F32)]
    if emit_bf16:
        x_spec = pl.BlockSpec((tm, D), lambda i, f: (i, 0), pipeline_mode=pl.Buffered(1))
        out_specs += [pl.BlockSpec((D, tf), lambda i, f: (0, f)), pl.BlockSpec((D, tf), lambda i, f: (0, f)),
                      pl.BlockSpec((tf, D), lambda i, f: (f, 0))]
        out_shape += [jax.ShapeDtypeStruct((D, F), BF16), jax.ShapeDtypeStruct((D, F), BF16),
                      jax.ShapeDtypeStruct((F, D), BF16)]
    out = pl.pallas_call(
        functools.partial(_ffn_kernel, slot=slot, emit_bf16=emit_bf16),
        grid=(M // tm, F // tf),
        in_specs=[x_spec,
                  pl.BlockSpec((None, N_MOD, D), _mod_spec(base, rpg, tm)),
                  pl.BlockSpec((1, D), lambda i, f: (0, 0)),
                  pl.BlockSpec((1, D), lambda i, f: (0, 0)),
                  up_spec, up_spec, down_spec],
        out_specs=out_specs,
        out_shape=out_shape,
        scratch_shapes=[pltpu.VMEM((tm, D), BF16)],
        compiler_params=_cparams(("arbitrary", "arbitrary")),
        name="ffn_cast" if emit_bf16 else "ffn",
    )(x, mod, g_pre.reshape(1, D), g_post.reshape(1, D), w_in, w_up, w_down)
    return (out[0], tuple(out[1:])) if emit_bf16 else out[0]


def _normmod_mm_kernel(x_ref, mod_ref, gpre_ref, w_ref, o_ref, h_scr, *, slot):
    @pl.when(pl.program_id(1) == 0)
    def _():
        _modulated_rows(x_ref, mod_ref, gpre_ref, slot, h_scr)

    o_ref[...] = _dot(h_scr[...], w_ref[...]).astype(o_ref.dtype)


def _normmod_mm(x, mod, grp, slot, g_pre, w, out_dtype, *, tm=1024, tn=1024):
    M, D = x.shape
    N = w.shape[1]
    tm = _tile(min(M, grp[1]), tm, SUBLANES)
    tn = _tile(N, tn, LANES)
    base, rpg = grp
    return pl.pallas_call(
        functools.partial(_normmod_mm_kernel, slot=slot),
        grid=(M // tm, N // tn),
        in_specs=[pl.BlockSpec((tm, D), lambda i, n: (i, 0)),
                  pl.BlockSpec((None, N_MOD, D), _mod_spec(base, rpg, tm)),
                  pl.BlockSpec((1, D), lambda i, n: (0, 0)),
                  pl.BlockSpec((D, tn), lambda i, n: (0, n))],
        out_specs=pl.BlockSpec((tm, tn), lambda i, n: (i, n)),
        out_shape=jax.ShapeDtypeStruct((M, N), out_dtype),
        scratch_shapes=[pltpu.VMEM((tm, D), BF16)],
        compiler_params=_cparams(("arbitrary", "arbitrary")),
        name="normmod_mm",
    )(x, mod, g_pre.reshape(1, D), w)


def _outproj_kernel(a_ref, w_ref, x_ref, mod_ref, gpost_ref, o_ref, *, slot):
    y = _dot(a_ref[...], w_ref[...])
    gate = mod_ref[3 * slot + 2:3 * slot + 3, :]
    o_ref[...] = x_ref[...] + gate * _rms(y, gpost_ref[...])


def _outproj(a, w, x, mod, grp, slot, g_post, *, tm=512):
    M, D = x.shape
    K = a.shape[1]
    tm = _tile(M, tm, SUBLANES)
    base, rpg = grp
    return pl.pallas_call(
        functools.partial(_outproj_kernel, slot=slot),
        grid=(M // tm,),
        in_specs=[pl.BlockSpec((tm, K), lambda i: (i, 0)),
                  pl.BlockSpec((K, D), lambda i: (0, 0), pipeline_mode=pl.Buffered(1)),
                  pl.BlockSpec((tm, D), lambda i: (i, 0)),
                  pl.BlockSpec((None, N_MOD, D), _mod_spec(base, rpg, tm)),
                  pl.BlockSpec((1, D), lambda i: (0, 0))],
        out_specs=pl.BlockSpec((tm, D), lambda i: (i, 0)),
        out_shape=jax.ShapeDtypeStruct((M, D), F32),
        compiler_params=_cparams(("arbitrary",)),
        name="outproj",
    )(a, w, x, mod, g_post.reshape(1, D))


def _nat_kernel(q_ref, k_ref, v_ref, kc_ref, vc_ref, bias_ref, o_ref, *, rows, group):
    scale = HEAD ** -0.5
    n_lat = WIN_H * GRID_W
    lane = lax.broadcasted_iota(jnp.int32, (1, LANES), 1)
    head_masks = (lane < HEAD, lane >= HEAD)
    qcol = lax.broadcasted_iota(jnp.int32, (2 * GRID_W, n_lat), 0) % GRID_W
    kcol = lax.broadcasted_iota(jnp.int32, (2 * GRID_W, n_lat), 1) % GRID_W
    cstart = jnp.clip(qcol - WIN_W // 2, 0, GRID_W - WIN_W)
    col_ok = (kcol >= cstart) & (kcol < cstart + WIN_W)
    kc = kc_ref[...]
    vc = vc_ref[...]
    zero = jnp.zeros((), BF16)

    def step(i, carry):
        chains = []
        for rr in range(group):
            r = i * group + rr
            r0 = jnp.clip(r - WIN_H // 2, 0, rows - WIN_H)
            q = q_ref[pl.ds(pl.multiple_of(r * GRID_W, GRID_W), GRID_W), :] * scale
            q2 = jnp.concatenate([jnp.where(head_masks[0], q, zero), jnp.where(head_masks[1], q, zero)], axis=0)
            kw = k_ref[pl.ds(pl.multiple_of(r0 * GRID_W, GRID_W), n_lat), :]
            vw = v_ref[pl.ds(pl.multiple_of(r0 * GRID_W, GRID_W), n_lat), :]
            chains.append((r, r0, q2, kw, vw))
        s = [_dot_nt(q2, kw) for (_, _, q2, kw, _) in chains]
        sc = [_dot_nt(q2, kc) for (_, _, q2, _, _) in chains]
        s = [jnp.where(col_ok, si + jnp.concatenate(
                [jnp.concatenate([bias_ref[h, 2 * jj - (r - r0) + WIN_H - 1] for jj in range(WIN_H // 2)], axis=1)
                 for h in range(2)], axis=0), MASK_VALUE)
             for si, (r, r0, _, _, _) in zip(s, chains)]
        m = [jnp.maximum(jnp.max(si, axis=-1, keepdims=True), jnp.max(ci, axis=-1, keepdims=True))
             for si, ci in zip(s, sc)]
        e = [jnp.exp(si - mi) for si, mi in zip(s, m)]
        ec = [jnp.exp(ci - mi) for ci, mi in zip(sc, m)]
        den = [jnp.sum(ei, axis=-1, keepdims=True) + jnp.sum(ci, axis=-1, keepdims=True) for ei, ci in zip(e, ec)]
        o = [(_dot(ei.astype(BF16), vw) + _dot(ci.astype(BF16), vc)) / di
             for ei, ci, di, (_, _, _, _, vw) in zip(e, ec, den, chains)]
        for rr in range(group):
            r = i * group + rr
            o_ref[pl.ds(pl.multiple_of(r * GRID_W, GRID_W), GRID_W), :] = jnp.where(
                head_masks[0], o[rr][:GRID_W], o[rr][GRID_W:]).astype(o_ref.dtype)
        return carry

    lax.fori_loop(0, rows // group, step, 0)


def _nat_bias_table(rpb):
    edge = GRID_W - WIN_W
    padded = jnp.pad(rpb, ((0, 0), (0, 0), (edge, edge)), mode="edge")
    rows = jnp.stack([padded[:, :, GRID_W - 1 - q:2 * GRID_W - 1 - q] for q in range(GRID_W)], axis=2)
    return jnp.concatenate([rows[:, :-1], rows[:, 1:]], axis=-1).astype(F32)


def _nat_attention(qkv, qkv_c, bias, *, group=4):
    B, T, D3 = qkv.shape
    D = D3 // 3
    C = qkv_c.shape[1]
    nd = D // LANES
    rows = T // GRID_W
    assert rows >= WIN_H and rows % group == 0
    return pl.pallas_call(
        functools.partial(_nat_kernel, rows=rows, group=group),
        grid=(B, nd),
        in_specs=[pl.BlockSpec((None, T, LANES), lambda b, p: (b, 0, p)),
                  pl.BlockSpec((None, T, LANES), lambda b, p: (b, 0, nd + p)),
                  pl.BlockSpec((None, T, LANES), lambda b, p: (b, 0, 2 * nd + p)),
                  pl.BlockSpec((None, C, LANES), lambda b, p: (b, 0, nd + p)),
                  pl.BlockSpec((None, C, LANES), lambda b, p: (b, 0, 2 * nd + p)),
                  pl.BlockSpec((2, 2 * WIN_H - 2, GRID_W, 2 * GRID_W), lambda b, p: (p, 0, 0, 0))],
        out_specs=pl.BlockSpec((None, T, LANES), lambda b, p: (b, 0, p)),
        out_shape=jax.ShapeDtypeStruct((B, T, D), BF16),
        compiler_params=_cparams(("arbitrary", "arbitrary")),
        name="nat_attention",
    )(qkv, qkv, qkv, qkv_c, qkv_c, bias)


def _ctx_attn_kernel(q_ref, k_ref, v_ref, o_ref):
    scale = HEAD ** -0.5
    lane = lax.broadcasted_iota(jnp.int32, (1, LANES), 1)
    head_masks = (lane < HEAD, lane >= HEAD)
    q = q_ref[...] * scale
    k = k_ref[...]
    v = v_ref[...]
    zero = jnp.zeros((), BF16)
    outs = []
    for h in range(2):
        s = _dot_nt(jnp.where(head_masks[h], q, zero), k)
        e = jnp.exp(s - jnp.max(s, axis=-1, keepdims=True))
        outs.append(_dot(e.astype(BF16), v) / jnp.sum(e, axis=-1, keepdims=True))
    o_ref[...] = jnp.where(head_masks[0], outs[0], outs[1]).astype(o_ref.dtype)


def _ctx_attention(qkv_c):
    B, C, D3 = qkv_c.shape
    D = D3 // 3
    nd = D // LANES
    return pl.pallas_call(
        _ctx_attn_kernel,
        grid=(B, nd),
        in_specs=[pl.BlockSpec((None, C, LANES), lambda b, p: (b, 0, p)),
                  pl.BlockSpec((None, C, LANES), lambda b, p: (b, 0, nd + p)),
                  pl.BlockSpec((None, C, LANES), lambda b, p: (b, 0, 2 * nd + p))],
        out_specs=pl.BlockSpec((None, C, LANES), lambda b, p: (b, 0, p)),
        out_shape=jax.ShapeDtypeStruct((B, C, D), BF16),
        compiler_params=_cparams(("arbitrary", "arbitrary")),
        name="ctx_attention",
    )(qkv_c, qkv_c, qkv_c)


def _pad_to(w, axis, mult=LANES):
    n = w.shape[axis]
    pad = (-n) % mult
    if pad == 0:
        return w
    widths = [(0, 0)] * w.ndim
    widths[axis] = (0, pad)
    return jnp.pad(w, widths)


def _col_tiles(w, tn=2 * LANES):
    K, N = w.shape
    tn = _tile(N, tn, LANES)
    return jnp.swapaxes(w.astype(BF16).reshape(K, N // tn, tn), 0, 1)


def _lora_in(w):
    w = _pad_to(w, 2)
    return jnp.concatenate(list(w), axis=1).astype(BF16)


def _lora_out(w):
    return _pad_to(w, 1).astype(BF16)


def _scan_inputs(q, B, L, D):
    three = lambda t: t.reshape(B, L, D)
    four = lambda t: t.reshape(2, B, L, D)
    return three(q["r"]), three(q["v"]), three(q["kk"]), four(q["lw"]), four(q["kd"]), four(q["a"])


def kernel(x, c, ctx, c_ctx, ada_w, ada_b, norm_pre, norm_post, ffn_w_gate, ffn_w_up, ffn_w_down, rwkv_mix, rwkv_w_r, rwkv_w_k, rwkv_w_v, rwkv_w_o, rwkv_w0, rwkv_w1, rwkv_w2, rwkv_a0, rwkv_a1, rwkv_a2, rwkv_v0, rwkv_v1, rwkv_v2, rwkv_k_k, rwkv_k_a, rwkv_r_k, rwkv_g1, rwkv_g2, rwkv_gn_w, rwkv_gn_b, nat_w_qkv, nat_w_o, nat_rpb):
    B, T, D = x.shape
    C = ctx.shape[1]
    depth = ada_w.shape[0]
    assert B + 1 <= 8 and D % (2 * LANES) == 0
    cvec = jnp.zeros((8, D), F32).at[:B].set(c).at[B].set(c_ctx)
    mods = _adaln(cvec, ada_w, ada_b).reshape(depth, 8, N_MOD, D)
    ffn_w = (ffn_w_gate, ffn_w_up, ffn_w_down)
    grp_l, grp_c = (0, T), (B, B * C)
    xl = x.reshape(B * T, D)
    xc = ctx.reshape(B * C, D)
    vf_l = vf_c = None

    def ffn_both(xl, xc, mod, i, slot, which, with_ctx):
        args = (slot, norm_pre[i, slot], norm_post[i, slot])
        if with_ctx and B * C <= 1024:
            xc, w = _ffn(xc, mod, grp_c, *args, *ffn_w, layer=i, which=which)
        else:
            w = tuple(t[i, which].astype(BF16) for t in ffn_w)
            if with_ctx:
                xc = _ffn(xc, mod, grp_c, *args, *w)
        return _ffn(xl, mod, grp_l, *args, *w), xc

    for i in range(depth):
        last = i == depth - 1
        j = i // 2
        mod = mods[i]
        xl, xc = ffn_both(xl, xc, mod, i, 0, 0, True)
        if i % 2 == 0:
            p = dict(
                mix=rwkv_mix[j], w_r=_col_tiles(rwkv_w_r[j]), w_k=_col_tiles(rwkv_w_k[j]),
                w_v=_col_tiles(rwkv_w_v[j]), w0=rwkv_w0[j], w1=_lora_in(rwkv_w1[j]), w2=_lora_out(rwkv_w2[j]),
                a0=rwkv_a0[j], a1=_lora_in(rwkv_a1[j]), a2=_lora_out(rwkv_a2[j]),
                k_k=rwkv_k_k[j], k_a=rwkv_k_a[j], r_k=rwkv_r_k[j].reshape(D),
                g1=_lora_in(rwkv_g1[j][None]), g2=_lora_out(rwkv_g2[j][None]),
                gn_w=rwkv_gn_w[j], gn_b=rwkv_gn_b[j],
                v_res=None if j == 0 else (rwkv_v0[j - 1], _lora_in(rwkv_v1[j - 1][None]), _lora_out(rwkv_v2[j - 1][None])))
            q_c = _rwkv_prep(xc, mod, grp_c, norm_pre[i, 1], p, vf_c, C)
            q_l = _rwkv_prep(xl, mod, grp_l, norm_pre[i, 1], p, vf_l, T)
            if j == 0:
                vf_l, vf_c = q_l["v"], q_c["v"]
            s0 = jnp.zeros((2, B, D // LANES, LANES, LANES), F32)
            y_c, s_c = _wkv_scan(*_scan_inputs(q_c, B, C, D), s0)
            y_l, _ = _wkv_scan(*_scan_inputs(q_l, B, T, D), s_c)
            w_o = rwkv_w_o[j].astype(BF16)
            xl = _rwkv_out(y_l.reshape(2, B * T, D), q_l, p, w_o, xl, mod, grp_l, norm_post[i, 1])
            if not last:
                xc = _rwkv_out(y_c.reshape(2, B * C, D), q_c, p, w_o, xc, mod, grp_c, norm_post[i, 1])
        else:
            w_qkv = nat_w_qkv[j].astype(BF16)
            w_o = nat_w_o[j].astype(BF16)
            qkv_l = _normmod_mm(xl, mod, grp_l, 1, norm_pre[i, 1], w_qkv, BF16).reshape(B, T, 3 * D)
            qkv_c = _normmod_mm(xc, mod, grp_c, 1, norm_pre[i, 1], w_qkv, BF16).reshape(B, C, 3 * D)
            a_l = _nat_attention(qkv_l, qkv_c, _nat_bias_table(nat_rpb[j])).reshape(B * T, D)
            xl = _outproj(a_l, w_o, xl, mod, grp_l, 1, norm_post[i, 1])
            if not last:
                a_c = _ctx_attention(qkv_c).reshape(B * C, D)
                xc = _outproj(a_c, w_o, xc, mod, grp_c, 1, norm_post[i, 1])
        xl, xc = ffn_both(xl, xc, mod, i, 2, 1, not last)
    return xl.reshape(B, T, D)
```

```python
import functools
import math

import jax
import jax.numpy as jnp
from jax import lax
from jax.experimental import pallas as pl
from jax.experimental.pallas import tpu as pltpu

F32 = jnp.float32
BF16 = jnp.bfloat16

LANES = 128
SUBLANES = 8
ROW_SLAB = 32
HEAD = 64
N_MOD = 9
MACARON_WEIGHT = 0.5
RMS_EPS = 1e-6
RWKV_GN_EPS = 64e-5
L2_EPS_SQ = 1e-24
GRID_W = 64
WIN_H = 8
WIN_W = 16
MASK_VALUE = -1e30
EXP_NEG_HALF = math.exp(-0.5)
VMEM_LIMIT = 56 * 1024 * 1024


def _cparams(sem, vmem_limit=VMEM_LIMIT):
    return pltpu.CompilerParams(dimension_semantics=sem, vmem_limit_bytes=vmem_limit)


def _dot(a, b):
    return jnp.dot(a, b, preferred_element_type=F32)


def _dot_nt(a, b):
    return lax.dot_general(a, b, (((1,), (1,)), ((), ())), preferred_element_type=F32)


def _dot_tn(a, b):
    return lax.dot_general(a, b, (((0,), (0,)), ((), ())), preferred_element_type=F32)


def _split3(x):
    hi = x.astype(BF16)
    r1 = x - hi.astype(F32)
    mid = r1.astype(BF16)
    lo = (r1 - mid.astype(F32)).astype(BF16)
    return hi, mid, lo


def _dot_exact_rhs01(x, m01):
    n = x.shape[0]
    parts = _dot(jnp.concatenate(_split3(x), axis=0), m01)
    return parts[:n] + parts[n:2 * n] + parts[2 * n:]


def _head_ones(n):
    r = lax.broadcasted_iota(jnp.int32, (n, n), 0) // HEAD
    c = lax.broadcasted_iota(jnp.int32, (n, n), 1) // HEAD
    return (r == c).astype(BF16)


def _rms(x, g):
    return x * lax.rsqrt(jnp.mean(x * x, axis=-1, keepdims=True) + RMS_EPS) * g


def _modulated(x, mod_ref, gpre_ref, slot):
    shift = mod_ref[3 * slot:3 * slot + 1, :]
    scale = mod_ref[3 * slot + 1:3 * slot + 2, :]
    return _rms(x, gpre_ref[...]) * (1 + scale) + shift


def _tile(n, want, unit):
    if n <= want:
        return n
    t = (want // unit) * unit
    while n % t:
        t -= unit
    return t


def _row_slabs(rows, fn):
    slab = _tile(rows, ROW_SLAB, 2 * SUBLANES)

    def body(i, carry):
        fn(pl.ds(pl.multiple_of(i * slab, slab), slab))
        return carry

    lax.fori_loop(0, rows // slab, body, 0, unroll=min(8, rows // slab))


def _modulated_rows(x_ref, mod_ref, gpre_ref, slot, dst_ref):
    shift = mod_ref[3 * slot:3 * slot + 1, :]
    scale1 = 1 + mod_ref[3 * slot + 1:3 * slot + 2, :]
    g = gpre_ref[...]

    def slab(sl):
        dst_ref[sl, :] = (_rms(x_ref[sl, :], g) * scale1 + shift).astype(dst_ref.dtype)

    _row_slabs(x_ref.shape[0], slab)


def _gated_residual_rows(x_ref, y_ref, wgate, gpost_ref, o_ref):
    g = gpost_ref[...]
    rows = x_ref.shape[0]
    slab = _tile(rows, ROW_SLAB, 2 * SUBLANES)
    for i in range(rows // slab):
        sl = slice(i * slab, (i + 1) * slab)
        o_ref[sl, :] = x_ref[sl, :] + wgate * _rms(y_ref[sl, :], g)


def _mod_spec(base, rows_per_group, tm):
    assert rows_per_group % tm == 0
    return lambda i, *_: (base + (i * tm) // rows_per_group, 0, 0)


def _wkv_kernel(r_ref, v_ref, kk_ref, lw_ref, kd_ref, a_ref, s0_ref, y_ref, sfin_ref, s_scr,
                *, chunk, pairs, nb):
    z = pl.program_id(0)
    c = pl.program_id(3)
    C = chunk
    side_by_side = lambda ref: jnp.concatenate([ref[b] for b in range(nb)], axis=1)

    @pl.when(c == 0)
    def _():
        for b in range(nb):
            s_scr[b * pairs:(b + 1) * pairs] = s0_ref[b]

    row = lax.broadcasted_iota(jnp.int32, (C, 2 * C), 0)
    pcol = lax.broadcasted_iota(jnp.int32, (C, 2 * C), 1)
    col = pcol % C
    d = (row - col) * (1 - 2 * z)
    strict = d > 0
    incl = d >= 0
    half_masks = (pcol < C, pcol >= C)
    lane = lax.broadcasted_iota(jnp.int32, (1, LANES), 1)
    head_masks = (lane < HEAD, lane >= HEAD)
    srow = lax.broadcasted_iota(jnp.int32, (LANES, LANES), 0)
    scol = lax.broadcasted_iota(jnp.int32, (LANES, LANES), 1)
    blockdiag = (srow < HEAD) == (scol < HEAD)
    eye = (row == col).astype(BF16)
    levels = []
    s = 1
    while s < C:
        levels.append((row // (2 * s) == col // (2 * s)) & (row // s != col // s))
        s *= 2

    def blockdiag2(m):
        zero = jnp.zeros_like(m)
        return jnp.concatenate([jnp.where(half_masks[0], m, zero), jnp.where(half_masks[1], m, zero)], axis=0)

    def by_head(x):
        zero = jnp.zeros_like(x)
        return jnp.concatenate([jnp.where(head_masks[0], x, zero), jnp.where(head_masks[1], x, zero)], axis=0)

    lw = side_by_side(lw_ref)
    kk = side_by_side(kk_ref)
    kd = side_by_side(kd_ref)
    L = lw
    rowid = lax.broadcasted_iota(jnp.int32, (C, 1), 0)
    s = 1
    while s < C:
        L = L + jnp.where(rowid >= s, pltpu.roll(L, s, 0), 0.0)
        s *= 2
    l_end = L[C - 1:C, :]
    L = jnp.where(z == 0, L, l_end - L + lw)
    lm = 0.5 * l_end
    e_m = jnp.exp(-lm)
    e_p = jnp.exp(lm)
    e_sh = jnp.exp(lm - L)
    at_t = -kk * jnp.exp(L - lw)
    rt_t = side_by_side(r_ref) * jnp.exp(L)
    sr_all = jnp.concatenate([at_t, rt_t], axis=0).astype(BF16)
    at_sh = at_t * e_m
    rt_sh = rt_t * e_m
    bt = (kk * side_by_side(a_ref)) * e_sh
    kt = kd * e_sh
    bk_all = jnp.concatenate([bt, kt], axis=0).astype(BF16)
    bkh_all = jnp.concatenate([bt * e_p, kt * e_p], axis=0).astype(BF16)
    v_all = side_by_side(v_ref).astype(BF16)
    s_decay = jnp.exp(l_end)

    prs = range(nb * pairs)
    lanes = [slice(p * LANES, (p + 1) * LANES) for p in prs]
    sr = [_dot_nt(sr_all[:, lanes[p]], s_scr[p].astype(BF16)) for p in prs]
    v_h = [by_head(v_all[:, lanes[p]]) for p in prs]
    quads = [_dot_nt(jnp.concatenate([at_sh[:, lanes[p]], rt_sh[:, lanes[p]]], axis=0).astype(BF16),
                     jnp.concatenate([by_head(bk_all[:C, lanes[p]]), by_head(bk_all[C:, lanes[p]])], axis=0))
             for p in prs]
    a_ab = [jnp.where(strict, q[:C, :2 * C], 0.0).astype(BF16) for q in quads]
    a_ak = [jnp.where(strict, q[:C, 2 * C:], 0.0).astype(BF16) for q in quads]
    r_bk = [jnp.concatenate([jnp.where(incl, q[C:, :2 * C], 0.0), jnp.where(incl, q[C:, 2 * C:], 0.0)],
                            axis=1).astype(BF16) for q in quads]
    t = [eye + jnp.where(levels[0], a, jnp.zeros_like(a)) for a in a_ab]
    for lv in levels[1:]:
        x = [jnp.where(lv, a, jnp.zeros_like(a)) for a in a_ab]
        m1 = [_dot(xi, blockdiag2(ti)).astype(BF16) for xi, ti in zip(x, t)]
        m2 = [_dot(ti, blockdiag2(mi)).astype(BF16) for ti, mi in zip(t, m1)]
        t = [ti + mi for ti, mi in zip(t, m2)]
    rhs = [(sr[p][:C] + _dot(a_ak[p], v_h[p])).astype(BF16) for p in prs]
    u = [_dot(t[p], by_head(rhs[p])).astype(BF16) for p in prs]
    y = [sr[p][C:] + _dot(r_bk[p], jnp.concatenate([by_head(u[p]), v_h[p]], axis=0)) for p in prs]
    for p in prs:
        y_ref[p // pairs, :, lanes[p % pairs]] = y[p]
        upd = _dot_tn(jnp.concatenate([u[p], v_all[:, lanes[p]]], axis=0), bkh_all[:, lanes[p]])
        s_scr[p] = s_scr[p] * s_decay[:, lanes[p]] + jnp.where(blockdiag, upd, 0.0)

    @pl.when(c == pl.num_programs(3) - 1)
    def _():
        for b in range(nb):
            sfin_ref[b] = s_scr[b * pairs:(b + 1) * pairs]


def _wkv_scan(r, v, kk, lw, kd, a, s0, *, chunk=64, pairs=16, nb=2):
    B, T, D = r.shape
    pairs = min(pairs, D // LANES)
    nb = nb if B % nb == 0 else 1
    lw_lanes = pairs * LANES
    assert T % chunk == 0 and D % lw_lanes == 0
    nc = T // chunk
    ng = D // lw_lanes

    def cidx(z, c):
        return c + z * (nc - 1 - 2 * c)

    tok_spec = pl.BlockSpec((nb, chunk, lw_lanes), lambda z, b, g, c: (b, cidx(z, c), g))
    dir_spec = pl.BlockSpec((None, nb, chunk, lw_lanes), lambda z, b, g, c: (z, b, cidx(z, c), g))
    st_spec = pl.BlockSpec((None, nb, pairs, LANES, LANES), lambda z, b, g, c: (z, b, g, 0, 0))
    y, s_fin = pl.pallas_call(
        functools.partial(_wkv_kernel, chunk=chunk, pairs=pairs, nb=nb),
        grid=(2, B // nb, ng, nc),
        in_specs=[tok_spec, tok_spec, tok_spec, dir_spec, dir_spec, dir_spec, st_spec],
        out_specs=[dir_spec, st_spec],
        out_shape=[jax.ShapeDtypeStruct((2, B, T, D), F32),
                   jax.ShapeDtypeStruct(s0.shape, F32)],
        scratch_shapes=[pltpu.VMEM((nb * pairs, LANES, LANES), F32)],
        compiler_params=_cparams(("arbitrary", "arbitrary", "arbitrary", "arbitrary")),
        name="wkv_scan",
    )(r, v, kk, lw, kd, a, s0)
    return y, s_fin


def _rwkv_prep_kernel(*refs, seq_len, tm, has_vres):
    (x_ref, xp_ref, xn_ref, mod_ref, gpre_ref, mix_ref, wr_ref, wk_ref, wv_ref,
     w1_ref, w2_ref, w0_ref, a1_ref, a2_ref, a0_ref, g1_ref, g2_ref, kk_ref, ka_ref) = refs[:19]
    pos = 19
    if has_vres:
        v1_ref, v2_ref, v0_ref, vf_ref = refs[pos:pos + 4]
        pos += 4
    r_ref, v_ref, kkn_ref, g_ref, lw_ref, kd_ref, a_ref = refs[pos:pos + 7]
    pos += 7
    xmix_scr, tw_scr, ta_scr, tg_scr = refs[pos:pos + 4]
    tv_scr = refs[pos + 4] if has_vres else None
    i = pl.program_id(0)
    n = pl.program_id(1)

    @pl.when(n == 0)
    def _():
        h = _modulated(x_ref[...], mod_ref, gpre_ref, 1)
        t0 = i * tm
        hp = _modulated(xp_ref[...], mod_ref, gpre_ref, 1)[SUBLANES - 1:SUBLANES, :]
        hn = _modulated(xn_ref[...], mod_ref, gpre_ref, 1)[0:1, :]
        hp = jnp.where(t0 % seq_len == 0, 0.0, hp)
        hn = jnp.where((t0 + tm) % seq_len == 0, 0.0, hn)
        rowid = lax.broadcasted_iota(jnp.int32, (tm, 1), 0)
        prev = jnp.where(rowid == 0, hp, pltpu.roll(h, 1, 0))
        nxt = jnp.where(rowid == tm - 1, hn, pltpu.roll(h, tm - 1, 0))
        xx = 0.5 * (prev + nxt) - h
        mixed = lambda m: (h + xx * mix_ref[m:m + 1, :]).astype(BF16)
        xmix_scr[0] = mixed(0)
        xmix_scr[1] = mixed(2)
        xv = mixed(3)
        xmix_scr[2] = xv
        tw_scr[...] = jnp.tanh(_dot(mixed(1), w1_ref[...])).astype(BF16)
        ta_scr[...] = _dot(mixed(4), a1_ref[...]).astype(BF16)
        tg_scr[...] = jax.nn.sigmoid(_dot(mixed(5), g1_ref[...])).astype(BF16)
        if has_vres:
            tv_scr[...] = _dot(xv, v1_ref[...]).astype(BF16)

    lp = w2_ref.shape[1]
    la = a2_ref.shape[1]
    a_dirs = []
    for zz in range(2):
        lora_w = _dot(tw_scr[:, zz * lp:(zz + 1) * lp], w2_ref[zz])
        lw_ref[zz] = -(EXP_NEG_HALF * jax.nn.sigmoid(w0_ref[zz:zz + 1, :] + lora_w))
        a = jax.nn.sigmoid(a0_ref[zz:zz + 1, :] + _dot(ta_scr[:, zz * la:(zz + 1) * la], a2_ref[zz]))
        a_ref[zz] = a
        a_dirs.append(1 + (a - 1) * ka_ref[...])
    g_ref[...] = _dot(tg_scr[...], g2_ref[...])
    v = _dot(xmix_scr[2], wv_ref[n])
    if has_vres:
        v = v + (vf_ref[...] - v) * jax.nn.sigmoid(v0_ref[...] + _dot(tv_scr[...], v2_ref[...]))
    v_ref[...] = v
    k = _dot(xmix_scr[1], wk_ref[n])
    for zz in range(2):
        kd_ref[zz] = k * a_dirs[zz]
    kk = k * kk_ref[...]
    ssq = _dot_exact_rhs01(kk * kk, _head_ones(kk.shape[1]))
    kkn_ref[...] = kk * lax.rsqrt(jnp.maximum(ssq, L2_EPS_SQ))
    r_ref[...] = _dot(xmix_scr[0], wr_ref[n])


def _rwkv_prep(x, mod, grp, g_pre, p, v_first, seq_len, *, tm=512):
    M, D = x.shape
    tm = _tile(seq_len, tm, SUBLANES)
    tn = p["w_rkv"].shape[3]
    base, rpg = grp
    wtile = lambda which: pl.BlockSpec((None, D // tn, D, tn), lambda i, n: (which, 0, 0, 0),
                                       pipeline_mode=pl.Buffered(1))
    has_vres = p["v_res"] is not None
    nb = M // SUBLANES
    row = lambda i, n: (i, 0)
    col = lambda i, n: (0, n)
    full = lambda i, n: (0, 0)
    col3 = lambda i, n: (0, 0, n)
    tile = lambda i, n: (i, n)
    tile3 = lambda i, n: (0, i, n)
    lw1, la1, lg1 = p["w1"].shape[1], p["a1"].shape[1], p["g1"].shape[1]
    in_specs = [
        pl.BlockSpec((tm, D), row, pipeline_mode=pl.Buffered(1)),
        pl.BlockSpec((SUBLANES, D), lambda i, n: (jnp.maximum(i * (tm // SUBLANES) - 1, 0), 0)),
        pl.BlockSpec((SUBLANES, D), lambda i, n: (jnp.minimum((i + 1) * (tm // SUBLANES), nb - 1), 0)),
        pl.BlockSpec((None, N_MOD, D), _mod_spec(base, rpg, tm)),
        pl.BlockSpec((1, D), full),
        pl.BlockSpec((6, D), full),
        wtile(0), wtile(1), wtile(2),
        pl.BlockSpec((D, lw1), full), pl.BlockSpec((2, lw1 // 2, tn), col3), pl.BlockSpec((2, tn), col),
        pl.BlockSpec((D, la1), full), pl.BlockSpec((2, la1 // 2, tn), col3), pl.BlockSpec((2, tn), col),
        pl.BlockSpec((D, lg1), full), pl.BlockSpec((lg1, tn), col),
        pl.BlockSpec((1, tn), col), pl.BlockSpec((1, tn), col),
    ]
    args = [x, x, x, mod, g_pre.reshape(1, D), p["mix"], p["w_rkv"], p["w_rkv"], p["w_rkv"],
            p["w1"], p["w2"], p["w0"], p["a1"], p["a2"], p["a0"], p["g1"], p["g2"][0],
            p["k_k"].reshape(1, D), p["k_a"].reshape(1, D)]
    scratch = [pltpu.VMEM((3, tm, D), BF16), pltpu.VMEM((tm, lw1), BF16),
               pltpu.VMEM((tm, la1), BF16), pltpu.VMEM((tm, lg1), BF16)]
    if has_vres:
        v0, v1, v2 = p["v_res"]
        lv1 = v1.shape[1]
        in_specs += [pl.BlockSpec((D, lv1), full), pl.BlockSpec((lv1, tn), col),
                     pl.BlockSpec((1, tn), col), pl.BlockSpec((tm, tn), tile)]
        args += [v1, v2[0], v0.reshape(1, D), v_first]
        scratch.append(pltpu.VMEM((tm, lv1), BF16))
    one = jax.ShapeDtypeStruct((M, D), F32)
    two = jax.ShapeDtypeStruct((2, M, D), F32)
    r, v, kk, g, lw, kd, a = pl.pallas_call(
        functools.partial(_rwkv_prep_kernel, seq_len=seq_len, tm=tm, has_vres=has_vres),
        grid=(M // tm, D // tn),
        in_specs=in_specs,
        out_specs=[pl.BlockSpec((tm, tn), tile)] * 4 + [pl.BlockSpec((2, tm, tn), tile3)] * 3,
        out_shape=[one] * 4 + [two] * 3,
        scratch_shapes=scratch,
        compiler_params=_cparams(("arbitrary", "arbitrary"), 63 * 1024 * 1024),
        name="rwkv_prep",
    )(*args)
    return dict(r=r, v=v, kk=kk, g=g, lw=lw, kd=kd, a=a)


def _rwkv_out_kernel(y_ref, r_ref, kd_ref, v_ref, g_ref, rk_ref, gnw_ref, gnb_ref, w_ref, x_ref, mod_ref,
                     gpost_ref, o_ref, pre_scr, *, slab):
    D = x_ref.shape[1]
    ones = _head_ones(slab)
    for j in range(D // slab):
        sl = slice(j * slab, (j + 1) * slab)
        wkv = y_ref[0, :, sl] + y_ref[1, :, sl]
        mu = _dot_exact_rhs01(wkv, ones) * (1.0 / HEAD)
        cen = wkv - mu
        var = _dot_exact_rhs01(cen * cen, ones) * (1.0 / HEAD)
        o = cen * lax.rsqrt(var + RWKV_GN_EPS) * gnw_ref[:, sl] + gnb_ref[:, sl]
        coef = _dot_exact_rhs01(r_ref[:, sl] * (kd_ref[0, :, sl] + kd_ref[1, :, sl]) * rk_ref[:, sl], ones)
        pre_scr[:, sl] = ((o + coef * v_ref[:, sl]) * g_ref[:, sl]).astype(BF16)
    y = _dot(pre_scr[...], w_ref[...])
    o_ref[...] = x_ref[...] + mod_ref[5:6, :] * _rms(y, gpost_ref[...])


def _rwkv_out(y, q, p, w_o, x, mod, grp, g_post, *, tm=256, slab=256):
    M, D = x.shape
    tm = _tile(M, tm, SUBLANES)
    slab = _tile(D, slab, LANES)
    base, rpg = grp
    row = lambda i: (i, 0)
    row3 = lambda i: (0, i, 0)
    full = lambda i: (0, 0)
    return pl.pallas_call(
        functools.partial(_rwkv_out_kernel, slab=slab),
        grid=(M // tm,),
        in_specs=[pl.BlockSpec((2, tm, D), row3), pl.BlockSpec((tm, D), row), pl.BlockSpec((2, tm, D), row3),
                  pl.BlockSpec((tm, D), row), pl.BlockSpec((tm, D), row),
                  pl.BlockSpec((1, D), full), pl.BlockSpec((1, D), full), pl.BlockSpec((1, D), full),
                  pl.BlockSpec((D, D), full, pipeline_mode=pl.Buffered(1)), pl.BlockSpec((tm, D), row),
                  pl.BlockSpec((None, N_MOD, D), _mod_spec(base, rpg, tm)), pl.BlockSpec((1, D), full)],
        out_specs=pl.BlockSpec((tm, D), row),
        out_shape=jax.ShapeDtypeStruct((M, D), F32),
        scratch_shapes=[pltpu.VMEM((tm, D), BF16)],
        compiler_params=_cparams(("arbitrary",)),
        name="rwkv_out",
    )(y, q["r"], q["kd"], q["v"], q["g"], p["r_k"].reshape(1, D), p["gn_w"].reshape(1, D),
      p["gn_b"].reshape(1, D), w_o, x, mod, g_post.reshape(1, D))


def _adaln_kernel(c_ref, w_ref, b_ref, o_ref):
    cv = c_ref[...]
    cv = cv * jax.nn.sigmoid(cv)
    o_ref[...] = _dot(cv.astype(BF16), w_ref[...].astype(BF16)) + b_ref[...]


def _adaln(cvec, ada_w, ada_b, *, bn=2048):
    depth, D, N = ada_w.shape
    bn = _tile(N, bn, LANES)
    return pl.pallas_call(
        _adaln_kernel,
        grid=(depth, N // bn),
        in_specs=[pl.BlockSpec((8, D), lambda i, n: (0, 0)),
                  pl.BlockSpec((None, D, bn), lambda i, n: (i, 0, n)),
                  pl.BlockSpec((None, 1, bn), lambda i, n: (i, 0, n))],
        out_specs=pl.BlockSpec((None, 8, bn), lambda i, n: (i, 0, n)),
        out_shape=jax.ShapeDtypeStruct((depth, 8, N), F32),
        compiler_params=_cparams(("arbitrary", "arbitrary")),
        name="adaln",
    )(cvec, ada_w, ada_b.reshape(depth, 1, N))


def _ffn_kernel(x_ref, mod_ref, gpre_ref, gpost_ref, wg_ref, wu_ref, wd_ref, o_ref, *rest, slot, emit_bf16):
    h_scr = rest[-1]
    f = pl.program_id(1)

    @pl.when(f == 0)
    def _():
        _modulated_rows(x_ref, mod_ref, gpre_ref, slot, h_scr)
        o_ref[...] = jnp.zeros_like(o_ref)

    wg, wu, wd = wg_ref[...], wu_ref[...], wd_ref[...]
    if emit_bf16:
        wg, wu, wd = wg.astype(BF16), wu.astype(BF16), wd.astype(BF16)
        for ref, w in zip(rest[:3], (wg, wu, wd)):
            ref[...] = w
    h = h_scr[...]
    g = _dot(h, wg)
    u = _dot(h, wu)
    a = (g * jax.nn.sigmoid(g)) * u
    o_ref[...] += _dot(a.astype(BF16), wd)

    @pl.when(f == pl.num_programs(1) - 1)
    def _():
        gate = mod_ref[3 * slot + 2:3 * slot + 3, :]
        _gated_residual_rows(x_ref, o_ref, MACARON_WEIGHT * gate, gpost_ref, o_ref)


def _ffn(x, mod, grp, slot, g_pre, g_post, w_in, w_up, w_down, layer=None, which=None, *, tm=1024, tf=512):
    M, D = x.shape
    F = w_in.shape[-1]
    emit_bf16 = layer is not None
    tm = _tile(min(M, grp[1]), tm, SUBLANES)
    tf = _tile(F, tf // 2 if emit_bf16 else tf, LANES)
    base, rpg = grp
    if emit_bf16:
        assert M == tm
        up_spec = pl.BlockSpec((None, None, D, tf), lambda i, f: (layer, which, 0, f))
        down_spec = pl.BlockSpec((None, None, tf, D), lambda i, f: (layer, which, f, 0))
    else:
        up_spec = pl.BlockSpec((D, tf), lambda i, f: (0, f))
        down_spec = pl.BlockSpec((tf, D), lambda i, f: (f, 0))
    x_spec = pl.BlockSpec((tm, D), lambda i, f: (i, 0))
    out_specs = [x_spec]
    out_shape = [jax.ShapeDtypeStruct((M, D), F32)]
    if emit_bf16:
        x_spec = pl.BlockSpec((tm, D), lambda i, f: (i, 0), pipeline_mode=pl.Buffered(1))
        out_specs += [pl.BlockSpec((D, tf), lambda i, f: (0, f)), pl.BlockSpec((D, tf), lambda i, f: (0, f)),
                      pl.BlockSpec((tf, D), lambda i, f: (f, 0))]
        out_shape += [jax.ShapeDtypeStruct((D, F), BF16), jax.ShapeDtypeStruct((D, F), BF16),
                      jax.ShapeDtypeStruct((F, D), BF16)]
    out = pl.pallas_call(
        functools.partial(_ffn_kernel, slot=slot, emit_bf16=emit_bf16),
        grid=(M // tm, F // tf),
        in_specs=[x_spec,
                  pl.BlockSpec((None, N_MOD, D), _mod_spec(base, rpg, tm)),
                  pl.BlockSpec((1, D), lambda i, f: (0, 0)),
                  pl.BlockSpec((1, D), lambda i, f: (0, 0)),
                  up_spec, up_spec, down_spec],
        out_specs=out_specs,
        out_shape=out_shape,
        scratch_shapes=[pltpu.VMEM((tm, D), BF16)],
        compiler_params=_cparams(("arbitrary", "arbitrary")),
        name="ffn_cast" if emit_bf16 else "ffn",
    )(x, mod, g_pre.reshape(1, D), g_post.reshape(1, D), w_in, w_up, w_down)
    return (out[0], tuple(out[1:])) if emit_bf16 else out[0]


def _normmod_mm_kernel(x_ref, mod_ref, gpre_ref, w_ref, o_ref, h_scr, *, slot):
    @pl.when(pl.program_id(1) == 0)
    def _():
        _modulated_rows(x_ref, mod_ref, gpre_ref, slot, h_scr)

    o_ref[...] = _dot(h_scr[...], w_ref[...]).astype(o_ref.dtype)


def _normmod_mm(x, mod, grp, slot, g_pre, w, out_dtype, *, tm=1024, tn=1024):
    M, D = x.shape
    N = w.shape[1]
    tm = _tile(min(M, grp[1]), tm, SUBLANES)
    tn = _tile(N, tn, LANES)
    base, rpg = grp
    return pl.pallas_call(
        functools.partial(_normmod_mm_kernel, slot=slot),
        grid=(M // tm, N // tn),
        in_specs=[pl.BlockSpec((tm, D), lambda i, n: (i, 0)),
                  pl.BlockSpec((None, N_MOD, D), _mod_spec(base, rpg, tm)),
                  pl.BlockSpec((1, D), lambda i, n: (0, 0)),
                  pl.BlockSpec((D, tn), lambda i, n: (0, n))],
        out_specs=pl.BlockSpec((tm, tn), lambda i, n: (i, n)),
        out_shape=jax.ShapeDtypeStruct((M, N), out_dtype),
        scratch_shapes=[pltpu.VMEM((tm, D), BF16)],
        compiler_params=_cparams(("arbitrary", "arbitrary")),
        name="normmod_mm",
    )(x, mod, g_pre.reshape(1, D), w)


def _outproj_kernel(a_ref, w_ref, x_ref, mod_ref, gpost_ref, o_ref, *, slot):
    y = _dot(a_ref[...], w_ref[...])
    gate = mod_ref[3 * slot + 2:3 * slot + 3, :]
    o_ref[...] = x_ref[...] + gate * _rms(y, gpost_ref[...])


def _outproj(a, w, x, mod, grp, slot, g_post, *, tm=512):
    M, D = x.shape
    K = a.shape[1]
    tm = _tile(M, tm, SUBLANES)
    base, rpg = grp
    return pl.pallas_call(
        functools.partial(_outproj_kernel, slot=slot),
        grid=(M // tm,),
        in_specs=[pl.BlockSpec((tm, K), lambda i: (i, 0)),
                  pl.BlockSpec((K, D), lambda i: (0, 0), pipeline_mode=pl.Buffered(1)),
                  pl.BlockSpec((tm, D), lambda i: (i, 0)),
                  pl.BlockSpec((None, N_MOD, D), _mod_spec(base, rpg, tm)),
                  pl.BlockSpec((1, D), lambda i: (0, 0))],
        out_specs=pl.BlockSpec((tm, D), lambda i: (i, 0)),
        out_shape=jax.ShapeDtypeStruct((M, D), F32),
        compiler_params=_cparams(("arbitrary",)),
        name="outproj",
    )(a, w, x, mod, g_post.reshape(1, D))


def _nat_kernel(q_ref, k_ref, v_ref, kc_ref, vc_ref, bias_ref, o_ref, *, rows, group):
    scale = HEAD ** -0.5
    n_lat = WIN_H * GRID_W
    lane = lax.broadcasted_iota(jnp.int32, (1, LANES), 1)
    head_masks = (lane < HEAD, lane >= HEAD)
    qcol = lax.broadcasted_iota(jnp.int32, (2 * GRID_W, n_lat), 0) % GRID_W
    kcol = lax.broadcasted_iota(jnp.int32, (2 * GRID_W, n_lat), 1) % GRID_W
    cstart = jnp.clip(qcol - WIN_W // 2, 0, GRID_W - WIN_W)
    col_ok = (kcol >= cstart) & (kcol < cstart + WIN_W)
    kc = kc_ref[...]
    vc = vc_ref[...]
    zero = jnp.zeros((), BF16)

    def step(i, carry):
        chains = []
        for rr in range(group):
            r = i * group + rr
            r0 = jnp.clip(r - WIN_H // 2, 0, rows - WIN_H)
            q = q_ref[pl.ds(pl.multiple_of(r * GRID_W, GRID_W), GRID_W), :] * scale
            q2 = jnp.concatenate([jnp.where(head_masks[0], q, zero), jnp.where(head_masks[1], q, zero)], axis=0)
            kw = k_ref[pl.ds(pl.multiple_of(r0 * GRID_W, GRID_W), n_lat), :]
            vw = v_ref[pl.ds(pl.multiple_of(r0 * GRID_W, GRID_W), n_lat), :]
            chains.append((r, r0, q2, kw, vw))
        s = [_dot_nt(q2, kw) for (_, _, q2, kw, _) in chains]
        sc = [_dot_nt(q2, kc) for (_, _, q2, _, _) in chains]
        s = [jnp.where(col_ok, si + jnp.concatenate(
                [jnp.concatenate([bias_ref[h, 2 * jj - (r - r0) + WIN_H - 1] for jj in range(WIN_H // 2)], axis=1)
                 for h in range(2)], axis=0), MASK_VALUE)
             for si, (r, r0, _, _, _) in zip(s, chains)]
        m = [jnp.maximum(jnp.max(si, axis=-1, keepdims=True), jnp.max(ci, axis=-1, keepdims=True))
             for si, ci in zip(s, sc)]
        e = [jnp.exp(si - mi) for si, mi in zip(s, m)]
        ec = [jnp.exp(ci - mi) for ci, mi in zip(sc, m)]
        den = [jnp.sum(ei, axis=-1, keepdims=True) + jnp.sum(ci, axis=-1, keepdims=True) for ei, ci in zip(e, ec)]
        o = [(_dot(ei.astype(BF16), vw) + _dot(ci.astype(BF16), vc)) / di
             for ei, ci, di, (_, _, _, _, vw) in zip(e, ec, den, chains)]
        for rr in range(group):
            r = i * group + rr
            o_ref[pl.ds(pl.multiple_of(r * GRID_W, GRID_W), GRID_W), :] = jnp.where(
                head_masks[0], o[rr][:GRID_W], o[rr][GRID_W:]).astype(o_ref.dtype)
        return carry

    lax.fori_loop(0, rows // group, step, 0)


def _nat_bias_table(rpb):
    edge = GRID_W - WIN_W
    padded = jnp.pad(rpb, ((0, 0), (0, 0), (edge, edge)), mode="edge")
    rows = jnp.stack([padded[:, :, GRID_W - 1 - q:2 * GRID_W - 1 - q] for q in range(GRID_W)], axis=2)
    return jnp.concatenate([rows[:, :-1], rows[:, 1:]], axis=-1).astype(F32)


def _nat_attention(qkv, qkv_c, bias, *, group=8):
    B, T, D3 = qkv.shape
    D = D3 // 3
    C = qkv_c.shape[1]
    nd = D // LANES
    rows = T // GRID_W
    assert rows >= WIN_H and rows % group == 0
    return pl.pallas_call(
        functools.partial(_nat_kernel, rows=rows, group=group),
        grid=(B, nd),
        in_specs=[pl.BlockSpec((None, T, LANES), lambda b, p: (b, 0, p)),
                  pl.BlockSpec((None, T, LANES), lambda b, p: (b, 0, nd + p)),
                  pl.BlockSpec((None, T, LANES), lambda b, p: (b, 0, 2 * nd + p)),
                  pl.BlockSpec((None, C, LANES), lambda b, p: (b, 0, nd + p)),
                  pl.BlockSpec((None, C, LANES), lambda b, p: (b, 0, 2 * nd + p)),
                  pl.BlockSpec((2, 2 * WIN_H - 2, GRID_W, 2 * GRID_W), lambda b, p: (p, 0, 0, 0))],
        out_specs=pl.BlockSpec((None, T, LANES), lambda b, p: (b, 0, p)),
        out_shape=jax.ShapeDtypeStruct((B, T, D), BF16),
        compiler_params=_cparams(("arbitrary", "arbitrary")),
        name="nat_attention",
    )(qkv, qkv, qkv, qkv_c, qkv_c, bias)


def _ctx_attn_kernel(q_ref, k_ref, v_ref, o_ref):
    scale = HEAD ** -0.5
    lane = lax.broadcasted_iota(jnp.int32, (1, LANES), 1)
    head_masks = (lane < HEAD, lane >= HEAD)
    q = q_ref[...] * scale
    k = k_ref[...]
    v = v_ref[...]
    zero = jnp.zeros((), BF16)
    outs = []
    for h in range(2):
        s = _dot_nt(jnp.where(head_masks[h], q, zero), k)
        e = jnp.exp(s - jnp.max(s, axis=-1, keepdims=True))
        outs.append(_dot(e.astype(BF16), v) / jnp.sum(e, axis=-1, keepdims=True))
    o_ref[...] = jnp.where(head_masks[0], outs[0], outs[1]).astype(o_ref.dtype)


def _ctx_attention(qkv_c):
    B, C, D3 = qkv_c.shape
    D = D3 // 3
    nd = D // LANES
    return pl.pallas_call(
        _ctx_attn_kernel,
        grid=(B, nd),
        in_specs=[pl.BlockSpec((None, C, LANES), lambda b, p: (b, 0, p)),
                  pl.BlockSpec((None, C, LANES), lambda b, p: (b, 0, nd + p)),
                  pl.BlockSpec((None, C, LANES), lambda b, p: (b, 0, 2 * nd + p))],
        out_specs=pl.BlockSpec((None, C, LANES), lambda b, p: (b, 0, p)),
        out_shape=jax.ShapeDtypeStruct((B, C, D), BF16),
        compiler_params=_cparams(("arbitrary", "arbitrary")),
        name="ctx_attention",
    )(qkv_c, qkv_c, qkv_c)


def _pad_to(w, axis, mult=LANES):
    n = w.shape[axis]
    pad = (-n) % mult
    if pad == 0:
        return w
    widths = [(0, 0)] * w.ndim
    widths[axis] = (0, pad)
    return jnp.pad(w, widths)


def _col_tiles(w, tn=2 * LANES):
    Z, K, N = w.shape
    tn = _tile(N, tn, LANES)
    return jnp.swapaxes(w.astype(BF16).reshape(Z, K, N // tn, tn), 1, 2)


def _lora_in(w):
    w = _pad_to(w, 2)
    return jnp.concatenate(list(w), axis=1).astype(BF16)


def _lora_out(w):
    return _pad_to(w, 1).astype(BF16)


def _scan_inputs(q, B, L, D):
    three = lambda t: t.reshape(B, L, D)
    four = lambda t: t.reshape(2, B, L, D)
    return three(q["r"]), three(q["v"]), three(q["kk"]), four(q["lw"]), four(q["kd"]), four(q["a"])


def kernel(x, c, ctx, c_ctx, ada_w, ada_b, norm_pre, norm_post, ffn_w_gate, ffn_w_up, ffn_w_down, rwkv_mix, rwkv_w_r, rwkv_w_k, rwkv_w_v, rwkv_w_o, rwkv_w0, rwkv_w1, rwkv_w2, rwkv_a0, rwkv_a1, rwkv_a2, rwkv_v0, rwkv_v1, rwkv_v2, rwkv_k_k, rwkv_k_a, rwkv_r_k, rwkv_g1, rwkv_g2, rwkv_gn_w, rwkv_gn_b, nat_w_qkv, nat_w_o, nat_rpb):
    B, T, D = x.shape
    C = ctx.shape[1]
    depth = ada_w.shape[0]
    assert B + 1 <= 8 and D % (2 * LANES) == 0
    cvec = jnp.zeros((8, D), F32).at[:B].set(c).at[B].set(c_ctx)
    mods = _adaln(cvec, ada_w, ada_b).reshape(depth, 8, N_MOD, D)
    ffn_w = (ffn_w_gate, ffn_w_up, ffn_w_down)
    grp_l, grp_c = (0, T), (B, B * C)
    xl = x.reshape(B * T, D)
    xc = ctx.reshape(B * C, D)
    vf_l = vf_c = None

    def ffn_both(xl, xc, mod, i, slot, which, with_ctx):
        args = (slot, norm_pre[i, slot], norm_post[i, slot])
        if with_ctx and B * C <= 1024:
            xc, w = _ffn(xc, mod, grp_c, *args, *ffn_w, layer=i, which=which)
        else:
            w = tuple(t[i, which].astype(BF16) for t in ffn_w)
            if with_ctx:
                xc = _ffn(xc, mod, grp_c, *args, *w)
        return _ffn(xl, mod, grp_l, *args, *w), xc

    for i in range(depth):
        last = i == depth - 1
        j = i // 2
        mod = mods[i]
        xl, xc = ffn_both(xl, xc, mod, i, 0, 0, True)
        if i % 2 == 0:
            p = dict(
                mix=rwkv_mix[j], w_rkv=_col_tiles(jnp.stack([rwkv_w_r[j], rwkv_w_k[j], rwkv_w_v[j]])),
                w0=rwkv_w0[j], w1=_lora_in(rwkv_w1[j]), w2=_lora_out(rwkv_w2[j]),
                a0=rwkv_a0[j], a1=_lora_in(rwkv_a1[j]), a2=_lora_out(rwkv_a2[j]),
                k_k=rwkv_k_k[j], k_a=rwkv_k_a[j], r_k=rwkv_r_k[j].reshape(D),
                g1=_lora_in(rwkv_g1[j][None]), g2=_lora_out(rwkv_g2[j][None]),
                gn_w=rwkv_gn_w[j], gn_b=rwkv_gn_b[j],
                v_res=None if j == 0 else (rwkv_v0[j - 1], _lora_in(rwkv_v1[j - 1][None]), _lora_out(rwkv_v2[j - 1][None])))
            q_c = _rwkv_prep(xc, mod, grp_c, norm_pre[i, 1], p, vf_c, C)
            q_l = _rwkv_prep(xl, mod, grp_l, norm_pre[i, 1], p, vf_l, T)
            if j == 0:
                vf_l, vf_c = q_l["v"], q_c["v"]
            s0 = jnp.zeros((2, B, D // LANES, LANES, LANES), F32)
            y_c, s_c = _wkv_scan(*_scan_inputs(q_c, B, C, D), s0)
            y_l, _ = _wkv_scan(*_scan_inputs(q_l, B, T, D), s_c)
            w_o = rwkv_w_o[j].astype(BF16)
            xl = _rwkv_out(y_l.reshape(2, B * T, D), q_l, p, w_o, xl, mod, grp_l, norm_post[i, 1])
            if not last:
                xc = _rwkv_out(y_c.reshape(2, B * C, D), q_c, p, w_o, xc, mod, grp_c, norm_post[i, 1])
        else:
            w_qkv = nat_w_qkv[j].astype(BF16)
            w_o = nat_w_o[j].astype(BF16)
            qkv_l = _normmod_mm(xl, mod, grp_l, 1, norm_pre[i, 1], w_qkv, BF16).reshape(B, T, 3 * D)
            qkv_c = _normmod_mm(xc, mod, grp_c, 1, norm_pre[i, 1], w_qkv, BF16).reshape(B, C, 3 * D)
            a_l = _nat_attention(qkv_l, qkv_c, _nat_bias_table(nat_rpb[j])).reshape(B * T, D)
            xl = _outproj(a_l, w_o, xl, mod, grp_l, 1, norm_post[i, 1])
            if not last:
                a_c = _ctx_attention(qkv_c).reshape(B * C, D)
                xc = _outproj(a_c, w_o, xc, mod, grp_c, 1, norm_post[i, 1])
        xl, xc = ffn_both(xl, xc, mod, i, 2, 1, not last)
    return xl.reshape(B, T, D)
```
